```python
import math
import jax, jax.numpy as jnp
from jax import lax
import numpy as np

D_MODEL = 1024
BATCH = 16
SEQ = 256
DEPTH = 4
DEC_BATCH = 4
DEC_SEQ = 1024
PAST_LEN = 512

GRID_W = 64
N_EVEN = (DEPTH + 1) // 2
N_ODD = DEPTH // 2
SSD_HEADS = 16
SSD_HD = 64
SSD_INNER = SSD_HEADS * SSD_HD
SSD_GROUPS = 4
HEADS_PER_GROUP = SSD_HEADS // SSD_GROUPS
D_STATE = 128
CONV_W = 5
CONV_CH = SSD_INNER + 2 * SSD_GROUPS * D_STATE
CHUNK = 128
ATT_HEADS = 8
ATT_HD = 64
ATT_VD = 2 * ATT_HD
ATT_INNER = ATT_HEADS * ATT_VD
Q_BLOCK = 128
ROPE_THETA = 10000.0
ROPE_PAIRS = ATT_HD // 4
FOUR_GROUPS = 4
FFN_HIDDEN = -(-8 * D_MODEL // (3 * 256)) * 256
IN_AB = SSD_INNER + CONV_CH + 2 * SSD_HEADS + 3 * ATT_INNER
OUT_AB = SSD_INNER + ATT_INNER
EPS = 1e-6

kernel_name = "hybrid_ssd_diffattn_fnet_diffusion_step"


def rmsnorm(x, g):
    xf = x.astype(jnp.float32)
    y = xf * lax.rsqrt(jnp.mean(xf * xf, axis=-1, keepdims=True) + EPS)
    return (y * g.astype(jnp.float32)).astype(x.dtype)


def modulate(x, g, shift, scale):
    return rmsnorm(x, g) * (1 + scale) + shift


def depthwise_conv(x, w, b):
    y = lax.conv_general_dilated(
        x, w.astype(x.dtype)[:, None, :], window_strides=(1,),
        padding=[(CONV_W // 2, CONV_W // 2)],
        dimension_numbers=("NWC", "WIO", "NWC"),
        feature_group_count=x.shape[-1])
    return y + b


def ssd_scan(x, dt, a, bm, cm, h0):
    bsz, L = x.shape[:2]
    q = math.gcd(L, CHUNK)
    nc = L // q
    G, R, P, N = SSD_GROUPS, HEADS_PER_GROUP, SSD_HD, D_STATE
    xf = x.astype(jnp.float32).reshape(bsz, nc, q, G, R, P)
    dtc = dt.reshape(bsz, nc, q, G, R)
    bc = bm.astype(jnp.float32).reshape(bsz, nc, q, G, N)
    cc = cm.astype(jnp.float32).reshape(bsz, nc, q, G, N)
    acs = jnp.cumsum(dtc * a, axis=2)
    seg = acs[:, :, :, None] - acs[:, :, None, :]
    lower = jnp.tril(jnp.ones((q, q), dtype=bool))[None, None, :, :, None, None]
    decay = jnp.exp(jnp.where(lower, seg, -jnp.inf))
    cb = jnp.einsum("bcign,bcjgn->bcijg", cc, bc)
    w_ij = cb[..., None] * decay * dtc[:, :, None]
    y_in = jnp.einsum("bcijgr,bcjgrp->bcigrp", w_ij, xf)
    to_end = jnp.exp(acs[:, :, -1:] - acs) * dtc
    st = jnp.einsum("bcjgrp,bcjgn->bcgrpn", to_end[..., None] * xf, bc)
    chunk_decay = jnp.exp(acs[:, :, -1])

    def step(h, inp):
        dec, s = inp
        return dec[..., None, None] * h + s, h

    h_last, h_prev = lax.scan(step, h0.astype(jnp.float32),
                              (jnp.moveaxis(chunk_decay, 1, 0), jnp.moveaxis(st, 1, 0)))
    h_prev = jnp.moveaxis(h_prev, 0, 1)
    y_out = jnp.einsum("bcign,bcgrpn->bcigrp", cc, h_prev) * jnp.exp(acs)[..., None]
    return (y_in + y_out).reshape(bsz, L, G, R, P), h_last


def rope_tables(L):
    rows = L // GRID_W
    t = jnp.arange(rows * GRID_W)
    row = (t // GRID_W).astype(jnp.float32)
    col = (t % GRID_W).astype(jnp.float32)
    freq = ROPE_THETA ** (-jnp.arange(ROPE_PAIRS, dtype=jnp.float32) / ROPE_PAIRS)
    ar = row[:, None] * freq[None]
    ac = col[:, None] * freq[None]
    return (jnp.cos(ar), jnp.sin(ar), jnp.cos(ac), jnp.sin(ac))


def _rot(x, cos, sin):
    a, b = x[..., :ROPE_PAIRS], x[..., ROPE_PAIRS:]
    cos = cos[None, :, None, None, :]
    sin = sin[None, :, None, None, :]
    return jnp.concatenate([a * cos - b * sin, a * sin + b * cos], axis=-1)


def rope2d(x, rope):
    cr, sr, ccol, scol = rope
    half = ATT_HD // 2
    return jnp.concatenate([_rot(x[..., :half], cr, sr), _rot(x[..., half:], ccol, scol)],
                           axis=-1).astype(x.dtype)


def diff_attend(q, k, v, lam):
    bsz, lq = q.shape[:2]
    blk = math.gcd(lq, Q_BLOCK)
    nb = lq // blk
    qb = jnp.moveaxis(q.reshape(bsz, nb, blk, ATT_HEADS, 2, ATT_HD), 1, 0)
    scale = ATT_HD ** -0.5

    def one(qblk):
        s = jnp.einsum("bqhmd,bkhmd->bhmqk", qblk, k).astype(jnp.float32) * scale
        p = jax.nn.softmax(s, axis=-1)
        pd = p[:, :, 0] - lam * p[:, :, 1]
        return jnp.einsum("bhqk,bkhe->bqhe", pd.astype(v.dtype), v)

    o = lax.map(one, qb)
    return jnp.moveaxis(o, 0, 1).reshape(bsz, lq, ATT_HEADS, ATT_VD)


def ab_mixer(h, w_in, conv_w, conv_b, dt_bias, a_log, d_skip, ssd_norm, lam_qk, subln,
             w_out, lambda_init, h0_f, h0_b, ctx_k=None, ctx_v=None, rope=None):
    bsz, L, _ = h.shape
    G, R, P, N = SSD_GROUPS, HEADS_PER_GROUP, SSD_HD, D_STATE
    proj = h @ w_in
    o1 = SSD_INNER
    o2 = o1 + CONV_CH
    o3 = o2 + 2 * SSD_HEADS
    o4 = o3 + ATT_INNER
    o5 = o4 + ATT_INNER
    z, xbc, dt_raw, q, k, v = jnp.split(proj, [o1, o2, o3, o4, o5], axis=-1)

    xbc = jax.nn.silu(depthwise_conv(xbc, conv_w, conv_b))
    xs, bm, cm = jnp.split(xbc, [SSD_INNER, SSD_INNER + G * N], axis=-1)
    xs = xs.reshape(bsz, L, G, R, P)
    bm = bm.reshape(bsz, L, G, N)
    cm = cm.reshape(bsz, L, G, N)
    dt = jax.nn.softplus(dt_raw.astype(jnp.float32).reshape(bsz, L, 2, SSD_HEADS)
                         + dt_bias.astype(jnp.float32)).reshape(bsz, L, 2, G, R)
    a = -jnp.exp(a_log.astype(jnp.float32)).reshape(2, G, R)
    y_f, hf = ssd_scan(xs, dt[:, :, 0], a[0], bm, cm, h0_f.reshape(bsz, G, R, P, N))
    y_b, hb = ssd_scan(jnp.flip(xs, 1), jnp.flip(dt[:, :, 1], 1), a[1],
                       jnp.flip(bm, 1), jnp.flip(cm, 1), h0_b.reshape(bsz, G, R, P, N))
    y = y_f + jnp.flip(y_b, 1) + d_skip.reshape(G, R)[..., None] * xs
    y = y.reshape(bsz, L, SSD_INNER).astype(h.dtype)
    y_ssd = rmsnorm(y * jax.nn.silu(z), ssd_norm)

    q = q.reshape(bsz, L, ATT_HEADS, 2, ATT_HD)
    k = k.reshape(bsz, L, ATT_HEADS, 2, ATT_HD)
    v = v.reshape(bsz, L, ATT_HEADS, ATT_VD)
    if rope is not None:
        q_att = rope2d(q, rope)
        k_att = rope2d(k, rope)
    else:
        q_att, k_att = q, k
    if ctx_k is not None:
        k_all = jnp.concatenate([ctx_k.astype(k_att.dtype), k_att], axis=1)
        v_all = jnp.concatenate([ctx_v.astype(v.dtype), v], axis=1)
    else:
        k_all, v_all = k_att, v
    lq = lam_qk.astype(jnp.float32)
    lam = jnp.exp(jnp.sum(lq[0] * lq[1])) - jnp.exp(jnp.sum(lq[2] * lq[3])) + lambda_init
    o = diff_attend(q_att, k_all, v_all, lam)
    o = rmsnorm(o, subln) * (1.0 - lambda_init)

    out = jnp.concatenate([y_ssd, o.reshape(bsz, L, ATT_INNER)], axis=-1) @ w_out
    hf = hf.reshape(bsz, SSD_HEADS, P, N).astype(h.dtype)
    hb = hb.reshape(bsz, SSD_HEADS, P, N).astype(h.dtype)
    return out, k, v, hf, hb


def fourier_mix(h, w, b):
    bsz, L, D = h.shape
    hg = h.reshape(bsz, L, FOUR_GROUPS, D // FOUR_GROUPS).astype(jnp.float32)
    f = jnp.fft.fftn(hg, axes=(1, 3), norm="ortho").real
    return f.reshape(bsz, L, D).astype(h.dtype) @ w + b


def swiglu(h, w_in, w_out):
    g, u = jnp.split(h @ w_in, 2, axis=-1)
    return (jax.nn.silu(g) * u) @ w_out


def setup_inputs(seed: int = 0) -> dict:
    key = jax.random.key(seed)
    ks = jax.random.split(key, 32)
    f32 = jnp.float32

    def nrm(k, shape, scale):
        return jax.random.normal(k, shape, f32) * scale

    dt0 = jnp.exp(jax.random.uniform(ks[12], (N_EVEN, 2, SSD_HEADS), f32,
                                     math.log(1e-3), math.log(1e-1)))
    dt_bias = dt0 + jnp.log(-jnp.expm1(-dt0))
    a_log = jnp.log(jax.random.uniform(ks[13], (N_EVEN, 2, SSD_HEADS), f32, 1.0, 16.0))
    return {
        "x_prompt": nrm(ks[0], (BATCH, SEQ, D_MODEL), 1.0),
        "x_sample": nrm(ks[1], (DEC_BATCH, DEC_SEQ, D_MODEL), 1.0),
        "cache_k": nrm(ks[2], (DEC_BATCH, N_EVEN, PAST_LEN, ATT_HEADS, 2, ATT_HD), 1.0),
        "cache_v": nrm(ks[3], (DEC_BATCH, N_EVEN, PAST_LEN, ATT_HEADS, ATT_VD), 1.0),
        "state_ssd_fwd": nrm(ks[4], (DEC_BATCH, N_EVEN, SSD_HEADS, SSD_HD, D_STATE), 0.05),
        "state_ssd_bwd": nrm(ks[5], (DEC_BATCH, N_EVEN, SSD_HEADS, SSD_HD, D_STATE), 0.05),
        "c": nrm(ks[6], (DEC_BATCH, D_MODEL), 1.0),
        "c_ctx": nrm(ks[7], (D_MODEL,), 1.0),
        "w_ada": nrm(ks[8], (DEPTH, D_MODEL, 6 * D_MODEL), 0.5 * D_MODEL ** -0.5),
        "b_ada": nrm(ks[9], (DEPTH, 6 * D_MODEL), 0.01),
        "norm_mix": 1.0 + nrm(ks[10], (DEPTH, D_MODEL), 0.02),
        "norm_ffn": 1.0 + nrm(ks[11], (DEPTH, D_MODEL), 0.02),
        "w_in_ab": nrm(ks[14], (N_EVEN, D_MODEL, IN_AB), D_MODEL ** -0.5),
        "conv_w": nrm(ks[15], (N_EVEN, CONV_W, CONV_CH), CONV_W ** -0.5),
        "conv_b": nrm(ks[16], (N_EVEN, CONV_CH), 0.02),
        "dt_bias": dt_bias,
        "a_log": a_log,
        "d_skip": 1.0 + nrm(ks[17], (N_EVEN, SSD_HEADS), 0.1),
        "ssd_norm": 1.0 + nrm(ks[18], (N_EVEN, SSD_INNER), 0.02),
        "lambda_qk": nrm(ks[19], (N_EVEN, 4, ATT_HD), 0.1),
        "subln": 1.0 + nrm(ks[20], (N_EVEN, ATT_VD), 0.02),
        "w_out_ab": nrm(ks[21], (N_EVEN, OUT_AB, D_MODEL), OUT_AB ** -0.5),
        "w_four": nrm(ks[22], (N_ODD, D_MODEL, D_MODEL), D_MODEL ** -0.5),
        "b_four": nrm(ks[23], (N_ODD, D_MODEL), 0.01),
        "w_ffn_in": nrm(ks[24], (DEPTH, D_MODEL, 2 * FFN_HIDDEN), D_MODEL ** -0.5),
        "w_ffn_out": nrm(ks[25], (DEPTH, FFN_HIDDEN, D_MODEL), FFN_HIDDEN ** -0.5),
        "norm_final": 1.0 + nrm(ks[26], (D_MODEL,), 0.02),
    }


def reference(x_prompt, x_sample, cache_k, cache_v, state_ssd_fwd, state_ssd_bwd, c, c_ctx,
              w_ada, b_ada, norm_mix, norm_ffn, w_in_ab, conv_w, conv_b, dt_bias, a_log,
              d_skip, ssd_norm, lambda_qk, subln, w_out_ab, w_four, b_four, w_ffn_in,
              w_ffn_out, norm_final):
    xp, xs = x_prompt, x_sample
    bp = xp.shape[0]
    rope = rope_tables(xs.shape[1])
    silu_ctx = jax.nn.silu(c_ctx)
    silu_c = jax.nn.silu(c)
    zero_state = jnp.zeros((bp, SSD_HEADS, SSD_HD, D_STATE), xp.dtype)
    new_k, new_v, new_f, new_b = [], [], [], []
    for l in range(DEPTH):
        mp = silu_ctx @ w_ada[l] + b_ada[l]
        ms = (silu_c @ w_ada[l] + b_ada[l])[:, None, :]
        sh1p, sc1p, g1p, sh2p, sc2p, g2p = jnp.split(mp, 6, axis=-1)
        sh1s, sc1s, g1s, sh2s, sc2s, g2s = jnp.split(ms, 6, axis=-1)
        hp = modulate(xp, norm_mix[l], sh1p, sc1p)
        hs = modulate(xs, norm_mix[l], sh1s, sc1s)
        if l % 2 == 0:
            e = l // 2
            lambda_init = 0.8 - 0.6 * math.exp(-0.3 * l)
            w = (w_in_ab[e], conv_w[e], conv_b[e], dt_bias[e], a_log[e], d_skip[e],
                 ssd_norm[e], lambda_qk[e], subln[e], w_out_ab[e], lambda_init)
            op, kp, vp, hfp, hbp = ab_mixer(hp, *w, zero_state, zero_state)
            os_, _, _, _, _ = ab_mixer(hs, *w, state_ssd_fwd[:, e], state_ssd_bwd[:, e],
                                       cache_k[:, e], cache_v[:, e], rope)
            new_k.append(kp)
            new_v.append(vp)
            new_f.append(hfp)
            new_b.append(hbp)
        else:
            o_i = l // 2
            op = fourier_mix(hp, w_four[o_i], b_four[o_i])
            os_ = fourier_mix(hs, w_four[o_i], b_four[o_i])
        xp = xp + g1p * op
        xs = xs + g1s * os_
        xp = xp + g2p * swiglu(modulate(xp, norm_ffn[l], sh2p, sc2p), w_ffn_in[l], w_ffn_out[l])
        xs = xs + g2s * swiglu(modulate(xs, norm_ffn[l], sh2s, sc2s), w_ffn_in[l], w_ffn_out[l])
    y_prompt = rmsnorm(xp, norm_final)
    y_sample = rmsnorm(xs, norm_final)
    return (y_prompt, y_sample, jnp.stack(new_k, axis=1), jnp.stack(new_v, axis=1),
            jnp.stack(new_f, axis=1), jnp.stack(new_b, axis=1))
```

```python
import functools
import math

import numpy as np
import jax
import jax.numpy as jnp
from jax import lax
from jax.experimental import pallas as pl
from jax.experimental.pallas import tpu as pltpu

D_MODEL = 1024
BATCH = 16
SEQ = 256
DEPTH = 4
DEC_BATCH = 4
DEC_SEQ = 1024
PAST_LEN = 512
GRID_W = 64
SSD_HEADS = 16
SSD_HD = 64
SSD_INNER = 1024
SSD_GROUPS = 4
D_STATE = 128
CONV_W = 5
CONV_CH = 2048
CHUNK = 128
ATT_HEADS = 8
ATT_HD = 64
ATT_VD = 128
ATT_INNER = 1024
ROPE_THETA = 10000.0
ROPE_PAIRS = 16
FOUR_GROUPS = 4
FOUR_GC = D_MODEL // FOUR_GROUPS
FFN_HIDDEN = 2816
EPS = 1e-6

P_TOK = BATCH * SEQ
S_TOK = DEC_BATCH * DEC_SEQ
N_TOK = P_TOK + S_TOK
N_MOD = 8
PROJ_W = 6 * 1024
COL_Z, COL_Q, COL_K, COL_V = 2048, 3072, 4096, 5120
DT_W = 128
VMEM_LIMIT = 56 * 1024 * 1024

F32 = jnp.float32
BF16 = jnp.bfloat16


def _params(*sem):
    return pltpu.CompilerParams(dimension_semantics=sem, vmem_limit_bytes=VMEM_LIMIT)


def _dot(a, b):
    return jnp.dot(a, b, preferred_element_type=F32)


def _dot_nt(a, b):
    return lax.dot_general(a, b, (((1,), (1,)), ((), ())), preferred_element_type=F32)


def _dot_tn(a, b):
    return lax.dot_general(a, b, (((0,), (0,)), ((), ())), preferred_element_type=F32)


def _silu(x):
    return x * jax.nn.sigmoid(x)


def _rms(x):
    return x * lax.rsqrt(jnp.mean(x * x, axis=-1, keepdims=True) + EPS)


def _mod_row(tm):
    return lambda i: jnp.maximum((i * tm) // DEC_SEQ - (P_TOK // DEC_SEQ - 1), 0)


def _ada_kernel(c_ref, w_ref, b_ref, o_ref):
    s = _silu(c_ref[...]).astype(BF16)
    o_ref[...] = _dot(s, w_ref[...].astype(BF16)) + b_ref[...]


def _ada_call(cvec, w_ada, b_ada):
    tn = 1024
    return pl.pallas_call(
        _ada_kernel,
        grid=(DEPTH, 6 * D_MODEL // tn),
        in_specs=[
            pl.BlockSpec((N_MOD, D_MODEL), lambda l, j: (0, 0)),
            pl.BlockSpec((None, D_MODEL, tn), lambda l, j: (l, 0, j)),
            pl.BlockSpec((None, 1, tn), lambda l, j: (l, 0, j)),
        ],
        out_specs=pl.BlockSpec((None, N_MOD, tn), lambda l, j: (l, 0, j)),
        out_shape=jax.ShapeDtypeStruct((DEPTH, N_MOD, 6 * D_MODEL), F32),
        compiler_params=_params("parallel", "parallel"),
        name="adaln",
    )(cvec, w_ada, b_ada.reshape(DEPTH, 1, 6 * D_MODEL))


def _inproj_kernel(x_ref, mod_ref, g_ref, w_ref, wdt_ref, proj_ref, dt_ref, h_ref):
    @pl.when(pl.program_id(1) == 0)
    def _():
        h = _rms(x_ref[...]) * g_ref[...] * (1.0 + mod_ref[1:2, :]) + mod_ref[0:1, :]
        hb = h.astype(BF16)
        h_ref[...] = hb
        dt_ref[...] = _dot(hb, wdt_ref[...])

    proj_ref[...] = _dot(h_ref[...], w_ref[...])


def _inproj_call(x, mod, l, gain, w_main, w_dt):
    tm, tn = 1024, 1024
    row = _mod_row(tm)
    return pl.pallas_call(
        _inproj_kernel,
        grid=(N_TOK // tm, PROJ_W // tn),
        in_specs=[
            pl.BlockSpec((tm, D_MODEL), lambda i, j: (i, 0)),
            pl.BlockSpec((None, None, 6, D_MODEL), lambda i, j: (l, row(i), 0, 0)),
            pl.BlockSpec((1, D_MODEL), lambda i, j: (0, 0)),
            pl.BlockSpec((D_MODEL, tn), lambda i, j: (0, j)),
            pl.BlockSpec((D_MODEL, DT_W), lambda i, j: (0, 0)),
        ],
        out_specs=[
            pl.BlockSpec((tm, tn), lambda i, j: (i, j)),
            pl.BlockSpec((tm, DT_W), lambda i, j: (i, 0)),
        ],
        out_shape=[
            jax.ShapeDtypeStruct((N_TOK, PROJ_W), F32),
            jax.ShapeDtypeStruct((N_TOK, DT_W), F32),
        ],
        scratch_shapes=[pltpu.VMEM((tm, D_MODEL), BF16)],
        compiler_params=_params("parallel", "arbitrary"),
        name="inproj",
    )(x, mod, gain, w_main, w_dt)


def _ssd_kernel(*refs, L, has_h0, emit_state):
    it = iter(refs)
    xbc_ref, z_ref, dt_ref = next(it), next(it), next(it)
    convw_ref, convb_ref, dtb_ref, alog_ref, dskip_ref, norm_ref = (next(it) for _ in range(6))
    if has_h0:
        h0f_ref, h0b_ref = next(it), next(it)
    y_ref = next(it)
    if emit_state:
        hf_ref, hb_ref = next(it), next(it)
    xpad, xact, bact, cact, yacc, dtv, state = (next(it) for _ in range(7))

    nc = L // CHUNK
    nh2 = SSD_HEADS // 2

    xpad[0:8, :] = jnp.zeros((8, CONV_CH), F32)
    xpad[L + 8:L + 16, :] = jnp.zeros((8, CONV_CH), F32)
    for c in range(nc):
        xpad[8 + c * CHUNK:8 + (c + 1) * CHUNK, :] = xbc_ref[c * CHUNK:(c + 1) * CHUNK, :]
    slab = 512
    for c in range(nc):
        for s in range(CONV_CH // slab):
            ls = slice(s * slab, (s + 1) * slab)
            acc = jnp.broadcast_to(convb_ref[:, ls], (CHUNK, slab))
            for k in range(CONV_W):
                r0 = 8 + c * CHUNK + k - CONV_W // 2
                acc = acc + convw_ref[k:k + 1, ls] * xpad[r0:r0 + CHUNK, ls]
            act = _silu(acc)
            rows = slice(c * CHUNK, (c + 1) * CHUNK)
            if s < 2:
                xact[rows, ls] = act
            elif s == 2:
                bact[rows, :] = act.astype(BF16)
            else:
                cact[rows, :] = act.astype(BF16)

    dr = dt_ref[...] + dtb_ref[...]
    dtv[...] = jnp.maximum(dr, 0.0) + jnp.log1p(jnp.exp(-jnp.abs(dr)))
    a_row = -jnp.exp(alog_ref[...])

    ri = lax.broadcasted_iota(jnp.int32, (CHUNK, CHUNK), 0)
    ci = lax.broadcasted_iota(jnp.int32, (CHUNK, CHUNK), 1)
    low_half = ci < SSD_HD
    top_half = ri < SSD_HD

    def run_direction(fwd):
        mask = (ri >= ci) if fwd else (ri <= ci)
        tri = mask.astype(F32)
        col0 = 0 if fwd else SSD_HEADS
        last = CHUNK - 1 if fwd else 0

        if has_h0:
            src = h0f_ref if fwd else h0b_ref
            for m in range(nh2):
                state[m] = src[m]
        else:
            for m in range(nh2):
                state[m] = jnp.zeros((CHUNK, D_STATE), F32)

        def chunk_body(step, carry):
            c = step if fwd else nc - 1 - step
            r0 = pl.multiple_of(c * CHUNK, CHUNK)
            rows = pl.ds(r0, CHUNK)
            dtc = dtv[rows, :]
            acs = jnp.dot(tri, dtc * a_row, preferred_element_type=F32,
                          precision=lax.Precision.HIGHEST)
            acs_t = acs.T
            dt_t = dtc.T
            acs_last = acs[last:last + 1, :]
            to_end = jnp.exp(acs_last - acs) * dtc
            e_acs = jnp.exp(acs)
            c_dec = jnp.exp(acs_last)
            for g in range(SSD_GROUPS):
                bg = bact[rows, g * D_STATE:(g + 1) * D_STATE]
                cg = cact[rows, g * D_STATE:(g + 1) * D_STATE]
                cb = _dot_nt(cg, bg)
                for pr in range(2):
                    m = g * 2 + pr
                    ha, hb = col0 + 2 * m, col0 + 2 * m + 1
                    lanes = slice(m * 128, (m + 1) * 128)
                    x2 = xact[rows, lanes]
                    x2b = x2.astype(BF16)

                    def intra(col):
                        seg = acs[:, col:col + 1] - acs_t[col:col + 1, :]
                        dec = jnp.exp(jnp.where(mask, seg, -jnp.inf))
                        w = cb * dec * dt_t[col:col + 1, :]
                        return _dot(w.astype(BF16), x2b)

                    y_in = jnp.where(low_half, intra(ha), intra(hb))
                    h_prev = state[m]
                    e2 = jnp.where(low_half, e_acs[:, ha:ha + 1], e_acs[:, hb:hb + 1])
                    y_out = _dot_nt(cg, h_prev.astype(BF16)) * e2
                    t2 = jnp.where(low_half, to_end[:, ha:ha + 1], to_end[:, hb:hb + 1])
                    st = _dot_tn((t2 * x2).astype(BF16), bg)
                    d2 = jnp.where(top_half, c_dec[:, ha:ha + 1], c_dec[:, hb:hb + 1])
                    state[m] = d2 * h_prev + st
                    y2 = y_in + y_out
                    if fwd:
                        yacc[rows, lanes] = y2 + dskip_ref[:, lanes] * x2
                    else:
                        yacc[rows, lanes] = yacc[rows, lanes] + y2
            if not fwd:
                yt = yacc[rows, :] * _silu(z_ref[rows, :])
                y_ref[rows, :] = (_rms(yt) * norm_ref[...]).astype(y_ref.dtype)
            return carry

        lax.fori_loop(0, nc, chunk_body, 0)
        if emit_state:
            dst = hf_ref if fwd else hb_ref
            for m in range(nh2):
                dst[m] = state[m]

    run_direction(True)
    run_direction(False)


def _ssd_call(proj, dtarr, y_prev, conv_w8, conv_b, dt_bias, a_log, dskip, norm, *,
              L, nb, blk0, h0=None, emit_state=False):
    nblk = N_TOK // L
    proj3 = proj.reshape(nblk, L, PROJ_W)
    dt3 = dtarr.reshape(nblk, L, DT_W)
    vec = lambda w: pl.BlockSpec((1, w), lambda b: (0, 0))
    in_specs = [
        pl.BlockSpec((None, L, CONV_CH), lambda b: (blk0 + b, 0, 0)),
        pl.BlockSpec((None, L, SSD_INNER), lambda b: (blk0 + b, 0, COL_Z // SSD_INNER)),
        pl.BlockSpec((None, L, DT_W), lambda b: (blk0 + b, 0, 0)),
        pl.BlockSpec((8, CONV_CH), lambda b: (0, 0)),
        vec(CONV_CH), vec(DT_W), vec(DT_W), vec(SSD_INNER), vec(SSD_INNER),
    ]
    args = [proj3, proj3, dt3, conv_w8, conv_b, dt_bias, a_log, dskip, norm]
    if h0 is not None:
        h0f, h0b, e = h0
        for h in (h0f, h0b):
            args.append(h.reshape(DEC_BATCH, h.shape[1], SSD_HEADS // 2, CHUNK, D_STATE))
            in_specs.append(pl.BlockSpec((None, None, SSD_HEADS // 2, CHUNK, D_STATE),
                                         lambda b: (b, e, 0, 0, 0)))
    io_alias = {}
    if y_prev is not None:
        io_alias = {len(args): 0}
        args.append(y_prev.reshape(nblk, L, SSD_INNER))
        in_specs.append(pl.BlockSpec(memory_space=pl.ANY))
    out_specs = [pl.BlockSpec((None, L, SSD_INNER), lambda b: (blk0 + b, 0, 0))]
    out_shape = [jax.ShapeDtypeStruct((nblk, L, SSD_INNER), BF16)]
    if emit_state:
        for _ in range(2):
            out_specs.append(pl.BlockSpec((None, SSD_HEADS // 2, CHUNK, D_STATE),
                                          lambda b: (b, 0, 0, 0)))
            out_shape.append(jax.ShapeDtypeStruct((nb, SSD_HEADS // 2, CHUNK, D_STATE), F32))
    kern = functools.partial(_ssd_kernel, L=L, has_h0=h0 is not None, emit_state=emit_state)
    if y_prev is not None:
        kern = _drop_arg(kern, len(args) - 1)
    outs = list(pl.pallas_call(
        kern,
        grid=(nb,),
        in_specs=in_specs,
        out_specs=out_specs,
        out_shape=out_shape,
        scratch_shapes=[
            pltpu.VMEM((L + 16, CONV_CH), F32),
            pltpu.VMEM((L, SSD_INNER), F32),
            pltpu.VMEM((L, SSD_GROUPS * D_STATE), BF16),
            pltpu.VMEM((L, SSD_GROUPS * D_STATE), BF16),
            pltpu.VMEM((L, SSD_INNER), F32),
            pltpu.VMEM((L, DT_W), F32),
            pltpu.VMEM((SSD_HEADS // 2, CHUNK, D_STATE), F32),
        ],
        input_output_aliases=io_alias,
        compiler_params=_params("parallel"),
        name=f"ssd_L{L}",
    )(*args))
    outs[0] = outs[0].reshape(N_TOK, SSD_INNER)
    return outs


def _drop_arg(kern, idx):
    def wrapped(*refs):
        return kern(*(refs[:idx] + refs[idx + 1:]))
    return wrapped


def _rope(x, cos, sin_a, sin_b):
    return (x * cos + pltpu.roll(x, 128 - ROPE_PAIRS, 1) * sin_a
            + pltpu.roll(x, ROPE_PAIRS, 1) * sin_b)


def _attn_kernel(*refs, L, n_ctx, lam_init):
    it = iter(refs)
    q_ref, k_ref, v_ref = next(it), next(it), next(it)
    if n_ctx:
        ck_ref, cv_ref = next(it), next(it)
        cq, saq, sbq, ck, sak, sbk = (next(it) for _ in range(6))
    lam_ref, sub_ref = next(it), next(it)
    o_ref = next(it)
    kall, vall = next(it), next(it)

    @pl.when(pl.program_id(2) == 0)
    def _():
        if n_ctx:
            kall[0:n_ctx, :] = ck_ref[...].astype(BF16)
            vall[0:n_ctx, :] = cv_ref[...].astype(BF16)
            kall[n_ctx:n_ctx + L, :] = _rope(k_ref[...], ck[...], sak[...], sbk[...]).astype(BF16)
        else:
            kall[...] = k_ref[...].astype(BF16)
        vall[n_ctx:n_ctx + L, :] = v_ref[...].astype(BF16)

    q = q_ref[...]
    if n_ctx:
        q = _rope(q, cq[...], saq[...], sbq[...])
    lane = lax.broadcasted_iota(jnp.int32, (1, ATT_VD), 1)
    scale = ATT_HD ** -0.5
    q1 = (q * jnp.where(lane < ATT_HD, scale, 0.0)).astype(BF16)
    q2 = (q * jnp.where(lane < ATT_HD, 0.0, scale)).astype(BF16)
    kk = kall[...]

    def softmax(qm):
        s = _dot_nt(qm, kk)
        p = jnp.exp(s - jnp.max(s, axis=-1, keepdims=True))
        return p / jnp.sum(p, axis=-1, keepdims=True)

    lq = lam_ref[...]
    lam = (jnp.exp(jnp.sum(lq[0:1, :] * lq[1:2, :], axis=-1, keepdims=True))
           - jnp.exp(jnp.sum(lq[2:3, :] * lq[3:4, :], axis=-1, keepdims=True)) + lam_init)
    pd = softmax(q1) - lam * softmax(q2)
    o = _dot(pd.astype(BF16), vall[...])
    o_ref[...] = (_rms(o) * sub_ref[...] * (1.0 - lam_init)).astype(o_ref.dtype)


def _attn_call(proj, o_prev, lam_qk, subln, lam_init, *, L, nb, blk0, tq, ctx=None, rope=None):
    nblk = N_TOK // L
    nq = L // tq
    proj3 = proj.reshape(nblk, L, PROJ_W)
    hq, hk, hv = COL_Q // ATT_VD, COL_K // ATT_VD, COL_V // ATT_VD
    in_specs = [
        pl.BlockSpec((None, tq, ATT_VD), lambda b, h, i: (blk0 + b, i, hq + h)),
        pl.BlockSpec((None, L, ATT_VD), lambda b, h, i: (blk0 + b, 0, hk + h)),
        pl.BlockSpec((None, L, ATT_VD), lambda b, h, i: (blk0 + b, 0, hv + h)),
    ]
    args = [proj3, proj3, proj3]
    n_ctx = 0
    if ctx is not None:
        cache_k, cache_v, e = ctx
        n_ctx = cache_k.shape[2]
        for cch in (cache_k, cache_v):
            args.append(cch.reshape(DEC_BATCH, cch.shape[1], n_ctx, ATT_INNER))
            in_specs.append(pl.BlockSpec((None, None, n_ctx, ATT_VD), lambda b, h, i: (b, e, 0, h)))
        for t in rope:
            args.append(t)
            in_specs.append(pl.BlockSpec((tq, ATT_VD), lambda b, h, i: (i, 0)))
        for t in rope:
            args.append(t)
            in_specs.append(pl.BlockSpec((L, ATT_VD), lambda b, h, i: (0, 0)))
    args += [lam_qk, subln]
    in_specs += [pl.BlockSpec((4, ATT_HD), lambda b, h, i: (0, 0)),
                 pl.BlockSpec((1, ATT_VD), lambda b, h, i: (0, 0))]
    kern = functools.partial(_attn_kernel, L=L, n_ctx=n_ctx, lam_init=lam_init)
    io_alias = {}
    if o_prev is not None:
        io_alias = {len(args): 0}
        args.append(o_prev.reshape(nblk, L, ATT_INNER))
        in_specs.append(pl.BlockSpec(memory_space=pl.ANY))
        kern = _drop_arg(kern, len(args) - 1)
    out = pl.pallas_call(
        kern,
        grid=(nb, ATT_HEADS, nq),
        in_specs=in_specs,
        out_specs=pl.BlockSpec((None, tq, ATT_VD), lambda b, h, i: (blk0 + b, i, h)),
        out_shape=jax.ShapeDtypeStruct((nblk, L, ATT_INNER), BF16),
        scratch_shapes=[pltpu.VMEM((n_ctx + L, ATT_VD), BF16),
                        pltpu.VMEM((n_ctx + L, ATT_VD), BF16)],
        input_output_aliases=io_alias,
        compiler_params=_params("parallel", "parallel", "arbitrary"),
        name=f"attn_L{L}",
    )(*args)
    return out.reshape(N_TOK, ATT_INNER)


def _outproj_kernel(x_ref, y_ref, o_ref, mod_ref, w_ref, out_ref):
    acc = _dot(y_ref[...], w_ref[0:SSD_INNER, :]) + _dot(o_ref[...], w_ref[SSD_INNER:, :])
    out_ref[...] = x_ref[...] + mod_ref[2:3, :] * acc


def _outproj_call(x, y, o, mod, l, w):
    tm = 512
    row = _mod_row(tm)
    tok = lambda: pl.BlockSpec((tm, D_MODEL), lambda i: (i, 0))
    return pl.pallas_call(
        _outproj_kernel,
        grid=(N_TOK // tm,),
        in_specs=[tok(), tok(), tok(),
                  pl.BlockSpec((None, None, 6, D_MODEL), lambda i: (l, row(i), 0, 0)),
                  pl.BlockSpec((SSD_INNER + ATT_INNER, D_MODEL), lambda i: (0, 0))],
        out_specs=tok(),
        out_shape=jax.ShapeDtypeStruct((N_TOK, D_MODEL), F32),
        compiler_params=_params("parallel"),
        name="outproj",
    )(x, y, o, mod, w)


def _ffn_kernel(x_ref, mod_ref, g_ref, wg_ref, wu_ref, wo_ref, out_ref, h_ref, acc_ref):
    j = pl.program_id(1)

    @pl.when(j == 0)
    def _():
        h = _rms(x_ref[...]) * g_ref[...] * (1.0 + mod_ref[4:5, :]) + mod_ref[3:4, :]
        h_ref[...] = h.astype(BF16)
        acc_ref[...] = jnp.zeros_like(acc_ref)

    h = h_ref[...]
    a = (_silu(_dot(h, wg_ref[...])) * _dot(h, wu_ref[...])).astype(BF16)
    acc_ref[...] += _dot(a, wo_ref[...])

    @pl.when(j == pl.num_programs(1) - 1)
    def _():
        out_ref[...] = x_ref[...] + mod_ref[5:6, :] * acc_ref[...]


def _ffn_call(x, mod, l, gain, w_in, w_out):
    tm, th = 512, FFN_HIDDEN // 2
    nj = FFN_HIDDEN // th
    row = _mod_row(tm)
    return pl.pallas_call(
        _ffn_kernel,
        grid=(N_TOK // tm, nj),
        in_specs=[
            pl.BlockSpec((tm, D_MODEL), lambda i, j: (i, 0)),
            pl.BlockSpec((None, None, 6, D_MODEL), lambda i, j: (l, row(i), 0, 0)),
            pl.BlockSpec((1, D_MODEL), lambda i, j: (0, 0)),
            pl.BlockSpec((D_MODEL, th), lambda i, j: (0, j)),
            pl.BlockSpec((D_MODEL, th), lambda i, j: (0, nj + j)),
            pl.BlockSpec((th, D_MODEL), lambda i, j: (j, 0)),
        ],
        out_specs=pl.BlockSpec((tm, D_MODEL), lambda i, j: (i, 0)),
        out_shape=jax.ShapeDtypeStruct((N_TOK, D_MODEL), F32),
        scratch_shapes=[pltpu.VMEM((tm, D_MODEL), BF16), pltpu.VMEM((tm, D_MODEL), F32)],
        compiler_params=_params("parallel", "arbitrary"),
        name="ffn",
    )(x, mod, gain, w_in, w_in, w_out)


def _four_kernel(x_ref, mod_ref, g_ref, cc_ref, sc_ref, cs_ref, w_ref, b_ref, out_ref, stk_ref, *, L):
    x = x_ref[...]
    h = (_rms(x) * g_ref[...] * (1.0 + mod_ref[1:2, :]) + mod_ref[0:1, :]).astype(BF16)
    for g in range(FOUR_GROUPS):
        ls = slice(g * FOUR_GC, (g + 1) * FOUR_GC)
        hg = h[:, ls]
        stk_ref[0:L, ls] = _dot(hg, cc_ref[...]).astype(BF16)
        stk_ref[L:2 * L, ls] = _dot(hg, sc_ref[...]).astype(BF16)
    f = _dot(cs_ref[...], stk_ref[...]).astype(BF16)
    out_ref[...] = x + mod_ref[2:3, :] * (_dot(f, w_ref[...]) + b_ref[...])


def _dft_tables(L):
    n = FOUR_GC
    ang_c = 2.0 * np.pi * ((np.arange(n)[:, None] * np.arange(n)[None, :]) % n) / n
    ang_l = 2.0 * np.pi * ((np.arange(L)[:, None] * np.arange(L)[None, :]) % L) / L
    cc = np.cos(ang_c) / math.sqrt(n)
    sc = np.sin(ang_c) / math.sqrt(n)
    cs = np.concatenate([np.cos(ang_l), -np.sin(ang_l)], axis=1) / math.sqrt(L)
    return (jnp.asarray(cc, F32).astype(BF16), jnp.asarray(sc, F32).astype(BF16),
            jnp.asarray(cs, F32).astype(BF16))


def _four_call(x, x_prev, mod, l, gain, w, b, *, L, nb, blk0):
    nblk = N_TOK // L
    cc, sc, cs = _dft_tables(L)
    row = _mod_row(L)
    args = [x.reshape(nblk, L, D_MODEL), mod, gain, cc, sc, cs, w, b]
    in_specs = [
        pl.BlockSpec((None, L, D_MODEL), lambda i: (blk0 + i, 0, 0)),
        pl.BlockSpec((None, None, 6, D_MODEL), lambda i: (l, row(blk0 + i), 0, 0)),
        pl.BlockSpec((1, D_MODEL), lambda i: (0, 0)),
        pl.BlockSpec((FOUR_GC, FOUR_GC), lambda i: (0, 0)),
        pl.BlockSpec((FOUR_GC, FOUR_GC), lambda i: (0, 0)),
        pl.BlockSpec((L, 2 * L), lambda i: (0, 0)),
        pl.BlockSpec((D_MODEL, D_MODEL), lambda i: (0, 0)),
        pl.BlockSpec((1, D_MODEL), lambda i: (0, 0)),
    ]
    kern = functools.partial(_four_kernel, L=L)
    io_alias = {}
    if x_prev is not None:
        io_alias = {len(args): 0}
        args.append(x_prev.reshape(nblk, L, D_MODEL))
        in_specs.append(pl.BlockSpec(memory_space=pl.ANY))
        kern = _drop_arg(kern, len(args) - 1)
    out = pl.pallas_call(
        kern,
        grid=(nb,),
        in_specs=in_specs,
        out_specs=pl.BlockSpec((None, L, D_MODEL), lambda i: (blk0 + i, 0, 0)),
        out_shape=jax.ShapeDtypeStruct((nblk, L, D_MODEL), F32),
        scratch_shapes=[pltpu.VMEM((2 * L, D_MODEL), BF16)],
        input_output_aliases=io_alias,
        compiler_params=_params("parallel"),
        name=f"fourier_L{L}",
    )(*args)
    return out.reshape(N_TOK, D_MODEL)


def _final_kernel(x_ref, g_ref, o_ref):
    o_ref[...] = _rms(x_ref[...]) * g_ref[...]


def _final_call(x, gain):
    tm = 1024
    return pl.pallas_call(
        _final_kernel,
        grid=(N_TOK // tm,),
        in_specs=[pl.BlockSpec((tm, D_MODEL), lambda i: (i, 0)),
                  pl.BlockSpec((1, D_MODEL), lambda i: (0, 0))],
        out_specs=pl.BlockSpec((tm, D_MODEL), lambda i: (i, 0)),
        out_shape=jax.ShapeDtypeStruct((N_TOK, D_MODEL), F32),
        compiler_params=_params("parallel"),
        name="final_norm",
    )(x, gain)


def _rope_lane_tables(L):
    t = jnp.arange(L)
    row = (t // GRID_W).astype(F32)
    col = (t % GRID_W).astype(F32)
    freq = ROPE_THETA ** (-jnp.arange(ROPE_PAIRS, dtype=F32) / ROPE_PAIRS)
    ar = row[:, None] * freq[None]
    ac = col[:, None] * freq[None]
    z = jnp.zeros_like(ar)
    cos32 = lambda a: jnp.concatenate([jnp.cos(a), jnp.cos(a)], axis=-1)
    sa32 = lambda a: jnp.concatenate([-jnp.sin(a), z], axis=-1)
    sb32 = lambda a: jnp.concatenate([z, jnp.sin(a)], axis=-1)
    tile = lambda f: jnp.concatenate([f(ar), f(ac), f(ar), f(ac)], axis=-1)
    return tile(cos32), tile(sa32), tile(sb32)


def _pad_lanes(v, width):
    v = v.reshape(1, -1).astype(F32)
    return jnp.pad(v, ((0, 0), (0, width - v.shape[1])))


def kernel(x_prompt, x_sample, cache_k, cache_v, state_ssd_fwd, state_ssd_bwd, c, c_ctx, w_ada, b_ada, norm_mix, norm_ffn, w_in_ab, conv_w, conv_b, dt_bias, a_log, d_skip, ssd_norm, lambda_qk, subln, w_out_ab, w_four, b_four, w_ffn_in, w_ffn_out, norm_final):
    x = jnp.concatenate([x_prompt.reshape(P_TOK, D_MODEL), x_sample.reshape(S_TOK, D_MODEL)], axis=0)
    cvec = jnp.concatenate([c_ctx[None, :], c, jnp.zeros((N_MOD - 1 - DEC_BATCH, D_MODEL), F32)], axis=0)
    mod = _ada_call(cvec, w_ada, b_ada).reshape(DEPTH, N_MOD, 6, D_MODEL)
    rope = _rope_lane_tables(DEC_SEQ)

    new_k, new_v, new_f, new_b = [], [], [], []
    o1, o2, o3 = SSD_INNER, SSD_INNER + CONV_CH, SSD_INNER + CONV_CH + 2 * SSD_HEADS
    for l in range(DEPTH):
        gain_mix = norm_mix[l].reshape(1, D_MODEL)
        if l % 2 == 0:
            e = l // 2
            lam_init = 0.8 - 0.6 * math.exp(-0.3 * l)
            w_in = w_in_ab[e]
            w_main = jnp.concatenate([w_in[:, o1:o2], w_in[:, :o1], w_in[:, o3:]], axis=1).astype(BF16)
            w_dt = jnp.pad(w_in[:, o2:o3], ((0, 0), (0, DT_W - 2 * SSD_HEADS))).astype(BF16)
            proj, dtarr = _inproj_call(x, mod, l, gain_mix, w_main, w_dt)

            conv_w8 = jnp.pad(conv_w[e], ((0, 8 - CONV_W), (0, 0)))
            ssd_w = (conv_w8, conv_b[e].reshape(1, CONV_CH), _pad_lanes(dt_bias[e], DT_W),
                     _pad_lanes(a_log[e], DT_W), jnp.repeat(d_skip[e], SSD_HD).reshape(1, SSD_INNER),
                     ssd_norm[e].reshape(1, SSD_INNER))
            y, hf, hb = _ssd_call(proj, dtarr, None, *ssd_w, L=SEQ, nb=BATCH, blk0=0, emit_state=True)
            (y,) = _ssd_call(proj, dtarr, y, *ssd_w, L=DEC_SEQ, nb=DEC_BATCH, blk0=P_TOK // DEC_SEQ,
                             h0=(state_ssd_fwd, state_ssd_bwd, e))
            sub = subln[e].reshape(1, ATT_VD)
            o = _attn_call(proj, None, lambda_qk[e], sub, lam_init, L=SEQ, nb=BATCH, blk0=0, tq=SEQ)
            o = _attn_call(proj, o, lambda_qk[e], sub, lam_init, L=DEC_SEQ, nb=DEC_BATCH,
                           blk0=P_TOK // DEC_SEQ, tq=256, ctx=(cache_k, cache_v, e), rope=rope)
            x = _outproj_call(x, y, o, mod, l, w_out_ab[e].astype(BF16))

            new_k.append(proj[:P_TOK, COL_K:COL_K + ATT_INNER].reshape(BATCH, SEQ, ATT_HEADS, 2, ATT_HD))
            new_v.append(proj[:P_TOK, COL_V:COL_V + ATT_INNER].reshape(BATCH, SEQ, ATT_HEADS, ATT_VD))
            new_f.append(hf.reshape(BATCH, SSD_HEADS, SSD_HD, D_STATE))
            new_b.append(hb.reshape(BATCH, SSD_HEADS, SSD_HD, D_STATE))
        else:
            i = l // 2
            w4 = w_four[i].astype(BF16)
            b4 = b_four[i].reshape(1, D_MODEL)
            xn = _four_call(x, None, mod, l, gain_mix, w4, b4, L=SEQ, nb=BATCH, blk0=0)
            x = _four_call(x, xn, mod, l, gain_mix, w4, b4, L=DEC_SEQ, nb=DEC_BATCH,
                           blk0=P_TOK // DEC_SEQ)
        x = _ffn_call(x, mod, l, norm_ffn[l].reshape(1, D_MODEL),
                      w_ffn_in[l].astype(BF16), w_ffn_out[l].astype(BF16))

    y = _final_call(x, norm_final.reshape(1, D_MODEL))
    return (y[:P_TOK].reshape(BATCH, SEQ, D_MODEL), y[P_TOK:].reshape(DEC_BATCH, DEC_SEQ, D_MODEL),
            jnp.stack(new_k, axis=1), jnp.stack(new_v, axis=1),
            jnp.stack(new_f, axis=1), jnp.stack(new_b, axis=1))
```

```python
import functools
import math

import numpy as np
import jax
import jax.numpy as jnp
from jax import lax
from jax.experimental import pallas as pl
from jax.experimental.pallas import tpu as pltpu

D_MODEL = 1024
BATCH = 16
SEQ = 256
DEPTH = 4
N_EVEN = 2
DEC_BATCH = 4
DEC_SEQ = 1024
PAST_LEN = 512
GRID_W = 64
SSD_HEADS = 16
SSD_HD = 64
SSD_INNER = 1024
SSD_GROUPS = 4
D_STATE = 128
CONV_W = 5
CONV_CH = 2048
CHUNK = 128
ATT_HEADS = 8
ATT_HD = 64
ATT_VD = 128
ATT_INNER = 1024
ROPE_THETA = 10000.0
ROPE_PAIRS = 16
FOUR_GROUPS = 4
FOUR_GC = D_MODEL // FOUR_GROUPS
FFN_HIDDEN = 2816
EPS = 1e-6

P_TOK = BATCH * SEQ
S_TOK = DEC_BATCH * DEC_SEQ
N_TOK = P_TOK + S_TOK
N_MOD = 8
PA_W = CONV_CH + SSD_INNER
PB_W = 3 * ATT_INNER
DT_W = 128
NH2 = SSD_HEADS // 2
LANE = 128
MXU_N = 256
VMEM_LIMIT = 56 * 1024 * 1024

F32 = jnp.float32
BF16 = jnp.bfloat16


def _params(*sem):
    return pltpu.CompilerParams(dimension_semantics=sem, vmem_limit_bytes=VMEM_LIMIT)


def _dot(a, b):
    return jnp.dot(a, b, preferred_element_type=F32)


def _dot_nt(a, b):
    return lax.dot_general(a, b, (((1,), (1,)), ((), ())), preferred_element_type=F32)


def _silu(x):
    return x * jax.nn.sigmoid(x)


def _rms(x):
    return x * lax.rsqrt(jnp.mean(x * x, axis=-1, keepdims=True) + EPS)


def _mod_row(tm):
    return lambda i: jnp.maximum((i * tm) // DEC_SEQ - (P_TOK // DEC_SEQ - 1), 0)


def _drop_args(kern, start, n):
    def wrapped(*refs):
        return kern(*(refs[:start] + refs[start + n:]))
    return wrapped


def _ada_kernel(c_ref, w_ref, b_ref, o_ref):
    s = _silu(c_ref[...]).astype(BF16)
    o_ref[...] = _dot(s, w_ref[...].astype(BF16)) + b_ref[...]


def _ada_call(cvec, w_ada, b_ada):
    tn = 1024
    return pl.pallas_call(
        _ada_kernel,
        grid=(DEPTH, 6 * D_MODEL // tn),
        in_specs=[
            pl.BlockSpec((N_MOD, D_MODEL), lambda l, j: (0, 0)),
            pl.BlockSpec((None, D_MODEL, tn), lambda l, j: (l, 0, j)),
            pl.BlockSpec((None, 1, tn), lambda l, j: (l, 0, j)),
        ],
        out_specs=pl.BlockSpec((None, N_MOD, tn), lambda l, j: (l, 0, j)),
        out_shape=jax.ShapeDtypeStruct((DEPTH, N_MOD, 6 * D_MODEL), F32),
        compiler_params=_params("parallel", "parallel"),
        name="adaln",
    )(cvec, w_ada, b_ada.reshape(DEPTH, 1, 6 * D_MODEL))


def _inproj_kernel(x_ref, mod_ref, g_ref, w_ref, wdt_ref, pa_ref, pb_ref, dt_ref, kn_ref, vn_ref,
                   h_ref, *, n_prompt_tiles, seqs_per_tile):
    i, j = pl.program_id(0), pl.program_id(1)

    @pl.when(j == 0)
    def _():
        h = _rms(x_ref[...]) * g_ref[...] * (1.0 + mod_ref[1:2, :]) + mod_ref[0:1, :]
        hb = h.astype(BF16)
        h_ref[...] = hb
        dt_ref[...] = _dot(hb, wdt_ref[...])

    res = _dot(h_ref[...], w_ref[...])

    @pl.when(j < 3)
    def _():
        pa_ref[...] = res

    @pl.when(j >= 3)
    def _():
        pb_ref[...] = res.astype(BF16)

    @pl.when((j == 4) & (i < n_prompt_tiles))
    def _():
        kn_ref[...] = res.reshape(seqs_per_tile, SEQ, ATT_INNER)

    @pl.when((j == 5) & (i < n_prompt_tiles))
    def _():
        vn_ref[...] = res.reshape(seqs_per_tile, SEQ, ATT_INNER)


def _inproj_call(x, mod, l, e, gain, w_in, kv_prev):
    tm, tn = 1024, 1024
    row = _mod_row(tm)
    npt = P_TOK // tm
    spt = tm // SEQ
    args = [x, mod, gain, w_in, w_in]
    in_specs = [
        pl.BlockSpec((tm, D_MODEL), lambda i, j: (i, 0)),
        pl.BlockSpec((None, None, 6, D_MODEL), lambda i, j: (l, row(i), 0, 0)),
        pl.BlockSpec((1, D_MODEL), lambda i, j: (0, 0)),
        pl.BlockSpec((None, D_MODEL, tn), lambda i, j: (e, 0, j)),
        pl.BlockSpec((None, D_MODEL, DT_W), lambda i, j: (e, 0, (PA_W + PB_W) // DT_W)),
    ]
    kern = functools.partial(_inproj_kernel, n_prompt_tiles=npt, seqs_per_tile=spt)
    io_alias = {}
    if kv_prev is not None:
        io_alias = {len(args): 3, len(args) + 1: 4}
        args += list(kv_prev)
        in_specs += [pl.BlockSpec(memory_space=pl.ANY)] * 2
        kern = _drop_args(kern, len(args) - 2, 2)
    kv_spec = pl.BlockSpec((spt, None, SEQ, ATT_INNER),
                           lambda i, j: (jnp.minimum(i, npt - 1), e, 0, 0))
    kv_shape = jax.ShapeDtypeStruct((BATCH, N_EVEN, SEQ, ATT_INNER), F32)
    return pl.pallas_call(
        kern,
        grid=(N_TOK // tm, (PA_W + PB_W) // tn),
        in_specs=in_specs,
        out_specs=[
            pl.BlockSpec((tm, tn), lambda i, j: (i, jnp.minimum(j, 2))),
            pl.BlockSpec((tm, tn), lambda i, j: (i, jnp.maximum(j - 3, 0))),
            pl.BlockSpec((tm, DT_W), lambda i, j: (i, 0)),
            kv_spec, kv_spec,
        ],
        out_shape=[
            jax.ShapeDtypeStruct((N_TOK, PA_W), F32),
            jax.ShapeDtypeStruct((N_TOK, PB_W), BF16),
            jax.ShapeDtypeStruct((N_TOK, DT_W), F32),
            kv_shape, kv_shape,
        ],
        scratch_shapes=[pltpu.VMEM((tm, D_MODEL), BF16)],
        input_output_aliases=io_alias,
        compiler_params=_params("arbitrary", "arbitrary"),
        name="inproj",
    )(*args)


def _ssd_kernel(*refs, L, has_h0, emit_state):
    it = iter(refs)
    xbc_ref, z_ref, dt_ref = next(it), next(it), next(it)
    convw_ref, convb_ref, dtb_ref, alog_ref, dskip_ref, norm_ref = (next(it) for _ in range(6))
    sel_refs = (next(it), next(it))
    if has_h0:
        h0_refs = (next(it), next(it))
    y_ref = next(it)
    if emit_state:
        hout_refs = (next(it), next(it))
    xpad, xact, xdt_f, xdt_b, bt, cact, yacc, dtv, state = (next(it) for _ in range(9))

    nc = L // CHUNK

    xpad[0:8, :] = jnp.zeros((8, CONV_CH), F32)
    xpad[L + 8:L + 16, :] = jnp.zeros((8, CONV_CH), F32)
    for c in range(nc):
        xpad[8 + c * CHUNK:8 + (c + 1) * CHUNK, :] = xbc_ref[c * CHUNK:(c + 1) * CHUNK, :]
    slab = 512
    for c in range(nc):
        rows = slice(c * CHUNK, (c + 1) * CHUNK)
        for s in range(CONV_CH // slab):
            ls = slice(s * slab, (s + 1) * slab)
            acc = jnp.broadcast_to(convb_ref[:, ls], (CHUNK, slab))
            for k in range(CONV_W):
                r0 = 8 + c * CHUNK + k - CONV_W // 2
                acc = acc + convw_ref[k:k + 1, ls] * xpad[r0:r0 + CHUNK, ls]
            act = _silu(acc)
            if s < 2:
                xact[rows, ls] = act
            elif s == 2:
                for g in range(SSD_GROUPS):
                    bt[c, g * D_STATE:(g + 1) * D_STATE, :] = (
                        act[:, g * D_STATE:(g + 1) * D_STATE].T.astype(BF16))
            else:
                cact[rows, :] = act.astype(BF16)

    dr = dt_ref[...] + dtb_ref[...]
    dtv[...] = jnp.maximum(dr, 0.0) + jnp.log1p(jnp.exp(-jnp.abs(dr)))
    a_row = -jnp.exp(alog_ref[...])

    for c in range(nc):
        rows = slice(c * CHUNK, (c + 1) * CHUNK)
        d = dtv[rows, :]
        d_hi = d.astype(BF16)
        d_lo = (d - d_hi.astype(F32)).astype(BF16)
        d2 = jnp.concatenate([d_hi, d_lo], axis=1)
        xc = xact[rows, :]
        xdt_f[rows, :] = (xc * _dot(d2, sel_refs[0][...])).astype(BF16)
        xdt_b[rows, :] = (xc * _dot(d2, sel_refs[1][...])).astype(BF16)

    ri = lax.broadcasted_iota(jnp.int32, (CHUNK, CHUNK), 0)
    ci = lax.broadcasted_iota(jnp.int32, (CHUNK, CHUNK), 1)
    low_half = ci < SSD_HD
    low_row = low_half[0:1, :]

    def run_direction(fwd):
        mask = (ri >= ci) if fwd else (ri <= ci)
        tri = mask.astype(F32)
        col0 = 0 if fwd else SSD_HEADS
        last = CHUNK - 1 if fwd else 0
        xdt = xdt_f if fwd else xdt_b
        d_idx = 0 if fwd else 1

        for m in range(NH2):
            if has_h0:
                state[m] = h0_refs[d_idx][m].T
            else:
                state[m] = jnp.zeros((D_STATE, LANE), F32)

        def chunk_body(step, carry):
            c = step if fwd else nc - 1 - step
            r0 = pl.multiple_of(c * CHUNK, CHUNK)
            rows = pl.ds(r0, CHUNK)
            dtc = dtv[rows, :]
            acs = jnp.dot(tri, dtc * a_row, preferred_element_type=F32,
                          precision=lax.Precision.HIGHEST)
            acs_t = acs.T
            acs_last = acs[last:last + 1, :]
            for g in range(SSD_GROUPS):
                btg = bt[c, g * D_STATE:(g + 1) * D_STATE, :]
                cg = cact[rows, g * D_STATE:(g + 1) * D_STATE]
                cbm = jnp.where(mask, _dot(cg, btg), 0.0)
                for pr in range(2):
                    m = g * 2 + pr
                    ha, hb = col0 + 2 * m, col0 + 2 * m + 1
                    lanes = slice(m * LANE, (m + 1) * LANE)
                    xd2 = xdt[rows, lanes]
                    acol_a = jnp.broadcast_to(acs[:, ha:ha + 1], (CHUNK, CHUNK))
                    acol_b = jnp.broadcast_to(acs[:, hb:hb + 1], (CHUNK, CHUNK))
                    w_a = cbm * jnp.exp(jnp.minimum(acol_a - acs_t[ha:ha + 1, :], 0.0))
                    w_b = cbm * jnp.exp(jnp.minimum(acol_b - acs_t[hb:hb + 1, :], 0.0))
                    w2 = jnp.concatenate([w_a.astype(BF16), w_b.astype(BF16)], axis=1)
                    zero = jnp.zeros_like(xd2)
                    x_blk = jnp.concatenate([jnp.where(low_half, xd2, zero),
                                             jnp.where(low_half, zero, xd2)], axis=0)
                    y_in = _dot(w2, x_blk)
                    acol2 = jnp.where(low_half, acol_a, acol_b)
                    last2 = jnp.where(low_row, acs_last[:, ha:ha + 1], acs_last[:, hb:hb + 1])
                    h_prev = state[m]
                    y_out = _dot(cg, h_prev.astype(BF16)) * jnp.exp(acol2)
                    to_end = jnp.exp(last2 - acol2)
                    st = _dot(btg, (to_end * xd2.astype(F32)).astype(BF16))
                    state[m] = jnp.exp(last2) * h_prev + st
                    y2 = y_in + y_out
                    if fwd:
                        yacc[rows, lanes] = y2 + dskip_ref[:, lanes] * xact[rows, lanes]
                    else:
                        yacc[rows, lanes] = yacc[rows, lanes] + y2
            if not fwd:
                yt = yacc[rows, :] * _silu(z_ref[rows, :])
                y_ref[rows, :] = (_rms(yt) * norm_ref[...]).astype(y_ref.dtype)
            return carry

        lax.fori_loop(0, nc, chunk_body, 0)
        if emit_state:
            for m in range(NH2):
                hout_refs[d_idx][m] = state[m].T

    run_direction(True)
    run_direction(False)


def _head_spread_tables():
    ch_head = np.arange(SSD_INNER) // SSD_HD
    out = []
    for d in range(2):
        k = np.arange(2 * DT_W) % DT_W
        out.append(jnp.asarray(k[:, None] == (d * SSD_HEADS + ch_head)[None, :], BF16))
    return out


def _ssd_call(pa, dtarr, y_prev, states_prev, weights, e, *, L, nb, blk0, h0=None):
    emit_state = states_prev is not None
    nblk = N_TOK // L
    nc = L // CHUNK
    pa3 = pa.reshape(nblk, L, PA_W)
    vec = lambda w: pl.BlockSpec((1, w), lambda b: (0, 0))
    once = pl.Buffered(1)
    in_specs = [
        pl.BlockSpec((None, L, CONV_CH), lambda b: (blk0 + b, 0, 0), pipeline_mode=once),
        pl.BlockSpec((None, L, SSD_INNER), lambda b: (blk0 + b, 0, CONV_CH // SSD_INNER),
                     pipeline_mode=once),
        pl.BlockSpec((None, L, DT_W), lambda b: (blk0 + b, 0, 0)),
        pl.BlockSpec((8, CONV_CH), lambda b: (0, 0)),
        vec(CONV_CH), vec(DT_W), vec(DT_W), vec(SSD_INNER), vec(SSD_INNER),
        pl.BlockSpec((2 * DT_W, SSD_INNER), lambda b: (0, 0)),
        pl.BlockSpec((2 * DT_W, SSD_INNER), lambda b: (0, 0)),
    ]
    args = [pa3, pa3, dtarr.reshape(nblk, L, DT_W), *weights, *_head_spread_tables()]
    state_spec = pl.BlockSpec((None, None, NH2, LANE, D_STATE), lambda b: (b, e, 0, 0, 0))
    if h0 is not None:
        for h in h0:
            args.append(h.reshape(DEC_BATCH, N_EVEN, NH2, LANE, D_STATE))
            in_specs.append(state_spec)
    kern = functools.partial(_ssd_kernel, L=L, has_h0=h0 is not None, emit_state=emit_state)
    io_alias = {}
    n_alias = 0
    prevs = [None if y_prev is None else y_prev.reshape(nblk, L, SSD_INNER)]
    prevs += list(states_prev) if emit_state else []
    for k, prev in enumerate(prevs):
        if prev is not None:
            io_alias[len(args)] = k
            args.append(prev)
            in_specs.append(pl.BlockSpec(memory_space=pl.ANY))
            n_alias += 1
    kern = _drop_args(kern, len(args) - n_alias, n_alias)
    out_specs = [pl.BlockSpec((None, L, SSD_INNER), lambda b: (blk0 + b, 0, 0))]
    out_shape = [jax.ShapeDtypeStruct((nblk, L, SSD_INNER), BF16)]
    if emit_state:
        out_specs += [state_spec, state_spec]
        out_shape += [jax.ShapeDtypeStruct((nb, N_EVEN, NH2, LANE, D_STATE), F32)] * 2
    return pl.pallas_call(
        kern,
        grid=(nb,),
        in_specs=in_specs,
        out_specs=out_specs,
        out_shape=out_shape,
        scratch_shapes=[
            pltpu.VMEM((L + 16, CONV_CH), F32),
            pltpu.VMEM((L, SSD_INNER), F32),
            pltpu.VMEM((L, SSD_INNER), BF16),
            pltpu.VMEM((L, SSD_INNER), BF16),
            pltpu.VMEM((nc, SSD_GROUPS * D_STATE, CHUNK), BF16),
            pltpu.VMEM((L, SSD_GROUPS * D_STATE), BF16),
            pltpu.VMEM((L, SSD_INNER), F32),
            pltpu.VMEM((L, DT_W), F32),
            pltpu.VMEM((NH2, D_STATE, LANE), F32),
        ],
        input_output_aliases=io_alias,
        compiler_params=_params("parallel"),
        name=f"ssd_L{L}",
    )(*args)


def _rope(x, cos, sin_a, sin_b):
    return (x * cos + pltpu.roll(x, LANE - ROPE_PAIRS, 1) * sin_a
            + pltpu.roll(x, ROPE_PAIRS, 1) * sin_b)


def _attn_kernel(*refs, L, n_ctx, lam_init):
    it = iter(refs)
    q_ref, k_ref, v_ref = next(it), next(it), next(it)
    if n_ctx:
        ck_ref, cv_ref = next(it), next(it)
        cq, saq, sbq, ck, sak, sbk = (next(it) for _ in range(6))
    lam_ref, sub_ref = next(it), next(it)
    o_ref = next(it)
    kall, vext = next(it), next(it)
    lk = n_ctx + L

    @pl.when(pl.program_id(1) == 0)
    def _():
        lane = lax.broadcasted_iota(jnp.int32, (lk, MXU_N - ATT_VD), 1)
        ones_col = jnp.where(lane == 0, 1.0, 0.0).astype(BF16)
        for h in range(ATT_HEADS):
            hs = slice(h * ATT_VD, (h + 1) * ATT_VD)
            if n_ctx:
                kall[h, 0:n_ctx, :] = ck_ref[:, hs].astype(BF16)
                vext[h, 0:n_ctx, 0:ATT_VD] = cv_ref[:, hs].astype(BF16)
                kh = _rope(k_ref[:, hs].astype(F32), ck[...], sak[...], sbk[...])
                kall[h, n_ctx:lk, :] = kh.astype(BF16)
            else:
                kall[h] = k_ref[:, hs]
            vext[h, n_ctx:lk, 0:ATT_VD] = v_ref[:, hs]
            vext[h, :, ATT_VD:MXU_N] = ones_col

    lane = lax.broadcasted_iota(jnp.int32, (1, ATT_VD), 1)
    scale = ATT_HD ** -0.5
    m1 = jnp.where(lane < ATT_HD, scale, 0.0)
    m2 = jnp.where(lane < ATT_HD, 0.0, scale)
    lq = lam_ref[...]
    lam = (jnp.exp(jnp.sum(lq[0:1, :] * lq[1:2, :], axis=-1, keepdims=True))
           - jnp.exp(jnp.sum(lq[2:3, :] * lq[3:4, :], axis=-1, keepdims=True)) + lam_init)

    for h in range(ATT_HEADS):
        hs = slice(h * ATT_VD, (h + 1) * ATT_VD)
        q = q_ref[:, hs].astype(F32)
        if n_ctx:
            q = _rope(q, cq[...], saq[...], sbq[...])
        kk = kall[h]
        vv = vext[h]

        def attend(qm):
            s = _dot_nt(qm.astype(BF16), kk)
            p = jnp.exp(s - jnp.max(s, axis=-1, keepdims=True)).astype(BF16)
            pv = _dot(p, vv)
            return pv[:, 0:ATT_VD] / pv[:, ATT_VD:ATT_VD + 1]

        o = attend(q * m1) - lam * attend(q * m2)
        o_ref[:, hs] = (_rms(o) * sub_ref[...] * (1.0 - lam_init)).astype(o_ref.dtype)


def _attn_call(pb, o_prev, lam_qk, subln, lam_init, *, L, nb, blk0, tq, ctx=None, rope=None):
    nblk = N_TOK // L
    nq = L // tq
    pb3 = pb.reshape(nblk, L, PB_W)
    in_specs = [
        pl.BlockSpec((None, tq, ATT_INNER), lambda b, i: (blk0 + b, i, 0)),
        pl.BlockSpec((None, L, ATT_INNER), lambda b, i: (blk0 + b, 0, 1)),
        pl.BlockSpec((None, L, ATT_INNER), lambda b, i: (blk0 + b, 0, 2)),
    ]
    args = [pb3, pb3, pb3]
    n_ctx = 0
    if ctx is not None:
        cache_k, cache_v, e = ctx
        n_ctx = cache_k.shape[2]
        for cch in (cache_k, cache_v):
            args.append(cch.reshape(DEC_BATCH, N_EVEN, n_ctx, ATT_INNER))
            in_specs.append(pl.BlockSpec((None, None, n_ctx, ATT_INNER), lambda b, i: (b, e, 0, 0)))
        for t in rope:
            args.append(t)
            in_specs.append(pl.BlockSpec((tq, ATT_VD), lambda b, i: (i, 0)))
        for t in rope:
            args.append(t)
            in_specs.append(pl.BlockSpec((L, ATT_VD), lambda b, i: (0, 0)))
    args += [lam_qk, subln]
    in_specs += [pl.BlockSpec((4, ATT_HD), lambda b, i: (0, 0)),
                 pl.BlockSpec((1, ATT_VD), lambda b, i: (0, 0))]
    kern = functools.partial(_attn_kernel, L=L, n_ctx=n_ctx, lam_init=lam_init)
    io_alias = {}
    if o_prev is not None:
        io_alias = {len(args): 0}
        args.append(o_prev.reshape(nblk, L, ATT_INNER))
        in_specs.append(pl.BlockSpec(memory_space=pl.ANY))
        kern = _drop_args(kern, len(args) - 1, 1)
    out = pl.pallas_call(
        kern,
        grid=(nb, nq),
        in_specs=in_specs,
        out_specs=pl.BlockSpec((None, tq, ATT_INNER), lambda b, i: (blk0 + b, i, 0)),
        out_shape=jax.ShapeDtypeStruct((nblk, L, ATT_INNER), BF16),
        scratch_shapes=[pltpu.VMEM((ATT_HEADS, n_ctx + L, ATT_VD), BF16),
                        pltpu.VMEM((ATT_HEADS, n_ctx + L, MXU_N), BF16)],
        input_output_aliases=io_alias,
        compiler_params=_params("parallel", "arbitrary"),
        name=f"attn_L{L}",
    )(*args)
    return out.reshape(N_TOK, ATT_INNER)


def _outproj_kernel(x_ref, y_ref, o_ref, mod_ref, w_ref, out_ref):
    acc = _dot(y_ref[...], w_ref[0:SSD_INNER, :]) + _dot(o_ref[...], w_ref[SSD_INNER:, :])
    out_ref[...] = x_ref[...] + mod_ref[2:3, :] * acc


def _outproj_call(x, y, o, mod, l, e, w):
    tm = 512
    row = _mod_row(tm)
    tok = lambda: pl.BlockSpec((tm, D_MODEL), lambda i: (i, 0))
    return pl.pallas_call(
        _outproj_kernel,
        grid=(N_TOK // tm,),
        in_specs=[tok(), tok(), tok(),
                  pl.BlockSpec((None, None, 6, D_MODEL), lambda i: (l, row(i), 0, 0)),
                  pl.BlockSpec((None, SSD_INNER + ATT_INNER, D_MODEL), lambda i: (e, 0, 0))],
        out_specs=tok(),
        out_shape=jax.ShapeDtypeStruct((N_TOK, D_MODEL), F32),
        compiler_params=_params("parallel"),
        name="outproj",
    )(x, y, o, mod, w)


def _ffn_kernel(x_ref, mod_ref, g_ref, wg_ref, wu_ref, wo_ref, out_ref, h_ref, acc_ref):
    j = pl.program_id(1)

    @pl.when(j == 0)
    def _():
        h = _rms(x_ref[...]) * g_ref[...] * (1.0 + mod_ref[4:5, :]) + mod_ref[3:4, :]
        h_ref[...] = h.astype(BF16)
        acc_ref[...] = jnp.zeros_like(acc_ref)

    h = h_ref[...]
    a = (_silu(_dot(h, wg_ref[...])) * _dot(h, wu_ref[...])).astype(BF16)
    acc_ref[...] += _dot(a, wo_ref[...])

    @pl.when(j == pl.num_programs(1) - 1)
    def _():
        out_ref[...] = x_ref[...] + mod_ref[5:6, :] * acc_ref[...]


def _ffn_call(x, mod, l, gain, w_in, w_out):
    tm, th = 512, FFN_HIDDEN // 2
    nj = FFN_HIDDEN // th
    row = _mod_row(tm)
    return pl.pallas_call(
        _ffn_kernel,
        grid=(N_TOK // tm, nj),
        in_specs=[
            pl.BlockSpec((tm, D_MODEL), lambda i, j: (i, 0)),
            pl.BlockSpec((None, None, 6, D_MODEL), lambda i, j: (l, row(i), 0, 0)),
            pl.BlockSpec((1, D_MODEL), lambda i, j: (0, 0)),
            pl.BlockSpec((None, D_MODEL, th), lambda i, j: (l, 0, j)),
            pl.BlockSpec((None, D_MODEL, th), lambda i, j: (l, 0, nj + j)),
            pl.BlockSpec((None, th, D_MODEL), lambda i, j: (l, j, 0)),
        ],
        out_specs=pl.BlockSpec((tm, D_MODEL), lambda i, j: (i, 0)),
        out_shape=jax.ShapeDtypeStruct((N_TOK, D_MODEL), F32),
        scratch_shapes=[pltpu.VMEM((tm, D_MODEL), BF16), pltpu.VMEM((tm, D_MODEL), F32)],
        compiler_params=_params("parallel", "arbitrary"),
        name="ffn",
    )(x, mod, gain, w_in, w_in, w_out)


def _four_kernel(x_ref, mod_ref, g_ref, cc_ref, sc_ref, cs_ref, w_ref, b_ref, out_ref, stk_ref, *, L):
    x = x_ref[...]
    h = (_rms(x) * g_ref[...] * (1.0 + mod_ref[1:2, :]) + mod_ref[0:1, :]).astype(BF16)
    for g in range(FOUR_GROUPS):
        ls = slice(g * FOUR_GC, (g + 1) * FOUR_GC)
        hg = h[:, ls]
        stk_ref[0:L, ls] = _dot(hg, cc_ref[...]).astype(BF16)
        stk_ref[L:2 * L, ls] = _dot(hg, sc_ref[...]).astype(BF16)
    f = _dot(cs_ref[...], stk_ref[...]).astype(BF16)
    out_ref[...] = x + mod_ref[2:3, :] * (_dot(f, w_ref[...]) + b_ref[...])


def _dft_tables(L):
    n = FOUR_GC
    ang_c = 2.0 * np.pi * ((np.arange(n)[:, None] * np.arange(n)[None, :]) % n) / n
    ang_l = 2.0 * np.pi * ((np.arange(L)[:, None] * np.arange(L)[None, :]) % L) / L
    cc = np.cos(ang_c) / math.sqrt(n)
    sc = np.sin(ang_c) / math.sqrt(n)
    cs = np.concatenate([np.cos(ang_l), -np.sin(ang_l)], axis=1) / math.sqrt(L)
    return (jnp.asarray(cc, F32).astype(BF16), jnp.asarray(sc, F32).astype(BF16),
            jnp.asarray(cs, F32).astype(BF16))


def _four_call(x, x_prev, mod, l, i_odd, gain, w, b, *, L, nb, blk0):
    nblk = N_TOK // L
    cc, sc, cs = _dft_tables(L)
    row = _mod_row(L)
    args = [x.reshape(nblk, L, D_MODEL), mod, gain, cc, sc, cs, w, b]
    in_specs = [
        pl.BlockSpec((None, L, D_MODEL), lambda i: (blk0 + i, 0, 0)),
        pl.BlockSpec((None, None, 6, D_MODEL), lambda i: (l, row(blk0 + i), 0, 0)),
        pl.BlockSpec((1, D_MODEL), lambda i: (0, 0)),
        pl.BlockSpec((FOUR_GC, FOUR_GC), lambda i: (0, 0)),
        pl.BlockSpec((FOUR_GC, FOUR_GC), lambda i: (0, 0)),
        pl.BlockSpec((L, 2 * L), lambda i: (0, 0)),
        pl.BlockSpec((None, D_MODEL, D_MODEL), lambda i: (i_odd, 0, 0)),
        pl.BlockSpec((1, D_MODEL), lambda i: (0, 0)),
    ]
    kern = functools.partial(_four_kernel, L=L)
    io_alias = {}
    if x_prev is not None:
        io_alias = {len(args): 0}
        args.append(x_prev.reshape(nblk, L, D_MODEL))
        in_specs.append(pl.BlockSpec(memory_space=pl.ANY))
        kern = _drop_args(kern, len(args) - 1, 1)
    out = pl.pallas_call(
        kern,
        grid=(nb,),
        in_specs=in_specs,
        out_specs=pl.BlockSpec((None, L, D_MODEL), lambda i: (blk0 + i, 0, 0)),
        out_shape=jax.ShapeDtypeStruct((nblk, L, D_MODEL), F32),
        scratch_shapes=[pltpu.VMEM((2 * L, D_MODEL), BF16)],
        input_output_aliases=io_alias,
        compiler_params=_params("parallel"),
        name=f"fourier_L{L}",
    )(*args)
    return out.reshape(N_TOK, D_MODEL)


def _final_kernel(x_ref, g_ref, op_ref, os_ref, *, n_prompt_tiles):
    y = _rms(x_ref[...]) * g_ref[...]

    @pl.when(pl.program_id(0) < n_prompt_tiles)
    def _():
        op_ref[...] = y

    @pl.when(pl.program_id(0) >= n_prompt_tiles)
    def _():
        os_ref[...] = y


def _final_call(x, gain):
    tm = 1024
    npt = P_TOK // tm
    return pl.pallas_call(
        functools.partial(_final_kernel, n_prompt_tiles=npt),
        grid=(N_TOK // tm,),
        in_specs=[pl.BlockSpec((tm, D_MODEL), lambda i: (i, 0)),
                  pl.BlockSpec((1, D_MODEL), lambda i: (0, 0))],
        out_specs=[pl.BlockSpec((tm, D_MODEL), lambda i: (jnp.minimum(i, npt - 1), 0)),
                   pl.BlockSpec((tm, D_MODEL), lambda i: (jnp.maximum(i - npt, 0), 0))],
        out_shape=[jax.ShapeDtypeStruct((P_TOK, D_MODEL), F32),
                   jax.ShapeDtypeStruct((S_TOK, D_MODEL), F32)],
        compiler_params=_params("arbitrary"),
        name="final_norm",
    )(x, gain)


def _rope_lane_tables(L):
    t = jnp.arange(L)
    row = (t // GRID_W).astype(F32)
    col = (t % GRID_W).astype(F32)
    freq = ROPE_THETA ** (-jnp.arange(ROPE_PAIRS, dtype=F32) / ROPE_PAIRS)
    ar = row[:, None] * freq[None]
    ac = col[:, None] * freq[None]
    z = jnp.zeros_like(ar)
    cos32 = lambda a: jnp.concatenate([jnp.cos(a), jnp.cos(a)], axis=-1)
    sa32 = lambda a: jnp.concatenate([-jnp.sin(a), z], axis=-1)
    sb32 = lambda a: jnp.concatenate([z, jnp.sin(a)], axis=-1)
    tile = lambda f: jnp.concatenate([f(ar), f(ac), f(ar), f(ac)], axis=-1)
    return tile(cos32), tile(sa32), tile(sb32)


def _pad_lanes(v, width):
    v = v.reshape(1, -1).astype(F32)
    return jnp.pad(v, ((0, 0), (0, width - v.shape[1])))


def kernel(x_prompt, x_sample, cache_k, cache_v, state_ssd_fwd, state_ssd_bwd, c, c_ctx, w_ada, b_ada, norm_mix, norm_ffn, w_in_ab, conv_w, conv_b, dt_bias, a_log, d_skip, ssd_norm, lambda_qk, subln, w_out_ab, w_four, b_four, w_ffn_in, w_ffn_out, norm_final):
    x = jnp.concatenate([x_prompt.reshape(P_TOK, D_MODEL), x_sample.reshape(S_TOK, D_MODEL)], axis=0)
    cvec = jnp.concatenate([c_ctx[None, :], c, jnp.zeros((N_MOD - 1 - DEC_BATCH, D_MODEL), F32)], axis=0)
    mod = _ada_call(cvec, w_ada, b_ada).reshape(DEPTH, N_MOD, 6, D_MODEL)
    rope = _rope_lane_tables(DEC_SEQ)

    o1, o2, o3 = SSD_INNER, SSD_INNER + CONV_CH, SSD_INNER + CONV_CH + 2 * SSD_HEADS
    w_in_bf = jnp.concatenate(
        [w_in_ab[:, :, o1:o2], w_in_ab[:, :, :o1], w_in_ab[:, :, o3:], w_in_ab[:, :, o2:o3],
         jnp.zeros((N_EVEN, D_MODEL, DT_W - 2 * SSD_HEADS), F32)], axis=2).astype(BF16)
    w_out_bf = w_out_ab.astype(BF16)
    w_four_bf = w_four.astype(BF16)
    w_ffn_in_bf = w_ffn_in.astype(BF16)
    w_ffn_out_bf = w_ffn_out.astype(BF16)

    kv_new = None
    states = (None, None)
    blk_s = P_TOK // DEC_SEQ
    for l in range(DEPTH):
        gain_mix = norm_mix[l].reshape(1, D_MODEL)
        if l % 2 == 0:
            e = l // 2
            lam_init = 0.8 - 0.6 * math.exp(-0.3 * l)
            pa, pb, dtarr, k_new, v_new = _inproj_call(x, mod, l, e, gain_mix, w_in_bf, kv_new)
            kv_new = (k_new, v_new)

            ssd_w = (jnp.pad(conv_w[e], ((0, 8 - CONV_W), (0, 0))), conv_b[e].reshape(1, CONV_CH),
                     _pad_lanes(dt_bias[e], DT_W), _pad_lanes(a_log[e], DT_W),
                     jnp.repeat(d_skip[e], SSD_HD).reshape(1, SSD_INNER),
                     ssd_norm[e].reshape(1, SSD_INNER))
            y, hf, hb = _ssd_call(pa, dtarr, None, states, ssd_w, e, L=SEQ, nb=BATCH, blk0=0)
            states = (hf, hb)
            (y,) = _ssd_call(pa, dtarr, y, None, ssd_w, e, L=DEC_SEQ, nb=DEC_BATCH, blk0=blk_s,
                             h0=(state_ssd_fwd, state_ssd_bwd))
            y = y.reshape(N_TOK, SSD_INNER)
            sub = subln[e].reshape(1, ATT_VD)
            o = _attn_call(pb, None, lambda_qk[e], sub, lam_init, L=SEQ, nb=BATCH, blk0=0, tq=SEQ)
            o = _attn_call(pb, o, lambda_qk[e], sub, lam_init, L=DEC_SEQ, nb=DEC_BATCH,
                           blk0=blk_s, tq=256, ctx=(cache_k, cache_v, e), rope=rope)
            x = _outproj_call(x, y, o, mod, l, e, w_out_bf)
        else:
            i_odd = l // 2
            b4 = b_four[i_odd].reshape(1, D_MODEL)
            xn = _four_call(x, None, mod, l, i_odd, gain_mix, w_four_bf, b4, L=SEQ, nb=BATCH, blk0=0)
            x = _four_call(x, xn, mod, l, i_odd, gain_mix, w_four_bf, b4, L=DEC_SEQ, nb=DEC_BATCH,
                           blk0=blk_s)
        x = _ffn_call(x, mod, l, norm_ffn[l].reshape(1, D_MODEL), w_ffn_in_bf, w_ffn_out_bf)

    y_p, y_s = _final_call(x, norm_final.reshape(1, D_MODEL))
    return (y_p.reshape(BATCH, SEQ, D_MODEL), y_s.reshape(DEC_BATCH, DEC_SEQ, D_MODEL),
            kv_new[0].reshape(BATCH, N_EVEN, SEQ, ATT_HEADS, 2, ATT_HD),
            kv_new[1].reshape(BATCH, N_EVEN, SEQ, ATT_HEADS, ATT_VD),
            states[0].reshape(BATCH, N_EVEN, SSD_HEADS, SSD_HD, D_STATE),
            states[1].reshape(BATCH, N_EVEN, SSD_HEADS, SSD_HD, D_STATE))
```

```python
import functools
import math

import numpy as np
import jax
import jax.numpy as jnp
from jax import lax
from jax.experimental import pallas as pl
from jax.experimental.pallas import tpu as pltpu

D_MODEL = 1024
BATCH = 16
SEQ = 256
DEPTH = 4
N_EVEN = 2
DEC_BATCH = 4
DEC_SEQ = 1024
PAST_LEN = 512
GRID_W = 64
SSD_HEADS = 16
SSD_HD = 64
SSD_INNER = 1024
SSD_GROUPS = 4
D_STATE = 128
CONV_W = 5
CONV_CH = 2048
CHUNK = 128
ATT_HEADS = 8
ATT_HD = 64
ATT_VD = 128
ATT_INNER = 1024
ROPE_THETA = 10000.0
ROPE_PAIRS = 16
FOUR_GROUPS = 4
FOUR_GC = D_MODEL // FOUR_GROUPS
FFN_HIDDEN = 2816
EPS = 1e-6

P_TOK = BATCH * SEQ
S_TOK = DEC_BATCH * DEC_SEQ
N_TOK = P_TOK + S_TOK
N_MOD = 8
PA_W = CONV_CH + SSD_INNER
PB_W = 3 * ATT_INNER
DT_W = 128
NH2 = SSD_HEADS // 2
LANE = 128
SUBLANE = 8
N_SLAB = CONV_CH // LANE
QBLK = 32
MXU_N = 256
VMEM_LIMIT = 56 * 1024 * 1024

F32 = jnp.float32
BF16 = jnp.bfloat16


def _params(*sem):
    return pltpu.CompilerParams(dimension_semantics=sem, vmem_limit_bytes=VMEM_LIMIT)


def _dot(a, b):
    return jnp.dot(a, b, preferred_element_type=F32)


def _dot_nt(a, b):
    return lax.dot_general(a, b, (((1,), (1,)), ((), ())), preferred_element_type=F32)


def _silu(x):
    return x * jax.nn.sigmoid(x)


def _rms(x):
    return x * lax.rsqrt(jnp.mean(x * x, axis=-1, keepdims=True) + EPS)


def _mod_row(tm):
    return lambda i: jnp.maximum((i * tm) // DEC_SEQ - (P_TOK // DEC_SEQ - 1), 0)


def _drop_args(kern, start, n):
    def wrapped(*refs):
        return kern(*(refs[:start] + refs[start + n:]))
    return wrapped


def _ada_kernel(c_ref, w_ref, b_ref, o_ref):
    s = _silu(c_ref[...]).astype(BF16)
    o_ref[...] = _dot(s, w_ref[...].astype(BF16)) + b_ref[...]


def _ada_call(cvec, w_ada, b_ada):
    tn = 1024
    return pl.pallas_call(
        _ada_kernel,
        grid=(DEPTH, 6 * D_MODEL // tn),
        in_specs=[
            pl.BlockSpec((N_MOD, D_MODEL), lambda l, j: (0, 0)),
            pl.BlockSpec((None, D_MODEL, tn), lambda l, j: (l, 0, j)),
            pl.BlockSpec((None, 1, tn), lambda l, j: (l, 0, j)),
        ],
        out_specs=pl.BlockSpec((None, N_MOD, tn), lambda l, j: (l, 0, j)),
        out_shape=jax.ShapeDtypeStruct((DEPTH, N_MOD, 6 * D_MODEL), F32),
        compiler_params=_params("parallel", "parallel"),
        name="adaln",
    )(cvec, w_ada, b_ada.reshape(DEPTH, 1, 6 * D_MODEL))


def _inproj_kernel(x_ref, mod_ref, g_ref, w_ref, xbc_ref, z_ref, pb_ref, dt_ref, kn_ref, vn_ref,
                   *, n_prompt_tiles, seqs_per_tile):
    h = _rms(x_ref[...]) * g_ref[...] * (1.0 + mod_ref[1:2, :]) + mod_ref[0:1, :]
    hb = h.astype(BF16)
    tn = ATT_INNER
    for j in range(CONV_CH // tn):
        res = _dot(hb, w_ref[:, j * tn:(j + 1) * tn])
        for s in range(tn // LANE):
            xbc_ref[j * (tn // LANE) + s] = res[:, s * LANE:(s + 1) * LANE]
    z_ref[...] = _dot(hb, w_ref[:, CONV_CH:PA_W])
    q = _dot(hb, w_ref[:, PA_W:PA_W + tn])
    pb_ref[:, 0:tn] = q.astype(BF16)
    k = _dot(hb, w_ref[:, PA_W + tn:PA_W + 2 * tn])
    pb_ref[:, tn:2 * tn] = k.astype(BF16)
    v = _dot(hb, w_ref[:, PA_W + 2 * tn:PA_W + 3 * tn])
    pb_ref[:, 2 * tn:3 * tn] = v.astype(BF16)
    dt_ref[...] = _dot(hb, w_ref[:, PA_W + PB_W:PA_W + PB_W + DT_W])

    @pl.when(pl.program_id(0) < n_prompt_tiles)
    def _():
        kn_ref[...] = k.reshape(seqs_per_tile, SEQ, ATT_INNER)
        vn_ref[...] = v.reshape(seqs_per_tile, SEQ, ATT_INNER)


def _inproj_call(x, mod, l, e, gain, w_in, kv_prev):
    tm = 512
    row = _mod_row(tm)
    npt = P_TOK // tm
    spt = tm // SEQ
    args = [x, mod, gain, w_in]
    in_specs = [
        pl.BlockSpec((tm, D_MODEL), lambda i: (i, 0)),
        pl.BlockSpec((None, None, 6, D_MODEL), lambda i: (l, row(i), 0, 0)),
        pl.BlockSpec((1, D_MODEL), lambda i: (0, 0)),
        pl.BlockSpec((None, D_MODEL, PA_W + PB_W + DT_W), lambda i: (e, 0, 0),
                     pipeline_mode=pl.Buffered(1)),
    ]
    kern = functools.partial(_inproj_kernel, n_prompt_tiles=npt, seqs_per_tile=spt)
    io_alias = {}
    if kv_prev is not None:
        io_alias = {len(args): 4, len(args) + 1: 5}
        args += list(kv_prev)
        in_specs += [pl.BlockSpec(memory_space=pl.ANY)] * 2
        kern = _drop_args(kern, len(args) - 2, 2)
    kv_spec = pl.BlockSpec((spt, None, SEQ, ATT_INNER),
                           lambda i: (jnp.minimum(i, npt - 1), e, 0, 0))
    kv_shape = jax.ShapeDtypeStruct((BATCH, N_EVEN, SEQ, ATT_INNER), F32)
    return pl.pallas_call(
        kern,
        grid=(N_TOK // tm,),
        in_specs=in_specs,
        out_specs=[
            pl.BlockSpec((N_SLAB, tm, LANE), lambda i: (0, i, 0)),
            pl.BlockSpec((tm, SSD_INNER), lambda i: (i, 0)),
            pl.BlockSpec((tm, PB_W), lambda i: (i, 0)),
            pl.BlockSpec((tm, DT_W), lambda i: (i, 0)),
            kv_spec, kv_spec,
        ],
        out_shape=[
            jax.ShapeDtypeStruct((N_SLAB, N_TOK, LANE), F32),
            jax.ShapeDtypeStruct((N_TOK, SSD_INNER), F32),
            jax.ShapeDtypeStruct((N_TOK, PB_W), BF16),
            jax.ShapeDtypeStruct((N_TOK, DT_W), F32),
            kv_shape, kv_shape,
        ],
        input_output_aliases=io_alias,
        compiler_params=_params("arbitrary"),
        name="inproj",
    )(*args)


def _ssd_kernel(*refs, L, has_h0, emit_state):
    it = iter(refs)
    xbc_ref, z_ref, dt_ref = next(it), next(it), next(it)
    convw_ref, convb_ref, dtb_ref, alog_ref, dskip_ref, norm_ref = (next(it) for _ in range(6))
    sel_refs = (next(it), next(it))
    if has_h0:
        h0_refs = (next(it), next(it))
    y_ref = next(it)
    if emit_state:
        hout_refs = (next(it), next(it))
    xact, xdt_f, xdt_b, bt, cact, tmp, yacc, dtv, state = (next(it) for _ in range(9))

    nc = L // CHUNK
    nq = L // SUBLANE
    half_w = CONV_W // 2

    rq = lax.broadcasted_iota(jnp.int32, (QBLK, LANE), 0)

    def phase(s, pe, q0):
        r0 = SUBLANE * q0 + pe
        if r0 < 0:
            a = xbc_ref[s, pl.ds(r0 + SUBLANE, QBLK, stride=SUBLANE), :]
            return jnp.where(rq == 0, 0.0, pltpu.roll(a, 1, 0))
        if r0 + SUBLANE * (QBLK - 1) >= L:
            a = xbc_ref[s, pl.ds(r0 - SUBLANE, QBLK, stride=SUBLANE), :]
            return jnp.where(rq == QBLK - 1, 0.0, pltpu.roll(a, QBLK - 1, 0))
        return xbc_ref[s, pl.ds(r0, QBLK, stride=SUBLANE), :]

    def conv_slab(s, dst, d_idx):
        ls = slice(s * LANE, (s + 1) * LANE)
        for q0 in range(0, nq, QBLK):
            ext = {pe: phase(s, pe, q0) for pe in range(-half_w, SUBLANE + half_w)}
            for p in range(SUBLANE):
                acc = jnp.broadcast_to(convb_ref[:, ls], (QBLK, LANE))
                for k in range(CONV_W):
                    acc = acc + convw_ref[k:k + 1, ls] * ext[p + k - half_w]
                dst[d_idx, pl.ds(SUBLANE * q0 + p, QBLK, stride=SUBLANE), :] = _silu(acc)

    for m in range(NH2):
        conv_slab(m, xact, m)
    for g in range(SSD_GROUPS):
        conv_slab(NH2 + g, tmp, g)
    for c in range(nc):
        for g in range(SSD_GROUPS):
            bt[c, g * D_STATE:(g + 1) * D_STATE, :] = (
                tmp[g, c * CHUNK:(c + 1) * CHUNK, :].T.astype(BF16))
    for g in range(SSD_GROUPS):
        conv_slab(NH2 + SSD_GROUPS + g, tmp, g)
    for c in range(nc):
        rows = slice(c * CHUNK, (c + 1) * CHUNK)
        for g in range(SSD_GROUPS):
            cact[g, rows, :] = tmp[g, rows, :].astype(BF16)

    dr = dt_ref[...] + dtb_ref[...]
    dtv[...] = jnp.maximum(dr, 0.0) + jnp.log1p(jnp.exp(-jnp.abs(dr)))
    a_row = -jnp.exp(alog_ref[...])

    for c in range(nc):
        rows = slice(c * CHUNK, (c + 1) * CHUNK)
        d = dtv[rows, :]
        d_hi = d.astype(BF16)
        d_lo = (d - d_hi.astype(F32)).astype(BF16)
        d2 = jnp.concatenate([d_hi, d_lo], axis=1)
        spread_f = _dot(d2, sel_refs[0][...])
        spread_b = _dot(d2, sel_refs[1][...])
        for m in range(NH2):
            lanes = slice(m * LANE, (m + 1) * LANE)
            xm = xact[m, rows, :]
            xdt_f[m, rows, :] = (xm * spread_f[:, lanes]).astype(BF16)
            xdt_b[m, rows, :] = (xm * spread_b[:, lanes]).astype(BF16)

    ri = lax.broadcasted_iota(jnp.int32, (CHUNK, CHUNK), 0)
    ci = lax.broadcasted_iota(jnp.int32, (CHUNK, CHUNK), 1)
    low_half = ci < SSD_HD
    low_row = low_half[0:1, :]

    def run_direction(fwd):
        mask = (ri >= ci) if fwd else (ri <= ci)
        tri = mask.astype(F32)
        col0 = 0 if fwd else SSD_HEADS
        last = CHUNK - 1 if fwd else 0
        xdt = xdt_f if fwd else xdt_b
        d_idx = 0 if fwd else 1

        for m in range(NH2):
            if has_h0:
                state[m] = h0_refs[d_idx][m].T
            else:
                state[m] = jnp.zeros((D_STATE, LANE), F32)

        def chunk_body(step, carry):
            c = step if fwd else nc - 1 - step
            r0 = pl.multiple_of(c * CHUNK, CHUNK)
            rows = pl.ds(r0, CHUNK)
            dtc = dtv[rows, :]
            acs = jnp.dot(tri, dtc * a_row, preferred_element_type=F32,
                          precision=lax.Precision.HIGHEST)
            acs_t = acs.T
            acs_last = acs[last:last + 1, :]
            for g in range(SSD_GROUPS):
                btg = bt[c, g * D_STATE:(g + 1) * D_STATE, :]
                cg = cact[g, rows, :]
                cbm = jnp.where(mask, _dot(cg, btg), 0.0)
                for pr in range(2):
                    m = g * 2 + pr
                    ha, hb = col0 + 2 * m, col0 + 2 * m + 1
                    lanes = slice(m * LANE, (m + 1) * LANE)
                    xd2 = xdt[m, rows, :]
                    acol_a = jnp.broadcast_to(acs[:, ha:ha + 1], (CHUNK, CHUNK))
                    acol_b = jnp.broadcast_to(acs[:, hb:hb + 1], (CHUNK, CHUNK))
                    w_a = cbm * jnp.exp(jnp.minimum(acol_a - acs_t[ha:ha + 1, :], 0.0))
                    w_b = cbm * jnp.exp(jnp.minimum(acol_b - acs_t[hb:hb + 1, :], 0.0))
                    w2 = jnp.concatenate([w_a.astype(BF16), w_b.astype(BF16)], axis=1)
                    zero = jnp.zeros_like(xd2)
                    x_blk = jnp.concatenate([jnp.where(low_half, xd2, zero),
                                             jnp.where(low_half, zero, xd2)], axis=0)
                    y_in = _dot(w2, x_blk)
                    acol2 = jnp.where(low_half, acol_a, acol_b)
                    last2 = jnp.where(low_row, acs_last[:, ha:ha + 1], acs_last[:, hb:hb + 1])
                    h_prev = state[m]
                    y_out = _dot(cg, h_prev.astype(BF16)) * jnp.exp(acol2)
                    to_end = jnp.exp(last2 - acol2)
                    st = _dot(btg, (to_end * xd2.astype(F32)).astype(BF16))
                    state[m] = jnp.exp(last2) * h_prev + st
                    y2 = y_in + y_out
                    if fwd:
                        yacc[rows, lanes] = y2 + dskip_ref[:, lanes] * xact[m, rows, :]
                    else:
                        yacc[rows, lanes] = yacc[rows, lanes] + y2
            if not fwd:
                yt = yacc[rows, :] * _silu(z_ref[rows, :])
                y_ref[rows, :] = (_rms(yt) * norm_ref[...]).astype(y_ref.dtype)
            return carry

        lax.fori_loop(0, nc, chunk_body, 0)
        if emit_state:
            for m in range(NH2):
                hout_refs[d_idx][m] = state[m].T

    run_direction(True)
    run_direction(False)


def _head_spread_tables():
    ch_head = np.arange(SSD_INNER) // SSD_HD
    out = []
    for d in range(2):
        k = np.arange(2 * DT_W) % DT_W
        out.append(jnp.asarray(k[:, None] == (d * SSD_HEADS + ch_head)[None, :], BF16))
    return out


def _ssd_call(xbc, z, dtarr, y_prev, states_prev, weights, e, *, L, nb, blk0, h0=None):
    emit_state = states_prev is not None
    nblk = N_TOK // L
    nc = L // CHUNK
    vec = lambda w: pl.BlockSpec((1, w), lambda b: (0, 0))
    in_specs = [
        pl.BlockSpec((N_SLAB, L, LANE), lambda b: (0, blk0 + b, 0)),
        pl.BlockSpec((None, L, SSD_INNER), lambda b: (blk0 + b, 0, 0)),
        pl.BlockSpec((None, L, DT_W), lambda b: (blk0 + b, 0, 0)),
        pl.BlockSpec((8, CONV_CH), lambda b: (0, 0)),
        vec(CONV_CH), vec(DT_W), vec(DT_W), vec(SSD_INNER), vec(SSD_INNER),
        pl.BlockSpec((2 * DT_W, SSD_INNER), lambda b: (0, 0)),
        pl.BlockSpec((2 * DT_W, SSD_INNER), lambda b: (0, 0)),
    ]
    args = [xbc, z.reshape(nblk, L, SSD_INNER), dtarr.reshape(nblk, L, DT_W), *weights,
            *_head_spread_tables()]
    state_spec = pl.BlockSpec((None, None, NH2, LANE, D_STATE), lambda b: (b, e, 0, 0, 0))
    if h0 is not None:
        for h in h0:
            args.append(h.reshape(DEC_BATCH, N_EVEN, NH2, LANE, D_STATE))
            in_specs.append(state_spec)
    kern = functools.partial(_ssd_kernel, L=L, has_h0=h0 is not None, emit_state=emit_state)
    io_alias = {}
    n_alias = 0
    prevs = [None if y_prev is None else y_prev.reshape(nblk, L, SSD_INNER)]
    prevs += list(states_prev) if emit_state else []
    for k, prev in enumerate(prevs):
        if prev is not None:
            io_alias[len(args)] = k
            args.append(prev)
            in_specs.append(pl.BlockSpec(memory_space=pl.ANY))
            n_alias += 1
    kern = _drop_args(kern, len(args) - n_alias, n_alias)
    out_specs = [pl.BlockSpec((None, L, SSD_INNER), lambda b: (blk0 + b, 0, 0))]
    out_shape = [jax.ShapeDtypeStruct((nblk, L, SSD_INNER), BF16)]
    if emit_state:
        out_specs += [state_spec, state_spec]
        out_shape += [jax.ShapeDtypeStruct((nb, N_EVEN, NH2, LANE, D_STATE), F32)] * 2
    return pl.pallas_call(
        kern,
        grid=(nb,),
        in_specs=in_specs,
        out_specs=out_specs,
        out_shape=out_shape,
        scratch_shapes=[
            pltpu.VMEM((NH2, L, LANE), F32),
            pltpu.VMEM((NH2, L, LANE), BF16),
            pltpu.VMEM((NH2, L, LANE), BF16),
            pltpu.VMEM((nc, SSD_GROUPS * D_STATE, CHUNK), BF16),
            pltpu.VMEM((SSD_GROUPS, L, D_STATE), BF16),
            pltpu.VMEM((SSD_GROUPS, L, D_STATE), F32),
            pltpu.VMEM((L, SSD_INNER), F32),
            pltpu.VMEM((L, DT_W), F32),
            pltpu.VMEM((NH2, D_STATE, LANE), F32),
        ],
        input_output_aliases=io_alias,
        compiler_params=_params("parallel"),
        name=f"ssd_L{L}",
    )(*args)


def _rope(x, cos, sin_a, sin_b):
    return (x * cos + pltpu.roll(x, LANE - ROPE_PAIRS, 1) * sin_a
            + pltpu.roll(x, ROPE_PAIRS, 1) * sin_b)


def _attn_kernel(*refs, L, n_ctx, lam_init):
    it = iter(refs)
    q_ref, k_ref, v_ref = next(it), next(it), next(it)
    if n_ctx:
        ck_ref, cv_ref = next(it), next(it)
        cq, saq, sbq, ck, sak, sbk = (next(it) for _ in range(6))
    lam_ref, sub_ref = next(it), next(it)
    o_ref = next(it)
    kall, vext = next(it), next(it)
    lk = n_ctx + L

    @pl.when(pl.program_id(1) == 0)
    def _():
        lane = lax.broadcasted_iota(jnp.int32, (lk, MXU_N - ATT_VD), 1)
        ones_col = jnp.where(lane == 0, 1.0, 0.0).astype(BF16)
        for h in range(ATT_HEADS):
            hs = slice(h * ATT_VD, (h + 1) * ATT_VD)
            if n_ctx:
                kall[h, 0:n_ctx, :] = ck_ref[:, hs].astype(BF16)
                vext[h, 0:n_ctx, 0:ATT_VD] = cv_ref[:, hs].astype(BF16)
                kh = _rope(k_ref[:, hs].astype(F32), ck[...], sak[...], sbk[...])
                kall[h, n_ctx:lk, :] = kh.astype(BF16)
            else:
                kall[h] = k_ref[:, hs]
            vext[h, n_ctx:lk, 0:ATT_VD] = v_ref[:, hs]
            vext[h, :, ATT_VD:MXU_N] = ones_col

    lane = lax.broadcasted_iota(jnp.int32, (1, ATT_VD), 1)
    scale = ATT_HD ** -0.5
    m1 = jnp.where(lane < ATT_HD, scale, 0.0)
    m2 = jnp.where(lane < ATT_HD, 0.0, scale)
    lq = lam_ref[...]
    lam = (jnp.exp(jnp.sum(lq[0:1, :] * lq[1:2, :], axis=-1, keepdims=True))
           - jnp.exp(jnp.sum(lq[2:3, :] * lq[3:4, :], axis=-1, keepdims=True)) + lam_init)

    for h in range(ATT_HEADS):
        hs = slice(h * ATT_VD, (h + 1) * ATT_VD)
        q = q_ref[:, hs].astype(F32)
        if n_ctx:
            q = _rope(q, cq[...], saq[...], sbq[...])
        kk = kall[h]
        vv = vext[h]

        def attend(qm):
            s = _dot_nt(qm.astype(BF16), kk)
            p = jnp.exp(s - jnp.max(s, axis=-1, keepdims=True)).astype(BF16)
            pv = _dot(p, vv)
            return pv[:, 0:ATT_VD] / pv[:, ATT_VD:ATT_VD + 1]

        o = attend(q * m1) - lam * attend(q * m2)
        o_ref[:, hs] = (_rms(o) * sub_ref[...] * (1.0 - lam_init)).astype(o_ref.dtype)


def _attn_call(pb, o_prev, lam_qk, subln, lam_init, *, L, nb, blk0, tq, ctx=None, rope=None):
    nblk = N_TOK // L
    nq = L // tq
    pb3 = pb.reshape(nblk, L, PB_W)
    in_specs = [
        pl.BlockSpec((None, tq, ATT_INNER), lambda b, i: (blk0 + b, i, 0)),
        pl.BlockSpec((None, L, ATT_INNER), lambda b, i: (blk0 + b, 0, 1)),
        pl.BlockSpec((None, L, ATT_INNER), lambda b, i: (blk0 + b, 0, 2)),
    ]
    args = [pb3, pb3, pb3]
    n_ctx = 0
    if ctx is not None:
        cache_k, cache_v, e = ctx
        n_ctx = cache_k.shape[2]
        for cch in (cache_k, cache_v):
            args.append(cch.reshape(DEC_BATCH, N_EVEN, n_ctx, ATT_INNER))
            in_specs.append(pl.BlockSpec((None, None, n_ctx, ATT_INNER), lambda b, i: (b, e, 0, 0)))
        for t in rope:
            args.append(t)
            in_specs.append(pl.BlockSpec((tq, ATT_VD), lambda b, i: (i, 0)))
        for t in rope:
            args.append(t)
            in_specs.append(pl.BlockSpec((L, ATT_VD), lambda b, i: (0, 0)))
    args += [lam_qk, subln]
    in_specs += [pl.BlockSpec((4, ATT_HD), lambda b, i: (0, 0)),
                 pl.BlockSpec((1, ATT_VD), lambda b, i: (0, 0))]
    kern = functools.partial(_attn_kernel, L=L, n_ctx=n_ctx, lam_init=lam_init)
    io_alias = {}
    if o_prev is not None:
        io_alias = {len(args): 0}
        args.append(o_prev.reshape(nblk, L, ATT_INNER))
        in_specs.append(pl.BlockSpec(memory_space=pl.ANY))
        kern = _drop_args(kern, len(args) - 1, 1)
    out = pl.pallas_call(
        kern,
        grid=(nb, nq),
        in_specs=in_specs,
        out_specs=pl.BlockSpec((None, tq, ATT_INNER), lambda b, i: (blk0 + b, i, 0)),
        out_shape=jax.ShapeDtypeStruct((nblk, L, ATT_INNER), BF16),
        scratch_shapes=[pltpu.VMEM((ATT_HEADS, n_ctx + L, ATT_VD), BF16),
                        pltpu.VMEM((ATT_HEADS, n_ctx + L, MXU_N), BF16)],
        input_output_aliases=io_alias,
        compiler_params=_params("parallel", "arbitrary"),
        name=f"attn_L{L}",
    )(*args)
    return out.reshape(N_TOK, ATT_INNER)


def _outproj_kernel(x_ref, y_ref, o_ref, mod_ref, w_ref, out_ref):
    acc = _dot(y_ref[...], w_ref[0:SSD_INNER, :]) + _dot(o_ref[...], w_ref[SSD_INNER:, :])
    out_ref[...] = x_ref[...] + mod_ref[2:3, :] * acc


def _outproj_call(x, y, o, mod, l, e, w):
    tm = 512
    row = _mod_row(tm)
    tok = lambda: pl.BlockSpec((tm, D_MODEL), lambda i: (i, 0))
    return pl.pallas_call(
        _outproj_kernel,
        grid=(N_TOK // tm,),
        in_specs=[tok(), tok(), tok(),
                  pl.BlockSpec((None, None, 6, D_MODEL), lambda i: (l, row(i), 0, 0)),
                  pl.BlockSpec((None, SSD_INNER + ATT_INNER, D_MODEL), lambda i: (e, 0, 0))],
        out_specs=tok(),
        out_shape=jax.ShapeDtypeStruct((N_TOK, D_MODEL), F32),
        compiler_params=_params("parallel"),
        name="outproj",
    )(x, y, o, mod, w)


def _ffn_kernel(x_ref, mod_ref, g_ref, win_ref, wout_ref, out_ref, h_ref, a_ref):
    h = _rms(x_ref[...]) * g_ref[...] * (1.0 + mod_ref[4:5, :]) + mod_ref[3:4, :]
    h_ref[...] = h.astype(BF16)
    for j in range(FFN_HIDDEN // MXU_N):
        cs = slice(j * MXU_N, (j + 1) * MXU_N)
        us = slice(FFN_HIDDEN + j * MXU_N, FFN_HIDDEN + (j + 1) * MXU_N)
        hb = h_ref[...]
        a_ref[:, cs] = (_silu(_dot(hb, win_ref[:, cs])) * _dot(hb, win_ref[:, us])).astype(BF16)
    out_ref[...] = x_ref[...] + mod_ref[5:6, :] * _dot(a_ref[...], wout_ref[...])


def _ffn_call(x, mod, l, gain, w_in, w_out):
    tm = 1024
    row = _mod_row(tm)
    once = pl.Buffered(1)
    return pl.pallas_call(
        _ffn_kernel,
        grid=(N_TOK // tm,),
        in_specs=[
            pl.BlockSpec((tm, D_MODEL), lambda i: (i, 0)),
            pl.BlockSpec((None, None, 6, D_MODEL), lambda i: (l, row(i), 0, 0)),
            pl.BlockSpec((1, D_MODEL), lambda i: (0, 0)),
            pl.BlockSpec((None, D_MODEL, 2 * FFN_HIDDEN), lambda i: (l, 0, 0), pipeline_mode=once),
            pl.BlockSpec((None, FFN_HIDDEN, D_MODEL), lambda i: (l, 0, 0), pipeline_mode=once),
        ],
        out_specs=pl.BlockSpec((tm, D_MODEL), lambda i: (i, 0)),
        out_shape=jax.ShapeDtypeStruct((N_TOK, D_MODEL), F32),
        scratch_shapes=[pltpu.VMEM((tm, D_MODEL), BF16), pltpu.VMEM((tm, FFN_HIDDEN), BF16)],
        compiler_params=_params("parallel"),
        name="ffn",
    )(x, mod, gain, w_in, w_out)


def _four_kernel(x_ref, mod_ref, g_ref, cc_ref, sc_ref, cs_ref, w_ref, b_ref, out_ref, stk_ref, *, L):
    x = x_ref[...]
    h = (_rms(x) * g_ref[...] * (1.0 + mod_ref[1:2, :]) + mod_ref[0:1, :]).astype(BF16)
    for g in range(FOUR_GROUPS):
        ls = slice(g * FOUR_GC, (g + 1) * FOUR_GC)
        hg = h[:, ls]
        stk_ref[0:L, ls] = _dot(hg, cc_ref[...]).astype(BF16)
        stk_ref[L:2 * L, ls] = _dot(hg, sc_ref[...]).astype(BF16)
    f = _dot(cs_ref[...], stk_ref[...]).astype(BF16)
    out_ref[...] = x + mod_ref[2:3, :] * (_dot(f, w_ref[...]) + b_ref[...])


def _dft_tables(L):
    n = FOUR_GC
    ang_c = 2.0 * np.pi * ((np.arange(n)[:, None] * np.arange(n)[None, :]) % n) / n
    ang_l = 2.0 * np.pi * ((np.arange(L)[:, None] * np.arange(L)[None, :]) % L) / L
    cc = np.cos(ang_c) / math.sqrt(n)
    sc = np.sin(ang_c) / math.sqrt(n)
    cs = np.concatenate([np.cos(ang_l), -np.sin(ang_l)], axis=1) / math.sqrt(L)
    return (jnp.asarray(cc, F32).astype(BF16), jnp.asarray(sc, F32).astype(BF16),
            jnp.asarray(cs, F32).astype(BF16))


def _four_call(x, x_prev, mod, l, i_odd, gain, w, b, *, L, nb, blk0):
    nblk = N_TOK // L
    cc, sc, cs = _dft_tables(L)
    row = _mod_row(L)
    args = [x.reshape(nblk, L, D_MODEL), mod, gain, cc, sc, cs, w, b]
    in_specs = [
        pl.BlockSpec((None, L, D_MODEL), lambda i: (blk0 + i, 0, 0)),
        pl.BlockSpec((None, None, 6, D_MODEL), lambda i: (l, row(blk0 + i), 0, 0)),
        pl.BlockSpec((1, D_MODEL), lambda i: (0, 0)),
        pl.BlockSpec((FOUR_GC, FOUR_GC), lambda i: (0, 0)),
        pl.BlockSpec((FOUR_GC, FOUR_GC), lambda i: (0, 0)),
        pl.BlockSpec((L, 2 * L), lambda i: (0, 0)),
        pl.BlockSpec((None, D_MODEL, D_MODEL), lambda i: (i_odd, 0, 0)),
        pl.BlockSpec((1, D_MODEL), lambda i: (0, 0)),
    ]
    kern = functools.partial(_four_kernel, L=L)
    io_alias = {}
    if x_prev is not None:
        io_alias = {len(args): 0}
        args.append(x_prev.reshape(nblk, L, D_MODEL))
        in_specs.append(pl.BlockSpec(memory_space=pl.ANY))
        kern = _drop_args(kern, len(args) - 1, 1)
    out = pl.pallas_call(
        kern,
        grid=(nb,),
        in_specs=in_specs,
        out_specs=pl.BlockSpec((None, L, D_MODEL), lambda i: (blk0 + i, 0, 0)),
        out_shape=jax.ShapeDtypeStruct((nblk, L, D_MODEL), F32),
        scratch_shapes=[pltpu.VMEM((2 * L, D_MODEL), BF16)],
        input_output_aliases=io_alias,
        compiler_params=_params("parallel"),
        name=f"fourier_L{L}",
    )(*args)
    return out.reshape(N_TOK, D_MODEL)


def _final_kernel(x_ref, g_ref, op_ref, os_ref, *, n_prompt_tiles):
    y = _rms(x_ref[...]) * g_ref[...]

    @pl.when(pl.program_id(0) < n_prompt_tiles)
    def _():
        op_ref[...] = y

    @pl.when(pl.program_id(0) >= n_prompt_tiles)
    def _():
        os_ref[...] = y


def _final_call(x, gain):
    tm = 1024
    npt = P_TOK // tm
    return pl.pallas_call(
        functools.partial(_final_kernel, n_prompt_tiles=npt),
        grid=(N_TOK // tm,),
        in_specs=[pl.BlockSpec((tm, D_MODEL), lambda i: (i, 0)),
                  pl.BlockSpec((1, D_MODEL), lambda i: (0, 0))],
        out_specs=[pl.BlockSpec((tm, D_MODEL), lambda i: (jnp.minimum(i, npt - 1), 0)),
                   pl.BlockSpec((tm, D_MODEL), lambda i: (jnp.maximum(i - npt, 0), 0))],
        out_shape=[jax.ShapeDtypeStruct((P_TOK, D_MODEL), F32),
                   jax.ShapeDtypeStruct((S_TOK, D_MODEL), F32)],
        compiler_params=_params("arbitrary"),
        name="final_norm",
    )(x, gain)


def _rope_lane_tables(L):
    t = jnp.arange(L)
    row = (t // GRID_W).astype(F32)
    col = (t % GRID_W).astype(F32)
    freq = ROPE_THETA ** (-jnp.arange(ROPE_PAIRS, dtype=F32) / ROPE_PAIRS)
    ar = row[:, None] * freq[None]
    ac = col[:, None] * freq[None]
    z = jnp.zeros_like(ar)
    cos32 = lambda a: jnp.concatenate([jnp.cos(a), jnp.cos(a)], axis=-1)
    sa32 = lambda a: jnp.concatenate([-jnp.sin(a), z], axis=-1)
    sb32 = lambda a: jnp.concatenate([z, jnp.sin(a)], axis=-1)
    tile = lambda f: jnp.concatenate([f(ar), f(ac), f(ar), f(ac)], axis=-1)
    return tile(cos32), tile(sa32), tile(sb32)


def _pad_lanes(v, width):
    v = v.reshape(1, -1).astype(F32)
    return jnp.pad(v, ((0, 0), (0, width - v.shape[1])))


def kernel(x_prompt, x_sample, cache_k, cache_v, state_ssd_fwd, state_ssd_bwd, c, c_ctx, w_ada, b_ada, norm_mix, norm_ffn, w_in_ab, conv_w, conv_b, dt_bias, a_log, d_skip, ssd_norm, lambda_qk, subln, w_out_ab, w_four, b_four, w_ffn_in, w_ffn_out, norm_final):
    x = jnp.concatenate([x_prompt.reshape(P_TOK, D_MODEL), x_sample.reshape(S_TOK, D_MODEL)], axis=0)
    cvec = jnp.concatenate([c_ctx[None, :], c, jnp.zeros((N_MOD - 1 - DEC_BATCH, D_MODEL), F32)], axis=0)
    mod = _ada_call(cvec, w_ada, b_ada).reshape(DEPTH, N_MOD, 6, D_MODEL)
    rope = _rope_lane_tables(DEC_SEQ)

    o1, o2, o3 = SSD_INNER, SSD_INNER + CONV_CH, SSD_INNER + CONV_CH + 2 * SSD_HEADS
    w_in_bf = jnp.concatenate(
        [w_in_ab[:, :, o1:o2], w_in_ab[:, :, :o1], w_in_ab[:, :, o3:], w_in_ab[:, :, o2:o3],
         jnp.zeros((N_EVEN, D_MODEL, DT_W - 2 * SSD_HEADS), F32)], axis=2).astype(BF16)
    w_out_bf = w_out_ab.astype(BF16)
    w_four_bf = w_four.astype(BF16)
    w_ffn_in_bf = w_ffn_in.astype(BF16)
    w_ffn_out_bf = w_ffn_out.astype(BF16)

    kv_new = None
    states = (None, None)
    blk_s = P_TOK // DEC_SEQ
    for l in range(DEPTH):
        gain_mix = norm_mix[l].reshape(1, D_MODEL)
        if l % 2 == 0:
            e = l // 2
            lam_init = 0.8 - 0.6 * math.exp(-0.3 * l)
            xbc, z, pb, dtarr, k_new, v_new = _inproj_call(x, mod, l, e, gain_mix, w_in_bf, kv_new)
            kv_new = (k_new, v_new)

            ssd_w = (jnp.pad(conv_w[e], ((0, 8 - CONV_W), (0, 0))), conv_b[e].reshape(1, CONV_CH),
                     _pad_lanes(dt_bias[e], DT_W), _pad_lanes(a_log[e], DT_W),
                     jnp.repeat(d_skip[e], SSD_HD).reshape(1, SSD_INNER),
                     ssd_norm[e].reshape(1, SSD_INNER))
            y, hf, hb = _ssd_call(xbc, z, dtarr, None, states, ssd_w, e, L=SEQ, nb=BATCH, blk0=0)
            states = (hf, hb)
            (y,) = _ssd_call(xbc, z, dtarr, y, None, ssd_w, e, L=DEC_SEQ, nb=DEC_BATCH, blk0=blk_s,
                             h0=(state_ssd_fwd, state_ssd_bwd))
            y = y.reshape(N_TOK, SSD_INNER)
            sub = subln[e].reshape(1, ATT_VD)
            o = _attn_call(pb, None, lambda_qk[e], sub, lam_init, L=SEQ, nb=BATCH, blk0=0, tq=SEQ)
            o = _attn_call(pb, o, lambda_qk[e], sub, lam_init, L=DEC_SEQ, nb=DEC_BATCH,
                           blk0=blk_s, tq=256, ctx=(cache_k, cache_v, e), rope=rope)
            x = _outproj_call(x, y, o, mod, l, e, w_out_bf)
        else:
            i_odd = l // 2
            b4 = b_four[i_odd].reshape(1, D_MODEL)
            xn = _four_call(x, None, mod, l, i_odd, gain_mix, w_four_bf, b4, L=SEQ, nb=BATCH, blk0=0)
            x = _four_call(x, xn, mod, l, i_odd, gain_mix, w_four_bf, b4, L=DEC_SEQ, nb=DEC_BATCH,
                           blk0=blk_s)
        x = _ffn_call(x, mod, l, norm_ffn[l].reshape(1, D_MODEL), w_ffn_in_bf, w_ffn_out_bf)

    y_p, y_s = _final_call(x, norm_final.reshape(1, D_MODEL))
    return (y_p.reshape(BATCH, SEQ, D_MODEL), y_s.reshape(DEC_BATCH, DEC_SEQ, D_MODEL),
            kv_new[0].reshape(BATCH, N_EVEN, SEQ, ATT_HEADS, 2, ATT_HD),
            kv_new[1].reshape(BATCH, N_EVEN, SEQ, ATT_HEADS, ATT_VD),
            states[0].reshape(BATCH, N_EVEN, SSD_HEADS, SSD_HD, D_STATE),
            states[1].reshape(BATCH, N_EVEN, SSD_HEADS, SSD_HD, D_STATE))
```

```python
import functools
import math

import numpy as np
import jax
import jax.numpy as jnp
from jax import lax
from jax.experimental import pallas as pl
from jax.experimental.pallas import tpu as pltpu

D_MODEL = 1024
BATCH = 16
SEQ = 256
DEPTH = 4
N_EVEN = 2
DEC_BATCH = 4
DEC_SEQ = 1024
PAST_LEN = 512
GRID_W = 64
SSD_HEADS = 16
SSD_HD = 64
SSD_INNER = 1024
SSD_GROUPS = 4
D_STATE = 128
CONV_W = 5
CONV_CH = 2048
CHUNK = 128
ATT_HEADS = 8
ATT_HD = 64
ATT_VD = 128
ATT_INNER = 1024
ROPE_THETA = 10000.0
ROPE_PAIRS = 16
FOUR_GROUPS = 4
FOUR_GC = D_MODEL // FOUR_GROUPS
FFN_HIDDEN = 2816
EPS = 1e-6

P_TOK = BATCH * SEQ
S_TOK = DEC_BATCH * DEC_SEQ
N_TOK = P_TOK + S_TOK
N_MOD = 8
PA_W = CONV_CH + SSD_INNER
PB_W = 3 * ATT_INNER
DT_W = 128
NH2 = SSD_HEADS // 2
LANE = 128
SUBLANE = 8
N_SLAB = CONV_CH // LANE
QBLK = 32
MXU_N = 256
VMEM_LIMIT = 56 * 1024 * 1024

F32 = jnp.float32
BF16 = jnp.bfloat16


def _params(*sem):
    return pltpu.CompilerParams(dimension_semantics=sem, vmem_limit_bytes=VMEM_LIMIT)


def _dot(a, b):
    return jnp.dot(a, b, preferred_element_type=F32)


def _dot_nt(a, b):
    return lax.dot_general(a, b, (((1,), (1,)), ((), ())), preferred_element_type=F32)


def _silu(x):
    return x * jax.nn.sigmoid(x)


def _rms(x):
    return x * lax.rsqrt(jnp.mean(x * x, axis=-1, keepdims=True) + EPS)


def _mod_row(tm):
    return lambda i: jnp.maximum((i * tm) // DEC_SEQ - (P_TOK // DEC_SEQ - 1), 0)


def _drop_args(kern, start, n):
    def wrapped(*refs):
        return kern(*(refs[:start] + refs[start + n:]))
    return wrapped


def _tok_specs(parts, tm):
    width = parts[0].shape[-1]
    if len(parts) == 1:
        return [pl.BlockSpec((tm, width), lambda i: (i, 0))]
    npt = P_TOK // tm
    return [pl.BlockSpec((tm, width), lambda i: (jnp.minimum(i, npt - 1), 0)),
            pl.BlockSpec((tm, width), lambda i: (jnp.maximum(i - npt, 0), 0))]


def _tok_load(refs, tm):
    if len(refs) == 1:
        return refs[0][...]
    return jnp.where(pl.program_id(0) < P_TOK // tm, refs[0][...], refs[1][...])


def _ada_kernel(c_ref, w_ref, b_ref, o_ref):
    s = _silu(c_ref[...]).astype(BF16)
    o_ref[...] = _dot(s, w_ref[...].astype(BF16)) + b_ref[...]


def _ada_call(cvec, w_ada, b_ada):
    tn = 1024
    return pl.pallas_call(
        _ada_kernel,
        grid=(DEPTH, 6 * D_MODEL // tn),
        in_specs=[
            pl.BlockSpec((N_MOD, D_MODEL), lambda l, j: (0, 0)),
            pl.BlockSpec((None, D_MODEL, tn), lambda l, j: (l, 0, j)),
            pl.BlockSpec((None, 1, tn), lambda l, j: (l, 0, j)),
        ],
        out_specs=pl.BlockSpec((None, N_MOD, tn), lambda l, j: (l, 0, j)),
        out_shape=jax.ShapeDtypeStruct((DEPTH, N_MOD, 6 * D_MODEL), F32),
        compiler_params=_params("parallel", "parallel"),
        name="adaln",
    )(cvec, w_ada, b_ada.reshape(DEPTH, 1, 6 * D_MODEL))


def _inproj_kernel(*refs, n_x, tm):
    x_refs = refs[:n_x]
    mod_ref, g_ref, w_ref, xbc_ref, z_ref, pb_ref, dt_ref, kn_ref, vn_ref = refs[n_x:]
    x = _tok_load(x_refs, tm)
    h = _rms(x) * g_ref[...] * (1.0 + mod_ref[1:2, :]) + mod_ref[0:1, :]
    hb = h.astype(BF16)
    tn = ATT_INNER
    for j in range(CONV_CH // tn):
        res = _dot(hb, w_ref[:, j * tn:(j + 1) * tn])
        for s in range(tn // LANE):
            xbc_ref[j * (tn // LANE) + s] = res[:, s * LANE:(s + 1) * LANE]
    z_ref[...] = _dot(hb, w_ref[:, CONV_CH:PA_W])
    q = _dot(hb, w_ref[:, PA_W:PA_W + tn])
    pb_ref[:, 0:tn] = q.astype(BF16)
    k = _dot(hb, w_ref[:, PA_W + tn:PA_W + 2 * tn])
    pb_ref[:, tn:2 * tn] = k.astype(BF16)
    v = _dot(hb, w_ref[:, PA_W + 2 * tn:PA_W + 3 * tn])
    pb_ref[:, 2 * tn:3 * tn] = v.astype(BF16)
    dt_ref[...] = _dot(hb, w_ref[:, PA_W + PB_W:PA_W + PB_W + DT_W])

    @pl.when(pl.program_id(0) < P_TOK // tm)
    def _():
        kn_ref[...] = k.reshape(tm // SEQ, SEQ, ATT_INNER)
        vn_ref[...] = v.reshape(tm // SEQ, SEQ, ATT_INNER)


def _inproj_call(x, mod, l, e, gain, w_in, kv_prev):
    tm = 512
    row = _mod_row(tm)
    npt = P_TOK // tm
    n_x = len(x)
    args = [*x, mod, gain, w_in, *kv_prev]
    in_specs = _tok_specs(x, tm) + [
        pl.BlockSpec((None, None, 6, D_MODEL), lambda i: (l, row(i), 0, 0)),
        pl.BlockSpec((1, D_MODEL), lambda i: (0, 0)),
        pl.BlockSpec((None, D_MODEL, PA_W + PB_W + DT_W), lambda i: (e, 0, 0),
                     pipeline_mode=pl.Buffered(1)),
        pl.BlockSpec(memory_space=pl.ANY), pl.BlockSpec(memory_space=pl.ANY),
    ]
    kern = functools.partial(_inproj_kernel, n_x=n_x, tm=tm)
    kern = _drop_args(kern, n_x + 3, 2)
    kv_spec = pl.BlockSpec((tm // SEQ, None, SEQ, ATT_INNER),
                           lambda i: (jnp.minimum(i, npt - 1), e, 0, 0))
    kv_shape = jax.ShapeDtypeStruct((BATCH, N_EVEN, SEQ, ATT_INNER), F32)
    return pl.pallas_call(
        kern,
        grid=(N_TOK // tm,),
        in_specs=in_specs,
        out_specs=[
            pl.BlockSpec((N_SLAB, tm, LANE), lambda i: (0, i, 0)),
            pl.BlockSpec((tm, SSD_INNER), lambda i: (i, 0)),
            pl.BlockSpec((tm, PB_W), lambda i: (i, 0)),
            pl.BlockSpec((tm, DT_W), lambda i: (i, 0)),
            kv_spec, kv_spec,
        ],
        out_shape=[
            jax.ShapeDtypeStruct((N_SLAB, N_TOK, LANE), F32),
            jax.ShapeDtypeStruct((N_TOK, SSD_INNER), F32),
            jax.ShapeDtypeStruct((N_TOK, PB_W), BF16),
            jax.ShapeDtypeStruct((N_TOK, DT_W), F32),
            kv_shape, kv_shape,
        ],
        input_output_aliases={n_x + 3: 4, n_x + 4: 5},
        compiler_params=_params("arbitrary"),
        name="inproj",
    )(*args)


def _ssd_kernel(*refs, L, has_h0, emit_state):
    it = iter(refs)
    xbc_ref, z_ref, dt_ref = next(it), next(it), next(it)
    convw_ref, convb_ref, dtb_ref, alog_ref, dskip_ref, norm_ref = (next(it) for _ in range(6))
    sel_refs = (next(it), next(it))
    if has_h0:
        h0_refs = (next(it), next(it))
    y_ref = next(it)
    if emit_state:
        hout_refs = (next(it), next(it))
    xact, xdt_f, xdt_b, bt, cact, tmp, yacc, dtv, state_f, state_b = (next(it) for _ in range(10))

    nc = L // CHUNK
    nq = L // SUBLANE
    half_w = CONV_W // 2

    rq = lax.broadcasted_iota(jnp.int32, (QBLK, LANE), 0)

    def phase(s, pe, q0):
        r0 = SUBLANE * q0 + pe
        if r0 < 0:
            a = xbc_ref[s, pl.ds(r0 + SUBLANE, QBLK, stride=SUBLANE), :]
            return jnp.where(rq == 0, 0.0, pltpu.roll(a, 1, 0))
        if r0 + SUBLANE * (QBLK - 1) >= L:
            a = xbc_ref[s, pl.ds(r0 - SUBLANE, QBLK, stride=SUBLANE), :]
            return jnp.where(rq == QBLK - 1, 0.0, pltpu.roll(a, QBLK - 1, 0))
        return xbc_ref[s, pl.ds(r0, QBLK, stride=SUBLANE), :]

    def conv_slab(s, dst, d_idx):
        ls = slice(s * LANE, (s + 1) * LANE)
        for q0 in range(0, nq, QBLK):
            ext = {pe: phase(s, pe, q0) for pe in range(-half_w, SUBLANE + half_w)}
            for p in range(SUBLANE):
                acc = jnp.broadcast_to(convb_ref[:, ls], (QBLK, LANE))
                for k in range(CONV_W):
                    acc = acc + convw_ref[k:k + 1, ls] * ext[p + k - half_w]
                dst[d_idx, pl.ds(SUBLANE * q0 + p, QBLK, stride=SUBLANE), :] = _silu(acc)

    for m in range(NH2):
        conv_slab(m, xact, m)
    for g in range(SSD_GROUPS):
        conv_slab(NH2 + g, tmp, g)
    for c in range(nc):
        for g in range(SSD_GROUPS):
            bt[c, g * D_STATE:(g + 1) * D_STATE, :] = (
                tmp[g, c * CHUNK:(c + 1) * CHUNK, :].T.astype(BF16))
    for g in range(SSD_GROUPS):
        conv_slab(NH2 + SSD_GROUPS + g, tmp, g)
    for c in range(nc):
        rows = slice(c * CHUNK, (c + 1) * CHUNK)
        for g in range(SSD_GROUPS):
            cact[g, rows, :] = tmp[g, rows, :].astype(BF16)

    dr = dt_ref[...] + dtb_ref[...]
    dtv[...] = jnp.maximum(dr, 0.0) + jnp.log1p(jnp.exp(-jnp.abs(dr)))
    a_row = -jnp.exp(alog_ref[...]) * math.log2(math.e)

    for c in range(nc):
        rows = slice(c * CHUNK, (c + 1) * CHUNK)
        d = dtv[rows, :]
        d_hi = d.astype(BF16)
        d_lo = (d - d_hi.astype(F32)).astype(BF16)
        d2 = jnp.concatenate([d_hi, d_lo], axis=1)
        spread_f = _dot(d2, sel_refs[0][...])
        spread_b = _dot(d2, sel_refs[1][...])
        for m in range(NH2):
            lanes = slice(m * LANE, (m + 1) * LANE)
            xm = xact[m, rows, :]
            xdt_f[m, rows, :] = (xm * spread_f[:, lanes]).astype(BF16)
            xdt_b[m, rows, :] = (xm * spread_b[:, lanes]).astype(BF16)

    ri = lax.broadcasted_iota(jnp.int32, (CHUNK, CHUNK), 0)
    ci = lax.broadcasted_iota(jnp.int32, (CHUNK, CHUNK), 1)
    low_half = ci < SSD_HD
    low_row = low_half[0:1, :]

    for d_idx, st_ref in enumerate((state_f, state_b)):
        for m in range(NH2):
            if has_h0:
                st_ref[m] = h0_refs[d_idx][m].T
            else:
                st_ref[m] = jnp.zeros((D_STATE, LANE), F32)

    def chunk_step(c, fwd, add):
        mask = (ri >= ci) if fwd else (ri <= ci)
        tri = mask.astype(F32)
        col0 = 0 if fwd else SSD_HEADS
        last = CHUNK - 1 if fwd else 0
        xdt = xdt_f if fwd else xdt_b
        state = state_f if fwd else state_b
        rows = pl.ds(pl.multiple_of(c * CHUNK, CHUNK), CHUNK)
        dtc = dtv[rows, :]
        acs = jnp.dot(tri, dtc * a_row, preferred_element_type=F32,
                      precision=lax.Precision.HIGHEST)
        acs_t = acs.T
        acs_last = acs[last:last + 1, :]
        for g in range(SSD_GROUPS):
            btg = bt[c, g * D_STATE:(g + 1) * D_STATE, :]
            cg = cact[g, rows, :]
            cbm = jnp.where(mask, _dot(cg, btg), 0.0)
            for pr in range(2):
                m = g * 2 + pr
                ha, hb = col0 + 2 * m, col0 + 2 * m + 1
                lanes = slice(m * LANE, (m + 1) * LANE)
                xd2 = xdt[m, rows, :]
                acol_a = jnp.broadcast_to(acs[:, ha:ha + 1], (CHUNK, CHUNK))
                acol_b = jnp.broadcast_to(acs[:, hb:hb + 1], (CHUNK, CHUNK))
                w_a = cbm * jnp.exp2(jnp.minimum(acol_a - acs_t[ha:ha + 1, :], 0.0))
                w_b = cbm * jnp.exp2(jnp.minimum(acol_b - acs_t[hb:hb + 1, :], 0.0))
                w2 = jnp.concatenate([w_a.astype(BF16), w_b.astype(BF16)], axis=1)
                zero = jnp.zeros_like(xd2)
                x_blk = jnp.concatenate([jnp.where(low_half, xd2, zero),
                                         jnp.where(low_half, zero, xd2)], axis=0)
                y_in = _dot(w2, x_blk)
                acol2 = jnp.where(low_half, acol_a, acol_b)
                last2 = jnp.where(low_row, acs_last[:, ha:ha + 1], acs_last[:, hb:hb + 1])
                h_prev = state[m]
                y_out = _dot(cg, h_prev.astype(BF16)) * jnp.exp2(acol2)
                to_end = jnp.exp2(last2 - acol2)
                st = _dot(btg, (to_end * xd2.astype(F32)).astype(BF16))
                state[m] = jnp.exp2(last2) * h_prev + st
                y2 = y_in + y_out
                if fwd:
                    y2 = y2 + dskip_ref[:, lanes] * xact[m, rows, :]
                if add:
                    y2 = y2 + yacc[rows, lanes]
                yacc[rows, lanes] = y2

    def finish(c):
        rows = pl.ds(pl.multiple_of(c * CHUNK, CHUNK), CHUNK)
        yt = yacc[rows, :] * _silu(z_ref[rows, :])
        y_ref[rows, :] = (_rms(yt) * norm_ref[...]).astype(y_ref.dtype)

    def first_half(s, carry):
        chunk_step(s, True, False)
        chunk_step(nc - 1 - s, False, False)
        return carry

    def second_half(s, carry):
        chunk_step(s, True, True)
        chunk_step(nc - 1 - s, False, True)
        finish(s)
        finish(nc - 1 - s)
        return carry

    lax.fori_loop(0, nc // 2, first_half, 0)
    lax.fori_loop(nc // 2, nc, second_half, 0)

    if emit_state:
        for d_idx, st_ref in enumerate((state_f, state_b)):
            for m in range(NH2):
                hout_refs[d_idx][m] = st_ref[m].T


def _head_spread_tables():
    ch_head = np.arange(SSD_INNER) // SSD_HD
    out = []
    for d in range(2):
        k = np.arange(2 * DT_W) % DT_W
        out.append(jnp.asarray(k[:, None] == (d * SSD_HEADS + ch_head)[None, :], BF16))
    return out


def _ssd_call(xbc, z, dtarr, states_prev, weights, e, *, L, nb, blk0, h0=None):
    emit_state = states_prev is not None
    nblk = N_TOK // L
    nc = L // CHUNK
    vec = lambda w: pl.BlockSpec((1, w), lambda b: (0, 0))
    in_specs = [
        pl.BlockSpec((N_SLAB, L, LANE), lambda b: (0, blk0 + b, 0)),
        pl.BlockSpec((None, L, SSD_INNER), lambda b: (blk0 + b, 0, 0)),
        pl.BlockSpec((None, L, DT_W), lambda b: (blk0 + b, 0, 0)),
        pl.BlockSpec((8, CONV_CH), lambda b: (0, 0)),
        vec(CONV_CH), vec(DT_W), vec(DT_W), vec(SSD_INNER), vec(SSD_INNER),
        pl.BlockSpec((2 * DT_W, SSD_INNER), lambda b: (0, 0)),
        pl.BlockSpec((2 * DT_W, SSD_INNER), lambda b: (0, 0)),
    ]
    args = [xbc, z.reshape(nblk, L, SSD_INNER), dtarr.reshape(nblk, L, DT_W), *weights,
            *_head_spread_tables()]
    state_spec = pl.BlockSpec((None, None, NH2, LANE, D_STATE), lambda b: (b, e, 0, 0, 0))
    if h0 is not None:
        for h in h0:
            args.append(h.reshape(DEC_BATCH, N_EVEN, NH2, LANE, D_STATE))
            in_specs.append(state_spec)
    kern = functools.partial(_ssd_kernel, L=L, has_h0=h0 is not None, emit_state=emit_state)
    out_specs = [pl.BlockSpec((None, L, SSD_INNER), lambda b: (b, 0, 0))]
    out_shape = [jax.ShapeDtypeStruct((nb, L, SSD_INNER), BF16)]
    io_alias = {}
    if emit_state:
        io_alias = {len(args): 1, len(args) + 1: 2}
        kern = _drop_args(kern, len(args), 2)
        args += list(states_prev)
        in_specs += [pl.BlockSpec(memory_space=pl.ANY)] * 2
        out_specs += [state_spec, state_spec]
        out_shape += [jax.ShapeDtypeStruct((nb, N_EVEN, NH2, LANE, D_STATE), F32)] * 2
    return pl.pallas_call(
        kern,
        grid=(nb,),
        in_specs=in_specs,
        out_specs=out_specs,
        out_shape=out_shape,
        scratch_shapes=[
            pltpu.VMEM((NH2, L, LANE), F32),
            pltpu.VMEM((NH2, L, LANE), BF16),
            pltpu.VMEM((NH2, L, LANE), BF16),
            pltpu.VMEM((nc, SSD_GROUPS * D_STATE, CHUNK), BF16),
            pltpu.VMEM((SSD_GROUPS, L, D_STATE), BF16),
            pltpu.VMEM((SSD_GROUPS, L, D_STATE), F32),
            pltpu.VMEM((L, SSD_INNER), F32),
            pltpu.VMEM((L, DT_W), F32),
            pltpu.VMEM((NH2, D_STATE, LANE), F32),
            pltpu.VMEM((NH2, D_STATE, LANE), F32),
        ],
        input_output_aliases=io_alias,
        compiler_params=_params("parallel"),
        name=f"ssd_L{L}",
    )(*args)


def _rope(x, cos, sin_a, sin_b):
    return (x * cos + pltpu.roll(x, LANE - ROPE_PAIRS, 1) * sin_a
            + pltpu.roll(x, ROPE_PAIRS, 1) * sin_b)


def _attn_kernel(*refs, L, n_ctx, lam_init):
    it = iter(refs)
    q_ref, k_ref, v_ref = next(it), next(it), next(it)
    if n_ctx:
        ck_ref, cv_ref = next(it), next(it)
        cq, saq, sbq, ck, sak, sbk = (next(it) for _ in range(6))
    lam_ref, sub_ref = next(it), next(it)
    o_ref = next(it)
    kall, vext = next(it), next(it)
    lk = n_ctx + L
    tq = q_ref.shape[0]

    @pl.when(pl.program_id(1) == 0)
    def _():
        ones = jnp.ones((lk, MXU_N - ATT_VD), BF16)
        for h in range(ATT_HEADS):
            hs = slice(h * ATT_VD, (h + 1) * ATT_VD)
            if n_ctx:
                kall[h, 0:n_ctx, :] = ck_ref[:, hs].astype(BF16)
                vext[h, 0:n_ctx, 0:ATT_VD] = cv_ref[:, hs].astype(BF16)
                kh = _rope(k_ref[:, hs].astype(F32), ck[...], sak[...], sbk[...])
                kall[h, n_ctx:lk, :] = kh.astype(BF16)
            else:
                kall[h] = k_ref[:, hs]
            vext[h, n_ctx:lk, 0:ATT_VD] = v_ref[:, hs]
            vext[h, :, ATT_VD:MXU_N] = ones

    lane = lax.broadcasted_iota(jnp.int32, (1, ATT_VD), 1)
    scale = ATT_HD ** -0.5
    m1 = jnp.where(lane < ATT_HD, scale, 0.0)
    m2 = jnp.where(lane < ATT_HD, 0.0, scale)
    lq = lam_ref[...]
    lam = (jnp.exp(jnp.sum(lq[0:1, :] * lq[1:2, :], axis=-1, keepdims=True))
           - jnp.exp(jnp.sum(lq[2:3, :] * lq[3:4, :], axis=-1, keepdims=True)) + lam_init)

    for h in range(ATT_HEADS):
        hs = slice(h * ATT_VD, (h + 1) * ATT_VD)
        q = q_ref[:, hs].astype(F32)
        if n_ctx:
            q = _rope(q, cq[...], saq[...], sbq[...])
        q12 = jnp.concatenate([(q * m1).astype(BF16), (q * m2).astype(BF16)], axis=0)
        s = _dot_nt(q12, kall[h])
        p = jnp.exp(s - jnp.max(s, axis=-1, keepdims=True)).astype(BF16)
        pv = _dot(p, vext[h])
        on = pv[:, 0:ATT_VD] / pv[:, ATT_VD:MXU_N]
        o = on[0:tq] - lam * on[tq:2 * tq]
        o_ref[:, hs] = (_rms(o) * sub_ref[...] * (1.0 - lam_init)).astype(o_ref.dtype)


def _attn_call(pb, lam_qk, subln, lam_init, *, L, nb, blk0, tq, ctx=None, rope=None):
    nblk = N_TOK // L
    nq = L // tq
    pb3 = pb.reshape(nblk, L, PB_W)
    in_specs = [
        pl.BlockSpec((None, tq, ATT_INNER), lambda b, i: (blk0 + b, i, 0)),
        pl.BlockSpec((None, L, ATT_INNER), lambda b, i: (blk0 + b, 0, 1)),
        pl.BlockSpec((None, L, ATT_INNER), lambda b, i: (blk0 + b, 0, 2)),
    ]
    args = [pb3, pb3, pb3]
    n_ctx = 0
    if ctx is not None:
        cache_k, cache_v, e = ctx
        n_ctx = cache_k.shape[2]
        for cch in (cache_k, cache_v):
            args.append(cch.reshape(DEC_BATCH, N_EVEN, n_ctx, ATT_INNER))
            in_specs.append(pl.BlockSpec((None, None, n_ctx, ATT_INNER), lambda b, i: (b, e, 0, 0)))
        for t in rope:
            args.append(t)
            in_specs.append(pl.BlockSpec((tq, ATT_VD), lambda b, i: (i, 0)))
        for t in rope:
            args.append(t)
            in_specs.append(pl.BlockSpec((L, ATT_VD), lambda b, i: (0, 0)))
    args += [lam_qk, subln]
    in_specs += [pl.BlockSpec((4, ATT_HD), lambda b, i: (0, 0)),
                 pl.BlockSpec((1, ATT_VD), lambda b, i: (0, 0))]
    return pl.pallas_call(
        functools.partial(_attn_kernel, L=L, n_ctx=n_ctx, lam_init=lam_init),
        grid=(nb, nq),
        in_specs=in_specs,
        out_specs=pl.BlockSpec((None, tq, ATT_INNER), lambda b, i: (b, i, 0)),
        out_shape=jax.ShapeDtypeStruct((nb, L, ATT_INNER), BF16),
        scratch_shapes=[pltpu.VMEM((ATT_HEADS, n_ctx + L, ATT_VD), BF16),
                        pltpu.VMEM((ATT_HEADS, n_ctx + L, MXU_N), BF16)],
        compiler_params=_params("parallel", "arbitrary"),
        name=f"attn_L{L}",
    )(*args)


def _outproj_kernel(*refs, n_x, tm):
    x_refs = refs[:n_x]
    yp_ref, ys_ref, op_ref, os_ref, mod_ref, w_ref, out_ref = refs[n_x:]
    y = _tok_load((yp_ref, ys_ref), tm)
    o = _tok_load((op_ref, os_ref), tm)
    acc = _dot(y, w_ref[0:SSD_INNER, :]) + _dot(o, w_ref[SSD_INNER:, :])
    out_ref[...] = _tok_load(x_refs, tm) + mod_ref[2:3, :] * acc


def _outproj_call(x, y, o, mod, l, e, w):
    tm = 512
    row = _mod_row(tm)
    return pl.pallas_call(
        functools.partial(_outproj_kernel, n_x=len(x), tm=tm),
        grid=(N_TOK // tm,),
        in_specs=_tok_specs(x, tm) + _tok_specs(y, tm) + _tok_specs(o, tm) + [
            pl.BlockSpec((None, None, 6, D_MODEL), lambda i: (l, row(i), 0, 0)),
            pl.BlockSpec((None, SSD_INNER + ATT_INNER, D_MODEL), lambda i: (e, 0, 0))],
        out_specs=pl.BlockSpec((tm, D_MODEL), lambda i: (i, 0)),
        out_shape=jax.ShapeDtypeStruct((N_TOK, D_MODEL), F32),
        compiler_params=_params("arbitrary"),
        name="outproj",
    )(*x, *y, *o, mod, w)


def _ffn_kernel(*refs, n_x, tm):
    x_refs = refs[:n_x]
    mod_ref, g_ref, win_ref, wout_ref, out_ref, h_ref, a_ref = refs[n_x:]
    x = _tok_load(x_refs, tm)
    h = _rms(x) * g_ref[...] * (1.0 + mod_ref[4:5, :]) + mod_ref[3:4, :]
    h_ref[...] = h.astype(BF16)
    for j in range(FFN_HIDDEN // MXU_N):
        cs = slice(j * MXU_N, (j + 1) * MXU_N)
        us = slice(FFN_HIDDEN + j * MXU_N, FFN_HIDDEN + (j + 1) * MXU_N)
        hb = h_ref[...]
        a_ref[:, cs] = (_silu(_dot(hb, win_ref[:, cs])) * _dot(hb, win_ref[:, us])).astype(BF16)
    out_ref[...] = _tok_load(x_refs, tm) + mod_ref[5:6, :] * _dot(a_ref[...], wout_ref[...])


def _ffn_call(x, mod, l, gain, w_in, w_out):
    tm = 1024
    row = _mod_row(tm)
    once = pl.Buffered(1)
    return pl.pallas_call(
        functools.partial(_ffn_kernel, n_x=len(x), tm=tm),
        grid=(N_TOK // tm,),
        in_specs=_tok_specs(x, tm) + [
            pl.BlockSpec((None, None, 6, D_MODEL), lambda i: (l, row(i), 0, 0)),
            pl.BlockSpec((1, D_MODEL), lambda i: (0, 0)),
            pl.BlockSpec((None, D_MODEL, 2 * FFN_HIDDEN), lambda i: (l, 0, 0), pipeline_mode=once),
            pl.BlockSpec((None, FFN_HIDDEN, D_MODEL), lambda i: (l, 0, 0), pipeline_mode=once),
        ],
        out_specs=pl.BlockSpec((tm, D_MODEL), lambda i: (i, 0)),
        out_shape=jax.ShapeDtypeStruct((N_TOK, D_MODEL), F32),
        scratch_shapes=[pltpu.VMEM((tm, D_MODEL), BF16), pltpu.VMEM((tm, FFN_HIDDEN), BF16)],
        compiler_params=_params("arbitrary"),
        name="ffn",
    )(*x, mod, gain, w_in, w_out)


def _four_kernel(x_ref, mod_ref, g_ref, cc_ref, sc_ref, cs_ref, w_ref, b_ref, out_ref, stk_ref, *, L):
    x = x_ref[...]
    h = (_rms(x) * g_ref[...] * (1.0 + mod_ref[1:2, :]) + mod_ref[0:1, :]).astype(BF16)
    for g in range(FOUR_GROUPS):
        ls = slice(g * FOUR_GC, (g + 1) * FOUR_GC)
        hg = h[:, ls]
        stk_ref[0:L, ls] = _dot(hg, cc_ref[...]).astype(BF16)
        stk_ref[L:2 * L, ls] = _dot(hg, sc_ref[...]).astype(BF16)
    f = _dot(cs_ref[...], stk_ref[...]).astype(BF16)
    out_ref[...] = x + mod_ref[2:3, :] * (_dot(f, w_ref[...]) + b_ref[...])


def _dft_tables(L):
    n = FOUR_GC
    ang_c = 2.0 * np.pi * ((np.arange(n)[:, None] * np.arange(n)[None, :]) % n) / n
    ang_l = 2.0 * np.pi * ((np.arange(L)[:, None] * np.arange(L)[None, :]) % L) / L
    cc = np.cos(ang_c) / math.sqrt(n)
    sc = np.sin(ang_c) / math.sqrt(n)
    cs = np.concatenate([np.cos(ang_l), -np.sin(ang_l)], axis=1) / math.sqrt(L)
    return (jnp.asarray(cc, F32).astype(BF16), jnp.asarray(sc, F32).astype(BF16),
            jnp.asarray(cs, F32).astype(BF16))


def _four_call(x, mod, l, i_odd, gain, w, b, *, L, nb, blk0):
    nblk = N_TOK // L
    cc, sc, cs = _dft_tables(L)
    row = _mod_row(L)
    out = pl.pallas_call(
        functools.partial(_four_kernel, L=L),
        grid=(nb,),
        in_specs=[
            pl.BlockSpec((None, L, D_MODEL), lambda i: (blk0 + i, 0, 0)),
            pl.BlockSpec((None, None, 6, D_MODEL), lambda i: (l, row(blk0 + i), 0, 0)),
            pl.BlockSpec((1, D_MODEL), lambda i: (0, 0)),
            pl.BlockSpec((FOUR_GC, FOUR_GC), lambda i: (0, 0)),
            pl.BlockSpec((FOUR_GC, FOUR_GC), lambda i: (0, 0)),
            pl.BlockSpec((L, 2 * L), lambda i: (0, 0)),
            pl.BlockSpec((None, D_MODEL, D_MODEL), lambda i: (i_odd, 0, 0)),
            pl.BlockSpec((1, D_MODEL), lambda i: (0, 0)),
        ],
        out_specs=pl.BlockSpec((None, L, D_MODEL), lambda i: (i, 0, 0)),
        out_shape=jax.ShapeDtypeStruct((nb, L, D_MODEL), F32),
        scratch_shapes=[pltpu.VMEM((2 * L, D_MODEL), BF16)],
        compiler_params=_params("parallel"),
        name=f"fourier_L{L}",
    )(x.reshape(nblk, L, D_MODEL), mod, gain, cc, sc, cs, w, b)
    return out.reshape(nb * L, D_MODEL)


def _final_kernel(x_ref, g_ref, op_ref, os_ref, *, n_prompt_tiles):
    y = _rms(x_ref[...]) * g_ref[...]

    @pl.when(pl.program_id(0) < n_prompt_tiles)
    def _():
        op_ref[...] = y

    @pl.when(pl.program_id(0) >= n_prompt_tiles)
    def _():
        os_ref[...] = y


def _final_call(x, gain):
    tm = 1024
    npt = P_TOK // tm
    return pl.pallas_call(
        functools.partial(_final_kernel, n_prompt_tiles=npt),
        grid=(N_TOK // tm,),
        in_specs=[pl.BlockSpec((tm, D_MODEL), lambda i: (i, 0)),
                  pl.BlockSpec((1, D_MODEL), lambda i: (0, 0))],
        out_specs=[pl.BlockSpec((tm, D_MODEL), lambda i: (jnp.minimum(i, npt - 1), 0)),
                   pl.BlockSpec((tm, D_MODEL), lambda i: (jnp.maximum(i - npt, 0), 0))],
        out_shape=[jax.ShapeDtypeStruct((P_TOK, D_MODEL), F32),
                   jax.ShapeDtypeStruct((S_TOK, D_MODEL), F32)],
        compiler_params=_params("arbitrary"),
        name="final_norm",
    )(x, gain)


def _rope_lane_tables(L):
    t = jnp.arange(L)
    row = (t // GRID_W).astype(F32)
    col = (t % GRID_W).astype(F32)
    freq = ROPE_THETA ** (-jnp.arange(ROPE_PAIRS, dtype=F32) / ROPE_PAIRS)
    ar = row[:, None] * freq[None]
    ac = col[:, None] * freq[None]
    z = jnp.zeros_like(ar)
    cos32 = lambda a: jnp.concatenate([jnp.cos(a), jnp.cos(a)], axis=-1)
    sa32 = lambda a: jnp.concatenate([-jnp.sin(a), z], axis=-1)
    sb32 = lambda a: jnp.concatenate([z, jnp.sin(a)], axis=-1)
    tile = lambda f: jnp.concatenate([f(ar), f(ac), f(ar), f(ac)], axis=-1)
    return tile(cos32), tile(sa32), tile(sb32)


def _pad_lanes(v, width):
    v = v.reshape(1, -1).astype(F32)
    return jnp.pad(v, ((0, 0), (0, width - v.shape[1])))


def kernel(x_prompt, x_sample, cache_k, cache_v, state_ssd_fwd, state_ssd_bwd, c, c_ctx, w_ada, b_ada, norm_mix, norm_ffn, w_in_ab, conv_w, conv_b, dt_bias, a_log, d_skip, ssd_norm, lambda_qk, subln, w_out_ab, w_four, b_four, w_ffn_in, w_ffn_out, norm_final):
    x = (x_prompt.reshape(P_TOK, D_MODEL), x_sample.reshape(S_TOK, D_MODEL))
    cvec = jnp.concatenate([c_ctx[None, :], c, jnp.zeros((N_MOD - 1 - DEC_BATCH, D_MODEL), F32)], axis=0)
    mod = _ada_call(cvec, w_ada, b_ada).reshape(DEPTH, N_MOD, 6, D_MODEL)
    rope = _rope_lane_tables(DEC_SEQ)

    o1, o2, o3 = SSD_INNER, SSD_INNER + CONV_CH, SSD_INNER + CONV_CH + 2 * SSD_HEADS
    w_in_bf = jnp.concatenate(
        [w_in_ab[:, :, o1:o2], w_in_ab[:, :, :o1], w_in_ab[:, :, o3:], w_in_ab[:, :, o2:o3],
         jnp.zeros((N_EVEN, D_MODEL, DT_W - 2 * SSD_HEADS), F32)], axis=2).astype(BF16)
    w_out_bf = w_out_ab.astype(BF16)
    w_four_bf = w_four.astype(BF16)
    w_ffn_in_bf = w_ffn_in.astype(BF16)
    w_ffn_out_bf = w_ffn_out.astype(BF16)

    kv_new = (jnp.zeros((BATCH, N_EVEN, SEQ, ATT_INNER), F32),) * 2
    states = (jnp.zeros((BATCH, N_EVEN, NH2, LANE, D_STATE), F32),) * 2
    blk_s = P_TOK // DEC_SEQ
    for l in range(DEPTH):
        gain_mix = norm_mix[l].reshape(1, D_MODEL)
        if l % 2 == 0:
            e = l // 2
            lam_init = 0.8 - 0.6 * math.exp(-0.3 * l)
            xbc, z, pb, dtarr, k_new, v_new = _inproj_call(x, mod, l, e, gain_mix, w_in_bf, kv_new)
            kv_new = (k_new, v_new)

            ssd_w = (jnp.pad(conv_w[e], ((0, 8 - CONV_W), (0, 0))), conv_b[e].reshape(1, CONV_CH),
                     _pad_lanes(dt_bias[e], DT_W), _pad_lanes(a_log[e], DT_W),
                     jnp.repeat(d_skip[e], SSD_HD).reshape(1, SSD_INNER),
                     ssd_norm[e].reshape(1, SSD_INNER))
            y_p, hf, hb = _ssd_call(xbc, z, dtarr, states, ssd_w, e, L=SEQ, nb=BATCH, blk0=0)
            states = (hf, hb)
            (y_s,) = _ssd_call(xbc, z, dtarr, None, ssd_w, e, L=DEC_SEQ, nb=DEC_BATCH, blk0=blk_s,
                               h0=(state_ssd_fwd, state_ssd_bwd))
            sub = subln[e].reshape(1, ATT_VD)
            o_p = _attn_call(pb, lambda_qk[e], sub, lam_init, L=SEQ, nb=BATCH, blk0=0, tq=SEQ)
            o_s = _attn_call(pb, lambda_qk[e], sub, lam_init, L=DEC_SEQ, nb=DEC_BATCH,
                             blk0=blk_s, tq=256, ctx=(cache_k, cache_v, e), rope=rope)
            halves = lambda a, b: (a.reshape(P_TOK, -1), b.reshape(S_TOK, -1))
            x = (_outproj_call(x, halves(y_p, y_s), halves(o_p, o_s), mod, l, e, w_out_bf),)
        else:
            i_odd = l // 2
            b4 = b_four[i_odd].reshape(1, D_MODEL)
            x = (_four_call(x[0], mod, l, i_odd, gain_mix, w_four_bf, b4, L=SEQ, nb=BATCH, blk0=0),
                 _four_call(x[0], mod, l, i_odd, gain_mix, w_four_bf, b4, L=DEC_SEQ, nb=DEC_BATCH,
                            blk0=blk_s))
        x = (_ffn_call(x, mod, l, norm_ffn[l].reshape(1, D_MODEL), w_ffn_in_bf, w_ffn_out_bf),)

    y_p, y_s = _final_call(x[0], norm_final.reshape(1, D_MODEL))
    return (y_p.reshape(BATCH, SEQ, D_MODEL), y_s.reshape(DEC_BATCH, DEC_SEQ, D_MODEL),
            kv_new[0].reshape(BATCH, N_EVEN, SEQ, ATT_HEADS, 2, ATT_HD),
            kv_new[1].reshape(BATCH, N_EVEN, SEQ, ATT_HEADS, ATT_VD),
            states[0].reshape(BATCH, N_EVEN, SSD_HEADS, SSD_HD, D_STATE),
            states[1].reshape(BATCH, N_EVEN, SSD_HEADS, SSD_HD, D_STATE))
```

```python
import functools
import math

import numpy as np
import jax
import jax.numpy as jnp
from jax import lax
from jax.experimental import pallas as pl
from jax.experimental.pallas import tpu as pltpu

D_MODEL = 1024
BATCH = 16
SEQ = 256
DEPTH = 4
N_EVEN = 2
DEC_BATCH = 4
DEC_SEQ = 1024
PAST_LEN = 512
GRID_W = 64
SSD_HEADS = 16
SSD_HD = 64
SSD_INNER = 1024
SSD_GROUPS = 4
D_STATE = 128
CONV_W = 5
CONV_CH = 2048
CHUNK = 128
ATT_HEADS = 8
ATT_HD = 64
ATT_VD = 128
ATT_INNER = 1024
ROPE_THETA = 10000.0
ROPE_PAIRS = 16
FOUR_GROUPS = 4
FOUR_GC = D_MODEL // FOUR_GROUPS
FFN_HIDDEN = 2816
EPS = 1e-6

P_TOK = BATCH * SEQ
S_TOK = DEC_BATCH * DEC_SEQ
N_TOK = P_TOK + S_TOK
N_MOD = 8
PA_W = CONV_CH + SSD_INNER
PB_W = 3 * ATT_INNER
DT_W = 128
NH2 = SSD_HEADS // 2
LANE = 128
SUBLANE = 8
N_SLAB = CONV_CH // LANE
QBLK = 32
MXU_N = 256
VMEM_LIMIT = 56 * 1024 * 1024
VMEM_LIMIT_INPROJ = 61 * 1024 * 1024

F32 = jnp.float32
BF16 = jnp.bfloat16


def _params(*sem, vmem=VMEM_LIMIT):
    return pltpu.CompilerParams(dimension_semantics=sem, vmem_limit_bytes=vmem)


def _dot(a, b):
    return jnp.dot(a, b, preferred_element_type=F32)


def _dot_nt(a, b):
    return lax.dot_general(a, b, (((1,), (1,)), ((), ())), preferred_element_type=F32)


def _silu(x):
    return x * jax.nn.sigmoid(x)


def _rms(x):
    return x * lax.rsqrt(jnp.mean(x * x, axis=-1, keepdims=True) + EPS)


def _mod_row(tm):
    return lambda i: jnp.maximum((i * tm) // DEC_SEQ - (P_TOK // DEC_SEQ - 1), 0)


def _drop_args(kern, start, n):
    def wrapped(*refs):
        return kern(*(refs[:start] + refs[start + n:]))
    return wrapped


def _tok_specs(parts, tm):
    width = parts[0].shape[-1]
    if len(parts) == 1:
        return [pl.BlockSpec((tm, width), lambda i: (i, 0))]
    npt = P_TOK // tm
    return [pl.BlockSpec((tm, width), lambda i: (jnp.minimum(i, npt - 1), 0)),
            pl.BlockSpec((tm, width), lambda i: (jnp.maximum(i - npt, 0), 0))]


def _tok_load(refs, tm):
    if len(refs) == 1:
        return refs[0][...]
    return jnp.where(pl.program_id(0) < P_TOK // tm, refs[0][...], refs[1][...])


def _ada_kernel(c_ref, w_ref, b_ref, o_ref):
    s = _silu(c_ref[...]).astype(BF16)
    o_ref[...] = _dot(s, w_ref[...].astype(BF16)) + b_ref[...]


def _ada_call(cvec, w_ada, b_ada):
    tn = 1024
    return pl.pallas_call(
        _ada_kernel,
        grid=(DEPTH, 6 * D_MODEL // tn),
        in_specs=[
            pl.BlockSpec((N_MOD, D_MODEL), lambda l, j: (0, 0)),
            pl.BlockSpec((None, D_MODEL, tn), lambda l, j: (l, 0, j)),
            pl.BlockSpec((None, 1, tn), lambda l, j: (l, 0, j)),
        ],
        out_specs=pl.BlockSpec((None, N_MOD, tn), lambda l, j: (l, 0, j)),
        out_shape=jax.ShapeDtypeStruct((DEPTH, N_MOD, 6 * D_MODEL), F32),
        compiler_params=_params("parallel", "parallel"),
        name="adaln",
    )(cvec, w_ada, b_ada.reshape(DEPTH, 1, 6 * D_MODEL))


def _regroup_kernel(w_ref, o_ref):
    o1, o2, o3 = SSD_INNER, SSD_INNER + CONV_CH, SSD_INNER + CONV_CH + 2 * SSD_HEADS
    o_ref[:, 0:CONV_CH] = w_ref[:, o1:o2].astype(BF16)
    o_ref[:, CONV_CH:PA_W] = w_ref[:, 0:o1].astype(BF16)
    tail = w_ref[:, o2:]
    o_ref[:, PA_W:PA_W + PB_W] = tail[:, o3 - o2:].astype(BF16)
    lane = lax.broadcasted_iota(jnp.int32, (1, DT_W), 1)
    o_ref[:, PA_W + PB_W:] = jnp.where(lane < o3 - o2, tail[:, 0:DT_W], 0.0).astype(BF16)


def _regroup_call(w_in_ab):
    tr = 128
    n_in = w_in_ab.shape[-1]
    return pl.pallas_call(
        _regroup_kernel,
        grid=(N_EVEN, D_MODEL // tr),
        in_specs=[pl.BlockSpec((None, tr, n_in), lambda e, i: (e, i, 0))],
        out_specs=pl.BlockSpec((None, tr, PA_W + PB_W + DT_W), lambda e, i: (e, i, 0)),
        out_shape=jax.ShapeDtypeStruct((N_EVEN, D_MODEL, PA_W + PB_W + DT_W), BF16),
        compiler_params=_params("parallel", "parallel"),
        name="regroup_w_in",
    )(w_in_ab)


def _inproj_kernel(*refs, n_x, tm, all_slots):
    x_refs = refs[:n_x]
    mod_ref, g_ref, w_ref, xbc_ref, z_ref, pb_ref, dt_ref, kn_ref, vn_ref = refs[n_x:]
    x = _tok_load(x_refs, tm)
    h = _rms(x) * g_ref[...] * (1.0 + mod_ref[1:2, :]) + mod_ref[0:1, :]
    hb = h.astype(BF16)
    tn = ATT_INNER
    for j in range(CONV_CH // tn):
        res = _dot(hb, w_ref[:, j * tn:(j + 1) * tn])
        for s in range(tn // LANE):
            xbc_ref[j * (tn // LANE) + s] = res[:, s * LANE:(s + 1) * LANE]
    z_ref[...] = _dot(hb, w_ref[:, CONV_CH:PA_W])
    q = _dot(hb, w_ref[:, PA_W:PA_W + tn])
    pb_ref[:, 0:tn] = q.astype(BF16)
    k = _dot(hb, w_ref[:, PA_W + tn:PA_W + 2 * tn])
    pb_ref[:, tn:2 * tn] = k.astype(BF16)
    v = _dot(hb, w_ref[:, PA_W + 2 * tn:PA_W + 3 * tn])
    pb_ref[:, 2 * tn:3 * tn] = v.astype(BF16)
    dt_ref[...] = _dot(hb, w_ref[:, PA_W + PB_W:PA_W + PB_W + DT_W])

    @pl.when(pl.program_id(0) < P_TOK // tm)
    def _():
        for new_ref, val in ((kn_ref, k), (vn_ref, v)):
            val = val.reshape(tm // SEQ, SEQ, ATT_INNER)
            if all_slots:
                new_ref[:, 0] = val
                for later in range(1, N_EVEN):
                    new_ref[:, later] = jnp.zeros_like(val)
            else:
                new_ref[...] = val


def _inproj_call(x, mod, l, e, gain, w_in, kv_prev):
    tm = 512
    row = _mod_row(tm)
    npt = P_TOK // tm
    n_x = len(x)
    args = [*x, mod, gain, w_in]
    in_specs = _tok_specs(x, tm) + [
        pl.BlockSpec((None, None, 6, D_MODEL), lambda i: (l, row(i), 0, 0)),
        pl.BlockSpec((1, D_MODEL), lambda i: (0, 0)),
        pl.BlockSpec((None, D_MODEL, PA_W + PB_W + DT_W), lambda i: (e, 0, 0),
                     pipeline_mode=pl.Buffered(1)),
    ]
    kern = functools.partial(_inproj_kernel, n_x=n_x, tm=tm, all_slots=kv_prev is None)
    if kv_prev is None:
        io_alias = {}
        kv_spec = pl.BlockSpec((tm // SEQ, N_EVEN, SEQ, ATT_INNER),
                               lambda i: (jnp.minimum(i, npt - 1), 0, 0, 0))
    else:
        io_alias = {n_x + 3: 4, n_x + 4: 5}
        args += list(kv_prev)
        in_specs += [pl.BlockSpec(memory_space=pl.ANY)] * 2
        kern = _drop_args(kern, n_x + 3, 2)
        kv_spec = pl.BlockSpec((tm // SEQ, None, SEQ, ATT_INNER),
                               lambda i: (jnp.minimum(i, npt - 1), e, 0, 0))
    kv_shape = jax.ShapeDtypeStruct((BATCH, N_EVEN, SEQ, ATT_INNER), F32)
    return pl.pallas_call(
        kern,
        grid=(N_TOK // tm,),
        in_specs=in_specs,
        out_specs=[
            pl.BlockSpec((N_SLAB, tm, LANE), lambda i: (0, i, 0)),
            pl.BlockSpec((tm, SSD_INNER), lambda i: (i, 0)),
            pl.BlockSpec((tm, PB_W), lambda i: (i, 0)),
            pl.BlockSpec((tm, DT_W), lambda i: (i, 0)),
            kv_spec, kv_spec,
        ],
        out_shape=[
            jax.ShapeDtypeStruct((N_SLAB, N_TOK, LANE), F32),
            jax.ShapeDtypeStruct((N_TOK, SSD_INNER), F32),
            jax.ShapeDtypeStruct((N_TOK, PB_W), BF16),
            jax.ShapeDtypeStruct((N_TOK, DT_W), F32),
            kv_shape, kv_shape,
        ],
        input_output_aliases=io_alias,
        compiler_params=_params("arbitrary", vmem=VMEM_LIMIT_INPROJ),
        name="inproj",
    )(*args)


def _ssd_kernel(*refs, L, has_h0, emit_state, all_slots):
    it = iter(refs)
    xbc_ref, z_ref, dt_ref = next(it), next(it), next(it)
    convw_ref, convb_ref, dtb_ref, alog_ref, dskip_ref, norm_ref = (next(it) for _ in range(6))
    sel_refs = (next(it), next(it))
    if has_h0:
        h0_refs = (next(it), next(it))
    y_ref = next(it)
    if emit_state:
        hout_refs = (next(it), next(it))
    xact, xdt_f, xdt_b, bt, cact, tmp, yacc, dtv, state_f, state_b = (next(it) for _ in range(10))

    nc = L // CHUNK
    nq = L // SUBLANE
    half_w = CONV_W // 2

    rq = lax.broadcasted_iota(jnp.int32, (QBLK, LANE), 0)

    def phase(s, pe, q0):
        r0 = SUBLANE * q0 + pe
        if r0 < 0:
            a = xbc_ref[s, pl.ds(r0 + SUBLANE, QBLK, stride=SUBLANE), :]
            return jnp.where(rq == 0, 0.0, pltpu.roll(a, 1, 0))
        if r0 + SUBLANE * (QBLK - 1) >= L:
            a = xbc_ref[s, pl.ds(r0 - SUBLANE, QBLK, stride=SUBLANE), :]
            return jnp.where(rq == QBLK - 1, 0.0, pltpu.roll(a, QBLK - 1, 0))
        return xbc_ref[s, pl.ds(r0, QBLK, stride=SUBLANE), :]

    def conv_slab(s, dst, d_idx):
        ls = slice(s * LANE, (s + 1) * LANE)
        for q0 in range(0, nq, QBLK):
            ext = {pe: phase(s, pe, q0) for pe in range(-half_w, SUBLANE + half_w)}
            for p in range(SUBLANE):
                acc = jnp.broadcast_to(convb_ref[:, ls], (QBLK, LANE))
                for k in range(CONV_W):
                    acc = acc + convw_ref[k:k + 1, ls] * ext[p + k - half_w]
                dst[d_idx, pl.ds(SUBLANE * q0 + p, QBLK, stride=SUBLANE), :] = _silu(acc)

    for m in range(NH2):
        conv_slab(m, xact, m)
    for g in range(SSD_GROUPS):
        conv_slab(NH2 + g, tmp, g)
    for c in range(nc):
        for g in range(SSD_GROUPS):
            bt[c, g * D_STATE:(g + 1) * D_STATE, :] = (
                tmp[g, c * CHUNK:(c + 1) * CHUNK, :].T.astype(BF16))
    for g in range(SSD_GROUPS):
        conv_slab(NH2 + SSD_GROUPS + g, tmp, g)
    for c in range(nc):
        rows = slice(c * CHUNK, (c + 1) * CHUNK)
        for g in range(SSD_GROUPS):
            cact[g, rows, :] = tmp[g, rows, :].astype(BF16)

    dr = dt_ref[...] + dtb_ref[...]
    dtv[...] = jnp.maximum(dr, 0.0) + jnp.log1p(jnp.exp(-jnp.abs(dr)))
    a_row = -jnp.exp(alog_ref[...]) * math.log2(math.e)

    for c in range(nc):
        rows = slice(c * CHUNK, (c + 1) * CHUNK)
        d = dtv[rows, :]
        d_hi = d.astype(BF16)
        d_lo = (d - d_hi.astype(F32)).astype(BF16)
        d2 = jnp.concatenate([d_hi, d_lo], axis=1)
        spread_f = _dot(d2, sel_refs[0][...])
        spread_b = _dot(d2, sel_refs[1][...])
        for m in range(NH2):
            lanes = slice(m * LANE, (m + 1) * LANE)
            xm = xact[m, rows, :]
            xdt_f[m, rows, :] = (xm * spread_f[:, lanes]).astype(BF16)
            xdt_b[m, rows, :] = (xm * spread_b[:, lanes]).astype(BF16)

    ri = lax.broadcasted_iota(jnp.int32, (CHUNK, CHUNK), 0)
    ci = lax.broadcasted_iota(jnp.int32, (CHUNK, CHUNK), 1)
    low_half = ci < SSD_HD
    low_row = low_half[0:1, :]

    for d_idx, st_ref in enumerate((state_f, state_b)):
        for m in range(NH2):
            if has_h0:
                st_ref[m] = h0_refs[d_idx][m].T
            else:
                st_ref[m] = jnp.zeros((D_STATE, LANE), F32)

    def chunk_step(c, fwd, add):
        mask = (ri >= ci) if fwd else (ri <= ci)
        tri = mask.astype(F32)
        col0 = 0 if fwd else SSD_HEADS
        last = CHUNK - 1 if fwd else 0
        xdt = xdt_f if fwd else xdt_b
        state = state_f if fwd else state_b
        rows = pl.ds(pl.multiple_of(c * CHUNK, CHUNK), CHUNK)
        dtc = dtv[rows, :]
        acs = jnp.dot(tri, dtc * a_row, preferred_element_type=F32,
                      precision=lax.Precision.HIGHEST)
        acs_t = acs.T
        acs_last = acs[last:last + 1, :]
        for g in range(SSD_GROUPS):
            btg = bt[c, g * D_STATE:(g + 1) * D_STATE, :]
            cg = cact[g, rows, :]
            cbm = jnp.where(mask, _dot(cg, btg), 0.0)
            for pr in range(2):
                m = g * 2 + pr
                ha, hb = col0 + 2 * m, col0 + 2 * m + 1
                lanes = slice(m * LANE, (m + 1) * LANE)
                xd2 = xdt[m, rows, :]
                acol_a = jnp.broadcast_to(acs[:, ha:ha + 1], (CHUNK, CHUNK))
                acol_b = jnp.broadcast_to(acs[:, hb:hb + 1], (CHUNK, CHUNK))
                w_a = cbm * jnp.exp2(jnp.minimum(acol_a - acs_t[ha:ha + 1, :], 0.0))
                w_b = cbm * jnp.exp2(jnp.minimum(acol_b - acs_t[hb:hb + 1, :], 0.0))
                w2 = jnp.concatenate([w_a.astype(BF16), w_b.astype(BF16)], axis=1)
                zero = jnp.zeros_like(xd2)
                x_blk = jnp.concatenate([jnp.where(low_half, xd2, zero),
                                         jnp.where(low_half, zero, xd2)], axis=0)
                y_in = _dot(w2, x_blk)
                acol2 = jnp.where(low_half, acol_a, acol_b)
                last2 = jnp.where(low_row, acs_last[:, ha:ha + 1], acs_last[:, hb:hb + 1])
                h_prev = state[m]
                y_out = _dot(cg, h_prev.astype(BF16)) * jnp.exp2(acol2)
                to_end = jnp.exp2(last2 - acol2)
                st = _dot(btg, (to_end * xd2.astype(F32)).astype(BF16))
                state[m] = jnp.exp2(last2) * h_prev + st
                y2 = y_in + y_out
                if fwd:
                    y2 = y2 + dskip_ref[:, lanes] * xact[m, rows, :]
                if add:
                    y2 = y2 + yacc[rows, lanes]
                yacc[rows, lanes] = y2

    def finish(c):
        rows = pl.ds(pl.multiple_of(c * CHUNK, CHUNK), CHUNK)
        yt = yacc[rows, :] * _silu(z_ref[rows, :])
        y_ref[rows, :] = (_rms(yt) * norm_ref[...]).astype(y_ref.dtype)

    def first_half(s, carry):
        chunk_step(s, True, False)
        chunk_step(nc - 1 - s, False, False)
        return carry

    def second_half(s, carry):
        chunk_step(s, True, True)
        chunk_step(nc - 1 - s, False, True)
        finish(s)
        finish(nc - 1 - s)
        return carry

    lax.fori_loop(0, nc // 2, first_half, 0)
    lax.fori_loop(nc // 2, nc, second_half, 0)

    if emit_state:
        for d_idx, st_ref in enumerate((state_f, state_b)):
            for m in range(NH2):
                if all_slots:
                    hout_refs[d_idx][0, m] = st_ref[m].T
                    for later in range(1, N_EVEN):
                        hout_refs[d_idx][later, m] = jnp.zeros((LANE, D_STATE), F32)
                else:
                    hout_refs[d_idx][m] = st_ref[m].T


def _head_spread_tables():
    ch_head = np.arange(SSD_INNER) // SSD_HD
    out = []
    for d in range(2):
        k = np.arange(2 * DT_W) % DT_W
        out.append(jnp.asarray(k[:, None] == (d * SSD_HEADS + ch_head)[None, :], BF16))
    return out


def _ssd_call(xbc, z, dtarr, states_prev, weights, e, *, L, nb, blk0, h0=None):
    emit_state = states_prev is not None
    all_slots = isinstance(states_prev, str)
    nblk = N_TOK // L
    nc = L // CHUNK
    vec = lambda w: pl.BlockSpec((1, w), lambda b: (0, 0))
    in_specs = [
        pl.BlockSpec((N_SLAB, L, LANE), lambda b: (0, blk0 + b, 0)),
        pl.BlockSpec((None, L, SSD_INNER), lambda b: (blk0 + b, 0, 0)),
        pl.BlockSpec((None, L, DT_W), lambda b: (blk0 + b, 0, 0)),
        pl.BlockSpec((8, CONV_CH), lambda b: (0, 0)),
        vec(CONV_CH), vec(DT_W), vec(DT_W), vec(SSD_INNER), vec(SSD_INNER),
        pl.BlockSpec((2 * DT_W, SSD_INNER), lambda b: (0, 0)),
        pl.BlockSpec((2 * DT_W, SSD_INNER), lambda b: (0, 0)),
    ]
    args = [xbc, z.reshape(nblk, L, SSD_INNER), dtarr.reshape(nblk, L, DT_W), *weights,
            *_head_spread_tables()]
    state_spec = pl.BlockSpec((None, None, NH2, LANE, D_STATE), lambda b: (b, e, 0, 0, 0))
    if h0 is not None:
        for h in h0:
            args.append(h.reshape(DEC_BATCH, N_EVEN, NH2, LANE, D_STATE))
            in_specs.append(state_spec)
    kern = functools.partial(_ssd_kernel, L=L, has_h0=h0 is not None, emit_state=emit_state,
                             all_slots=all_slots)
    out_specs = [pl.BlockSpec((None, L, SSD_INNER), lambda b: (b, 0, 0))]
    out_shape = [jax.ShapeDtypeStruct((nb, L, SSD_INNER), BF16)]
    io_alias = {}
    if emit_state:
        if all_slots:
            out_spec = pl.BlockSpec((None, N_EVEN, NH2, LANE, D_STATE), lambda b: (b, 0, 0, 0, 0))
        else:
            out_spec = state_spec
            io_alias = {len(args): 1, len(args) + 1: 2}
            kern = _drop_args(kern, len(args), 2)
            args += list(states_prev)
            in_specs += [pl.BlockSpec(memory_space=pl.ANY)] * 2
        out_specs += [out_spec, out_spec]
        out_shape += [jax.ShapeDtypeStruct((nb, N_EVEN, NH2, LANE, D_STATE), F32)] * 2
    return pl.pallas_call(
        kern,
        grid=(nb,),
        in_specs=in_specs,
        out_specs=out_specs,
        out_shape=out_shape,
        scratch_shapes=[
            pltpu.VMEM((NH2, L, LANE), F32),
            pltpu.VMEM((NH2, L, LANE), BF16),
            pltpu.VMEM((NH2, L, LANE), BF16),
            pltpu.VMEM((nc, SSD_GROUPS * D_STATE, CHUNK), BF16),
            pltpu.VMEM((SSD_GROUPS, L, D_STATE), BF16),
            pltpu.VMEM((SSD_GROUPS, L, D_STATE), F32),
            pltpu.VMEM((L, SSD_INNER), F32),
            pltpu.VMEM((L, DT_W), F32),
            pltpu.VMEM((NH2, D_STATE, LANE), F32),
            pltpu.VMEM((NH2, D_STATE, LANE), F32),
        ],
        input_output_aliases=io_alias,
        compiler_params=_params("parallel"),
        name=f"ssd_L{L}",
    )(*args)


def _rope(x, cos, sin_a, sin_b):
    return (x * cos + pltpu.roll(x, LANE - ROPE_PAIRS, 1) * sin_a
            + pltpu.roll(x, ROPE_PAIRS, 1) * sin_b)


def _attn_kernel(*refs, L, n_ctx, lam_init, stack_maps):
    it = iter(refs)
    q_ref, k_ref, v_ref = next(it), next(it), next(it)
    if n_ctx:
        ck_ref, cv_ref = next(it), next(it)
        cq, saq, sbq, ck, sak, sbk = (next(it) for _ in range(6))
    lam_ref, sub_ref = next(it), next(it)
    o_ref = next(it)
    kall, vext = next(it), next(it)
    lk = n_ctx + L
    tq = q_ref.shape[0]

    @pl.when(pl.program_id(1) == 0)
    def _():
        ones = jnp.ones((lk, MXU_N - ATT_VD), BF16)
        for h in range(ATT_HEADS):
            hs = slice(h * ATT_VD, (h + 1) * ATT_VD)
            if n_ctx:
                kall[h, 0:n_ctx, :] = ck_ref[:, hs].astype(BF16)
                vext[h, 0:n_ctx, 0:ATT_VD] = cv_ref[:, hs].astype(BF16)
                kh = _rope(k_ref[:, hs].astype(F32), ck[...], sak[...], sbk[...])
                kall[h, n_ctx:lk, :] = kh.astype(BF16)
            else:
                kall[h] = k_ref[:, hs]
            vext[h, n_ctx:lk, 0:ATT_VD] = v_ref[:, hs]
            vext[h, :, ATT_VD:MXU_N] = ones

    lane = lax.broadcasted_iota(jnp.int32, (1, ATT_VD), 1)
    scale = ATT_HD ** -0.5 * math.log2(math.e)
    m1 = jnp.where(lane < ATT_HD, scale, 0.0)
    m2 = jnp.where(lane < ATT_HD, 0.0, scale)
    lq = lam_ref[...]
    lam = (jnp.exp(jnp.sum(lq[0:1, :] * lq[1:2, :], axis=-1, keepdims=True))
           - jnp.exp(jnp.sum(lq[2:3, :] * lq[3:4, :], axis=-1, keepdims=True)) + lam_init)

    for h in range(ATT_HEADS):
        hs = slice(h * ATT_VD, (h + 1) * ATT_VD)
        q = q_ref[:, hs].astype(F32)
        if n_ctx:
            q = _rope(q, cq[...], saq[...], sbq[...])
        def attend(qm):
            s = _dot_nt(qm, kall[h])
            p = jnp.exp2(s - jnp.max(s, axis=-1, keepdims=True)).astype(BF16)
            pv = _dot(p, vext[h])
            return pv[:, 0:ATT_VD] / pv[:, ATT_VD:MXU_N]

        q1, q2 = (q * m1).astype(BF16), (q * m2).astype(BF16)
        if stack_maps:
            on = attend(jnp.concatenate([q1, q2], axis=0))
            o = on[0:tq] - lam * on[tq:2 * tq]
        else:
            o = attend(q1) - lam * attend(q2)
        o_ref[:, hs] = (_rms(o) * sub_ref[...] * (1.0 - lam_init)).astype(o_ref.dtype)


def _attn_call(pb, lam_qk, subln, lam_init, *, L, nb, blk0, tq, ctx=None, rope=None):
    nblk = N_TOK // L
    nq = L // tq
    pb3 = pb.reshape(nblk, L, PB_W)
    in_specs = [
        pl.BlockSpec((None, tq, ATT_INNER), lambda b, i: (blk0 + b, i, 0)),
        pl.BlockSpec((None, L, ATT_INNER), lambda b, i: (blk0 + b, 0, 1)),
        pl.BlockSpec((None, L, ATT_INNER), lambda b, i: (blk0 + b, 0, 2)),
    ]
    args = [pb3, pb3, pb3]
    n_ctx = 0
    if ctx is not None:
        cache_k, cache_v, e = ctx
        n_ctx = cache_k.shape[2]
        for cch in (cache_k, cache_v):
            args.append(cch.reshape(DEC_BATCH, N_EVEN, n_ctx, ATT_INNER))
            in_specs.append(pl.BlockSpec((None, None, n_ctx, ATT_INNER), lambda b, i: (b, e, 0, 0)))
        for t in rope:
            args.append(t)
            in_specs.append(pl.BlockSpec((tq, ATT_VD), lambda b, i: (i, 0)))
        for t in rope:
            args.append(t)
            in_specs.append(pl.BlockSpec((L, ATT_VD), lambda b, i: (0, 0)))
    args += [lam_qk, subln]
    in_specs += [pl.BlockSpec((4, ATT_HD), lambda b, i: (0, 0)),
                 pl.BlockSpec((1, ATT_VD), lambda b, i: (0, 0))]
    return pl.pallas_call(
        functools.partial(_attn_kernel, L=L, n_ctx=n_ctx, lam_init=lam_init,
                          stack_maps=n_ctx + L <= 2 * MXU_N),
        grid=(nb, nq),
        in_specs=in_specs,
        out_specs=pl.BlockSpec((None, tq, ATT_INNER), lambda b, i: (b, i, 0)),
        out_shape=jax.ShapeDtypeStruct((nb, L, ATT_INNER), BF16),
        scratch_shapes=[pltpu.VMEM((ATT_HEADS, n_ctx + L, ATT_VD), BF16),
                        pltpu.VMEM((ATT_HEADS, n_ctx + L, MXU_N), BF16)],
        compiler_params=_params("parallel", "arbitrary"),
        name=f"attn_L{L}",
    )(*args)


def _outproj_kernel(*refs, n_x, tm):
    x_refs = refs[:n_x]
    yp_ref, ys_ref, op_ref, os_ref, mod_ref, w_ref, out_ref = refs[n_x:]
    y = _tok_load((yp_ref, ys_ref), tm)
    o = _tok_load((op_ref, os_ref), tm)
    acc = _dot(y, w_ref[0:SSD_INNER, :]) + _dot(o, w_ref[SSD_INNER:, :])
    out_ref[...] = _tok_load(x_refs, tm) + mod_ref[2:3, :] * acc


def _outproj_call(x, y, o, mod, l, e, w):
    tm = 512
    row = _mod_row(tm)
    return pl.pallas_call(
        functools.partial(_outproj_kernel, n_x=len(x), tm=tm),
        grid=(N_TOK // tm,),
        in_specs=_tok_specs(x, tm) + _tok_specs(y, tm) + _tok_specs(o, tm) + [
            pl.BlockSpec((None, None, 6, D_MODEL), lambda i: (l, row(i), 0, 0)),
            pl.BlockSpec((None, SSD_INNER + ATT_INNER, D_MODEL), lambda i: (e, 0, 0))],
        out_specs=pl.BlockSpec((tm, D_MODEL), lambda i: (i, 0)),
        out_shape=jax.ShapeDtypeStruct((N_TOK, D_MODEL), F32),
        compiler_params=_params("arbitrary"),
        name="outproj",
    )(*x, *y, *o, mod, w)


def _ffn_kernel(*refs, n_x, tm, final):
    x_refs = refs[:n_x]
    if final:
        mod_ref, g_ref, win_ref, wout_ref, gf_ref, op_ref, os_ref, h_ref, a_ref = refs[n_x:]
    else:
        mod_ref, g_ref, win_ref, wout_ref, out_ref, h_ref, a_ref = refs[n_x:]
    x = _tok_load(x_refs, tm)
    h = _rms(x) * g_ref[...] * (1.0 + mod_ref[4:5, :]) + mod_ref[3:4, :]
    h_ref[...] = h.astype(BF16)
    for j in range(FFN_HIDDEN // MXU_N):
        cs = slice(j * MXU_N, (j + 1) * MXU_N)
        us = slice(FFN_HIDDEN + j * MXU_N, FFN_HIDDEN + (j + 1) * MXU_N)
        hb = h_ref[...]
        a_ref[:, cs] = (_silu(_dot(hb, win_ref[:, cs])) * _dot(hb, win_ref[:, us])).astype(BF16)
    x_new = _tok_load(x_refs, tm) + mod_ref[5:6, :] * _dot(a_ref[...], wout_ref[...])
    if final:
        y = _rms(x_new) * gf_ref[...]

        @pl.when(pl.program_id(0) < P_TOK // tm)
        def _():
            op_ref[...] = y

        @pl.when(pl.program_id(0) >= P_TOK // tm)
        def _():
            os_ref[...] = y
    else:
        out_ref[...] = x_new


def _ffn_call(x, mod, l, gain, w_in, w_out, final_gain=None):
    tm = 1024
    row = _mod_row(tm)
    npt = P_TOK // tm
    once = pl.Buffered(1)
    final = final_gain is not None
    args = [*x, mod, gain, w_in, w_out]
    in_specs = _tok_specs(x, tm) + [
        pl.BlockSpec((None, None, 6, D_MODEL), lambda i: (l, row(i), 0, 0)),
        pl.BlockSpec((1, D_MODEL), lambda i: (0, 0)),
        pl.BlockSpec((None, D_MODEL, 2 * FFN_HIDDEN), lambda i: (l, 0, 0), pipeline_mode=once),
        pl.BlockSpec((None, FFN_HIDDEN, D_MODEL), lambda i: (l, 0, 0), pipeline_mode=once),
    ]
    if final:
        args.append(final_gain)
        in_specs.append(pl.BlockSpec((1, D_MODEL), lambda i: (0, 0)))
        out_specs = [pl.BlockSpec((tm, D_MODEL), lambda i: (jnp.minimum(i, npt - 1), 0)),
                     pl.BlockSpec((tm, D_MODEL), lambda i: (jnp.maximum(i - npt, 0), 0))]
        out_shape = [jax.ShapeDtypeStruct((P_TOK, D_MODEL), F32),
                     jax.ShapeDtypeStruct((S_TOK, D_MODEL), F32)]
    else:
        out_specs = pl.BlockSpec((tm, D_MODEL), lambda i: (i, 0))
        out_shape = jax.ShapeDtypeStruct((N_TOK, D_MODEL), F32)
    return pl.pallas_call(
        functools.partial(_ffn_kernel, n_x=len(x), tm=tm, final=final),
        grid=(N_TOK // tm,),
        in_specs=in_specs,
        out_specs=out_specs,
        out_shape=out_shape,
        scratch_shapes=[pltpu.VMEM((tm, D_MODEL), BF16), pltpu.VMEM((tm, FFN_HIDDEN), BF16)],
        compiler_params=_params("arbitrary"),
        name="ffn",
    )(*args)


def _four_kernel(x_ref, mod_ref, g_ref, cc_ref, sc_ref, cs_ref, w_ref, b_ref, out_ref, stk_ref, *, L):
    x = x_ref[...]
    h = (_rms(x) * g_ref[...] * (1.0 + mod_ref[1:2, :]) + mod_ref[0:1, :]).astype(BF16)
    for g in range(FOUR_GROUPS):
        ls = slice(g * FOUR_GC, (g + 1) * FOUR_GC)
        hg = h[:, ls]
        stk_ref[0:L, ls] = _dot(hg, cc_ref[...]).astype(BF16)
        stk_ref[L:2 * L, ls] = _dot(hg, sc_ref[...]).astype(BF16)
    f = _dot(cs_ref[...], stk_ref[...]).astype(BF16)
    out_ref[...] = x + mod_ref[2:3, :] * (_dot(f, w_ref[...]) + b_ref[...])


def _dft_tables(L):
    n = FOUR_GC
    ang_c = 2.0 * np.pi * ((np.arange(n)[:, None] * np.arange(n)[None, :]) % n) / n
    ang_l = 2.0 * np.pi * ((np.arange(L)[:, None] * np.arange(L)[None, :]) % L) / L
    cc = np.cos(ang_c) / math.sqrt(n)
    sc = np.sin(ang_c) / math.sqrt(n)
    cs = np.concatenate([np.cos(ang_l), -np.sin(ang_l)], axis=1) / math.sqrt(L)
    return (jnp.asarray(cc, F32).astype(BF16), jnp.asarray(sc, F32).astype(BF16),
            jnp.asarray(cs, F32).astype(BF16))


def _four_call(x, mod, l, i_odd, gain, w, b, *, L, nb, blk0):
    nblk = N_TOK // L
    cc, sc, cs = _dft_tables(L)
    row = _mod_row(L)
    out = pl.pallas_call(
        functools.partial(_four_kernel, L=L),
        grid=(nb,),
        in_specs=[
            pl.BlockSpec((None, L, D_MODEL), lambda i: (blk0 + i, 0, 0)),
            pl.BlockSpec((None, None, 6, D_MODEL), lambda i: (l, row(blk0 + i), 0, 0)),
            pl.BlockSpec((1, D_MODEL), lambda i: (0, 0)),
            pl.BlockSpec((FOUR_GC, FOUR_GC), lambda i: (0, 0)),
            pl.BlockSpec((FOUR_GC, FOUR_GC), lambda i: (0, 0)),
            pl.BlockSpec((L, 2 * L), lambda i: (0, 0)),
            pl.BlockSpec((None, D_MODEL, D_MODEL), lambda i: (i_odd, 0, 0)),
            pl.BlockSpec((1, D_MODEL), lambda i: (0, 0)),
        ],
        out_specs=pl.BlockSpec((None, L, D_MODEL), lambda i: (i, 0, 0)),
        out_shape=jax.ShapeDtypeStruct((nb, L, D_MODEL), F32),
        scratch_shapes=[pltpu.VMEM((2 * L, D_MODEL), BF16)],
        compiler_params=_params("parallel"),
        name=f"fourier_L{L}",
    )(x.reshape(nblk, L, D_MODEL), mod, gain, cc, sc, cs, w, b)
    return out.reshape(nb * L, D_MODEL)


def _rope_lane_tables(L):
    t = jnp.arange(L)
    row = (t // GRID_W).astype(F32)
    col = (t % GRID_W).astype(F32)
    freq = ROPE_THETA ** (-jnp.arange(ROPE_PAIRS, dtype=F32) / ROPE_PAIRS)
    ar = row[:, None] * freq[None]
    ac = col[:, None] * freq[None]
    z = jnp.zeros_like(ar)
    cos32 = lambda a: jnp.concatenate([jnp.cos(a), jnp.cos(a)], axis=-1)
    sa32 = lambda a: jnp.concatenate([-jnp.sin(a), z], axis=-1)
    sb32 = lambda a: jnp.concatenate([z, jnp.sin(a)], axis=-1)
    tile = lambda f: jnp.concatenate([f(ar), f(ac), f(ar), f(ac)], axis=-1)
    return tile(cos32), tile(sa32), tile(sb32)


def _pad_lanes(v, width):
    v = v.reshape(1, -1).astype(F32)
    return jnp.pad(v, ((0, 0), (0, width - v.shape[1])))


def kernel(x_prompt, x_sample, cache_k, cache_v, state_ssd_fwd, state_ssd_bwd, c, c_ctx, w_ada, b_ada, norm_mix, norm_ffn, w_in_ab, conv_w, conv_b, dt_bias, a_log, d_skip, ssd_norm, lambda_qk, subln, w_out_ab, w_four, b_four, w_ffn_in, w_ffn_out, norm_final):
    x = (x_prompt.reshape(P_TOK, D_MODEL), x_sample.reshape(S_TOK, D_MODEL))
    cvec = jnp.concatenate([c_ctx[None, :], c, jnp.zeros((N_MOD - 1 - DEC_BATCH, D_MODEL), F32)], axis=0)
    mod = _ada_call(cvec, w_ada, b_ada).reshape(DEPTH, N_MOD, 6, D_MODEL)
    rope = _rope_lane_tables(DEC_SEQ)

    w_in_bf = _regroup_call(w_in_ab)
    w_out_bf = w_out_ab.astype(BF16)
    w_four_bf = w_four.astype(BF16)
    w_ffn_in_bf = w_ffn_in.astype(BF16)
    w_ffn_out_bf = w_ffn_out.astype(BF16)

    kv_new = None
    states = "new"
    blk_s = P_TOK // DEC_SEQ
    for l in range(DEPTH):
        gain_mix = norm_mix[l].reshape(1, D_MODEL)
        if l % 2 == 0:
            e = l // 2
            lam_init = 0.8 - 0.6 * math.exp(-0.3 * l)
            xbc, z, pb, dtarr, k_new, v_new = _inproj_call(x, mod, l, e, gain_mix, w_in_bf, kv_new)
            kv_new = (k_new, v_new)

            ssd_w = (jnp.pad(conv_w[e], ((0, 8 - CONV_W), (0, 0))), conv_b[e].reshape(1, CONV_CH),
                     _pad_lanes(dt_bias[e], DT_W), _pad_lanes(a_log[e], DT_W),
                     jnp.repeat(d_skip[e], SSD_HD).reshape(1, SSD_INNER),
                     ssd_norm[e].reshape(1, SSD_INNER))
            y_p, hf, hb = _ssd_call(xbc, z, dtarr, states, ssd_w, e, L=SEQ, nb=BATCH, blk0=0)
            states = (hf, hb)
            (y_s,) = _ssd_call(xbc, z, dtarr, None, ssd_w, e, L=DEC_SEQ, nb=DEC_BATCH, blk0=blk_s,
                               h0=(state_ssd_fwd, state_ssd_bwd))
            sub = subln[e].reshape(1, ATT_VD)
            o_p = _attn_call(pb, lambda_qk[e], sub, lam_init, L=SEQ, nb=BATCH, blk0=0, tq=SEQ)
            o_s = _attn_call(pb, lambda_qk[e], sub, lam_init, L=DEC_SEQ, nb=DEC_BATCH,
                             blk0=blk_s, tq=256, ctx=(cache_k, cache_v, e), rope=rope)
            halves = lambda a, b: (a.reshape(P_TOK, -1), b.reshape(S_TOK, -1))
            x = (_outproj_call(x, halves(y_p, y_s), halves(o_p, o_s), mod, l, e, w_out_bf),)
        else:
            i_odd = l // 2
            b4 = b_four[i_odd].reshape(1, D_MODEL)
            x = (_four_call(x[0], mod, l, i_odd, gain_mix, w_four_bf, b4, L=SEQ, nb=BATCH, blk0=0),
                 _four_call(x[0], mod, l, i_odd, gain_mix, w_four_bf, b4, L=DEC_SEQ, nb=DEC_BATCH,
                            blk0=blk_s))
        last = l == DEPTH - 1
        x = _ffn_call(x, mod, l, norm_ffn[l].reshape(1, D_MODEL), w_ffn_in_bf, w_ffn_out_bf,
                      final_gain=norm_final.reshape(1, D_MODEL) if last else None)
        x = tuple(x) if last else (x,)

    y_p, y_s = x
    return (y_p.reshape(BATCH, SEQ, D_MODEL), y_s.reshape(DEC_BATCH, DEC_SEQ, D_MODEL),
            kv_new[0].reshape(BATCH, N_EVEN, SEQ, ATT_HEADS, 2, ATT_HD),
            kv_new[1].reshape(BATCH, N_EVEN, SEQ, ATT_HEADS, ATT_VD),
            states[0].reshape(BATCH, N_EVEN, SSD_HEADS, SSD_HD, D_STATE),
            states[1].reshape(BATCH, N_EVEN, SSD_HEADS, SSD_HD, D_STATE))
```

```python
import functools
import math

import numpy as np
import jax
import jax.numpy as jnp
from jax import lax
from jax.experimental import pallas as pl
from jax.experimental.pallas import tpu as pltpu

D_MODEL = 1024
BATCH = 16
SEQ = 256
DEPTH = 4
N_EVEN = 2
DEC_BATCH = 4
DEC_SEQ = 1024
PAST_LEN = 512
GRID_W = 64
SSD_HEADS = 16
SSD_HD = 64
SSD_INNER = 1024
SSD_GROUPS = 4
D_STATE = 128
CONV_W = 5
CONV_CH = 2048
CHUNK = 128
ATT_HEADS = 8
ATT_HD = 64
ATT_VD = 128
ATT_INNER = 1024
ROPE_THETA = 10000.0
ROPE_PAIRS = 16
FOUR_GROUPS = 4
FOUR_GC = D_MODEL // FOUR_GROUPS
FFN_HIDDEN = 2816
EPS = 1e-6

P_TOK = BATCH * SEQ
S_TOK = DEC_BATCH * DEC_SEQ
N_TOK = P_TOK + S_TOK
N_MOD = 8
PA_W = CONV_CH + SSD_INNER
PB_W = 3 * ATT_INNER
DT_W = 128
NH2 = SSD_HEADS // 2
LANE = 128
SUBLANE = 8
N_SLAB = CONV_CH // LANE
QBLK = 32
MXU_N = 256
VMEM_LIMIT = 56 * 1024 * 1024
VMEM_LIMIT_INPROJ = 61 * 1024 * 1024

F32 = jnp.float32
BF16 = jnp.bfloat16


def _params(*sem, vmem=VMEM_LIMIT):
    return pltpu.CompilerParams(dimension_semantics=sem, vmem_limit_bytes=vmem)


def _dot(a, b):
    return jnp.dot(a, b, preferred_element_type=F32)


def _dot_nt(a, b):
    return lax.dot_general(a, b, (((1,), (1,)), ((), ())), preferred_element_type=F32)


def _silu(x):
    h = 0.5 * x
    return h + h * jnp.tanh(h)


def _rms(x):
    return x * lax.rsqrt(jnp.mean(x * x, axis=-1, keepdims=True) + EPS)


def _mod_row(tm):
    return lambda i: jnp.maximum((i * tm) // DEC_SEQ - (P_TOK // DEC_SEQ - 1), 0)


def _drop_args(kern, start, n):
    def wrapped(*refs):
        return kern(*(refs[:start] + refs[start + n:]))
    return wrapped


def _tok_specs(parts, tm):
    width = parts[0].shape[-1]
    if len(parts) == 1:
        return [pl.BlockSpec((tm, width), lambda i: (i, 0))]
    npt = P_TOK // tm
    return [pl.BlockSpec((tm, width), lambda i: (jnp.minimum(i, npt - 1), 0)),
            pl.BlockSpec((tm, width), lambda i: (jnp.maximum(i - npt, 0), 0))]


def _tok_load(refs, tm):
    if len(refs) == 1:
        return refs[0][...]
    return jnp.where(pl.program_id(0) < P_TOK // tm, refs[0][...], refs[1][...])


def _launch(parts, grid, sem, name):
    n_in = [len(p["args"]) for p in parts]
    n_out = [len(p["out_shape"]) for p in parts]
    n_scr = [len(p["scratch"]) for p in parts]

    def kern(*refs):
        ins = refs[:sum(n_in)]
        outs = refs[sum(n_in):sum(n_in) + sum(n_out)]
        scr = refs[sum(n_in) + sum(n_out):]
        i0 = o0 = s0 = 0
        for p, ni, no, ns in zip(parts, n_in, n_out, n_scr):
            p["kernel"](*ins[i0:i0 + ni], *outs[o0:o0 + no], *scr[s0:s0 + ns])
            i0, o0, s0 = i0 + ni, o0 + no, s0 + ns

    alias = {}
    i0 = o0 = 0
    for p, ni, no in zip(parts, n_in, n_out):
        alias.update({i0 + k: o0 + v for k, v in p["alias"].items()})
        i0, o0 = i0 + ni, o0 + no
    cat = lambda key: [v for p in parts for v in p[key]]
    outs = pl.pallas_call(
        kern, grid=grid, in_specs=cat("in_specs"), out_specs=cat("out_specs"),
        out_shape=cat("out_shape"), scratch_shapes=cat("scratch"),
        input_output_aliases=alias, compiler_params=_params(*sem), name=name,
    )(*cat("args"))
    res, o0 = [], 0
    for no in n_out:
        res.append(list(outs[o0:o0 + no]))
        o0 += no
    return res


def _ada_kernel(c_ref, w_ref, b_ref, o_ref):
    s = _silu(c_ref[...]).astype(BF16)
    o_ref[...] = _dot(s, w_ref[...].astype(BF16)) + b_ref[...]


def _ada_call(cvec, w_ada, b_ada):
    tn = 1024
    return pl.pallas_call(
        _ada_kernel,
        grid=(DEPTH, 6 * D_MODEL // tn),
        in_specs=[
            pl.BlockSpec((N_MOD, D_MODEL), lambda l, j: (0, 0)),
            pl.BlockSpec((None, D_MODEL, tn), lambda l, j: (l, 0, j)),
            pl.BlockSpec((None, 1, tn), lambda l, j: (l, 0, j)),
        ],
        out_specs=pl.BlockSpec((None, N_MOD, tn), lambda l, j: (l, 0, j)),
        out_shape=jax.ShapeDtypeStruct((DEPTH, N_MOD, 6 * D_MODEL), F32),
        compiler_params=_params("parallel", "parallel"),
        name="adaln",
    )(cvec, w_ada, b_ada.reshape(DEPTH, 1, 6 * D_MODEL))


def _regroup_kernel(w_ref, o_ref):
    o1, o2, o3 = SSD_INNER, SSD_INNER + CONV_CH, SSD_INNER + CONV_CH + 2 * SSD_HEADS
    o_ref[:, 0:CONV_CH] = w_ref[:, o1:o2]
    o_ref[:, CONV_CH:PA_W] = w_ref[:, 0:o1]
    tail = w_ref[:, o2:].astype(F32)
    o_ref[:, PA_W:PA_W + PB_W] = tail[:, o3 - o2:o3 - o2 + PB_W].astype(BF16)
    lane = lax.broadcasted_iota(jnp.int32, (1, DT_W), 1)
    o_ref[:, PA_W + PB_W:] = jnp.where(lane < o3 - o2, tail[:, 0:DT_W], 0.0).astype(BF16)


def _regroup_call(w_in_ab):
    tr = 128
    n_in = PA_W + PB_W + DT_W
    w_in_ab = jnp.pad(w_in_ab, ((0, 0), (0, 0), (0, n_in - w_in_ab.shape[-1]))).astype(BF16)
    return pl.pallas_call(
        _regroup_kernel,
        grid=(N_EVEN, D_MODEL // tr),
        in_specs=[pl.BlockSpec((None, tr, n_in), lambda e, i: (e, i, 0))],
        out_specs=pl.BlockSpec((None, tr, PA_W + PB_W + DT_W), lambda e, i: (e, i, 0)),
        out_shape=jax.ShapeDtypeStruct((N_EVEN, D_MODEL, PA_W + PB_W + DT_W), BF16),
        compiler_params=_params("parallel", "parallel"),
        name="regroup_w_in",
    )(w_in_ab)


def _inproj_kernel(*refs, n_x, tm, all_slots):
    x_refs = refs[:n_x]
    mod_ref, g_ref, w_ref, xbc_ref, z_ref, pb_ref, dt_ref, kn_ref, vn_ref = refs[n_x:]
    x = _tok_load(x_refs, tm)
    h = _rms(x) * g_ref[...] * (1.0 + mod_ref[1:2, :]) + mod_ref[0:1, :]
    hb = h.astype(BF16)
    tn = ATT_INNER
    for j in range(CONV_CH // tn):
        res = _dot(hb, w_ref[:, j * tn:(j + 1) * tn])
        for s in range(tn // LANE):
            xbc_ref[j * (tn // LANE) + s] = res[:, s * LANE:(s + 1) * LANE]
    z_ref[...] = _dot(hb, w_ref[:, CONV_CH:PA_W])
    q = _dot(hb, w_ref[:, PA_W:PA_W + tn])
    pb_ref[:, 0:tn] = q.astype(BF16)
    k = _dot(hb, w_ref[:, PA_W + tn:PA_W + 2 * tn])
    pb_ref[:, tn:2 * tn] = k.astype(BF16)
    v = _dot(hb, w_ref[:, PA_W + 2 * tn:PA_W + 3 * tn])
    pb_ref[:, 2 * tn:3 * tn] = v.astype(BF16)
    dt_ref[...] = _dot(hb, w_ref[:, PA_W + PB_W:PA_W + PB_W + DT_W])

    @pl.when(pl.program_id(0) < P_TOK // tm)
    def _():
        for new_ref, val in ((kn_ref, k), (vn_ref, v)):
            val = val.reshape(tm // SEQ, SEQ, ATT_INNER)
            if all_slots:
                new_ref[:, 0] = val
                for later in range(1, N_EVEN):
                    new_ref[:, later] = jnp.zeros_like(val)
            else:
                new_ref[...] = val


def _inproj_call(x, mod, l, e, gain, w_in, kv_prev):
    tm = 512
    row = _mod_row(tm)
    npt = P_TOK // tm
    n_x = len(x)
    args = [*x, mod, gain, w_in]
    in_specs = _tok_specs(x, tm) + [
        pl.BlockSpec((None, None, 6, D_MODEL), lambda i: (l, row(i), 0, 0)),
        pl.BlockSpec((1, D_MODEL), lambda i: (0, 0)),
        pl.BlockSpec((None, D_MODEL, PA_W + PB_W + DT_W), lambda i: (e, 0, 0),
                     pipeline_mode=pl.Buffered(1)),
    ]
    kern = functools.partial(_inproj_kernel, n_x=n_x, tm=tm, all_slots=kv_prev is None)
    if kv_prev is None:
        io_alias = {}
        kv_spec = pl.BlockSpec((tm // SEQ, N_EVEN, SEQ, ATT_INNER),
                               lambda i: (jnp.minimum(i, npt - 1), 0, 0, 0))
    else:
        io_alias = {n_x + 3: 4, n_x + 4: 5}
        args += list(kv_prev)
        in_specs += [pl.BlockSpec(memory_space=pl.ANY)] * 2
        kern = _drop_args(kern, n_x + 3, 2)
        kv_spec = pl.BlockSpec((tm // SEQ, None, SEQ, ATT_INNER),
                               lambda i: (jnp.minimum(i, npt - 1), e, 0, 0))
    kv_shape = jax.ShapeDtypeStruct((BATCH, N_EVEN, SEQ, ATT_INNER), F32)
    return pl.pallas_call(
        kern,
        grid=(N_TOK // tm,),
        in_specs=in_specs,
        out_specs=[
            pl.BlockSpec((N_SLAB, tm, LANE), lambda i: (0, i, 0)),
            pl.BlockSpec((tm, SSD_INNER), lambda i: (i, 0)),
            pl.BlockSpec((tm, PB_W), lambda i: (i, 0)),
            pl.BlockSpec((tm, DT_W), lambda i: (i, 0)),
            kv_spec, kv_spec,
        ],
        out_shape=[
            jax.ShapeDtypeStruct((N_SLAB, N_TOK, LANE), F32),
            jax.ShapeDtypeStruct((N_TOK, SSD_INNER), F32),
            jax.ShapeDtypeStruct((N_TOK, PB_W), BF16),
            jax.ShapeDtypeStruct((N_TOK, DT_W), F32),
            kv_shape, kv_shape,
        ],
        input_output_aliases=io_alias,
        compiler_params=_params("arbitrary", vmem=VMEM_LIMIT_INPROJ),
        name="inproj",
    )(*args)


def _ssd_kernel(*refs, L, has_h0, emit_state, all_slots):
    it = iter(refs)
    xbc_ref, z_ref, dt_ref = next(it), next(it), next(it)
    convw_ref, convb_ref, dtb_ref, alog_ref, dskip_ref, norm_ref = (next(it) for _ in range(6))
    sel_refs = (next(it), next(it))
    if has_h0:
        h0_refs = (next(it), next(it))
    y_ref = next(it)
    if emit_state:
        hout_refs = (next(it), next(it))
    (xact, xdt_f, xdt_b, bt, cact, tmp, yacc, dtv, state_f, state_b,
     acs_s, acst_s, cb_s) = (next(it) for _ in range(13))

    nc = L // CHUNK
    nq = L // SUBLANE
    half_w = CONV_W // 2

    rq = lax.broadcasted_iota(jnp.int32, (QBLK, LANE), 0)

    def phase(s, pe, q0):
        r0 = SUBLANE * q0 + pe
        if r0 < 0:
            a = xbc_ref[s, pl.ds(r0 + SUBLANE, QBLK, stride=SUBLANE), :]
            return jnp.where(rq == 0, 0.0, pltpu.roll(a, 1, 0))
        if r0 + SUBLANE * (QBLK - 1) >= L:
            a = xbc_ref[s, pl.ds(r0 - SUBLANE, QBLK, stride=SUBLANE), :]
            return jnp.where(rq == QBLK - 1, 0.0, pltpu.roll(a, QBLK - 1, 0))
        return xbc_ref[s, pl.ds(r0, QBLK, stride=SUBLANE), :]

    def conv_slab(s, dst, d_idx):
        ls = slice(s * LANE, (s + 1) * LANE)
        for q0 in range(0, nq, QBLK):
            ext = {pe: phase(s, pe, q0) for pe in range(-half_w, SUBLANE + half_w)}
            for p in range(SUBLANE):
                acc = jnp.broadcast_to(convb_ref[:, ls], (QBLK, LANE))
                for k in range(CONV_W):
                    acc = acc + convw_ref[k:k + 1, ls] * ext[p + k - half_w]
                dst[d_idx, pl.ds(SUBLANE * q0 + p, QBLK, stride=SUBLANE), :] = _silu(acc)

    for m in range(NH2):
        conv_slab(m, xact, m)
    for g in range(SSD_GROUPS):
        conv_slab(NH2 + g, tmp, g)
    for c in range(nc):
        for g in range(SSD_GROUPS):
            bt[c, g * D_STATE:(g + 1) * D_STATE, :] = (
                tmp[g, c * CHUNK:(c + 1) * CHUNK, :].T.astype(BF16))
    for g in range(SSD_GROUPS):
        conv_slab(NH2 + SSD_GROUPS + g, tmp, g)
    for c in range(nc):
        rows = slice(c * CHUNK, (c + 1) * CHUNK)
        for g in range(SSD_GROUPS):
            cact[g, rows, :] = tmp[g, rows, :].astype(BF16)

    dr = dt_ref[...] + dtb_ref[...]
    dtv[...] = jnp.maximum(dr, 0.0) + jnp.log1p(jnp.exp(-jnp.abs(dr)))
    a_row = -jnp.exp(alog_ref[...]) * math.log2(math.e)

    for c in range(nc):
        rows = slice(c * CHUNK, (c + 1) * CHUNK)
        d = dtv[rows, :]
        d_hi = d.astype(BF16)
        d_lo = (d - d_hi.astype(F32)).astype(BF16)
        d2 = jnp.concatenate([d_hi, d_lo], axis=1)
        spread_f = _dot(d2, sel_refs[0][...])
        spread_b = _dot(d2, sel_refs[1][...])
        for m in range(NH2):
            lanes = slice(m * LANE, (m + 1) * LANE)
            xm = xact[m, rows, :]
            xdt_f[m, rows, :] = (xm * spread_f[:, lanes]).astype(BF16)
            xdt_b[m, rows, :] = (xm * spread_b[:, lanes]).astype(BF16)

    ri = lax.broadcasted_iota(jnp.int32, (CHUNK, CHUNK), 0)
    ci = lax.broadcasted_iota(jnp.int32, (CHUNK, CHUNK), 1)
    low_half = ci < SSD_HD
    low_row = low_half[0:1, :]

    for c in range(nc):
        rows = slice(c * CHUNK, (c + 1) * CHUNK)
        dta = dtv[rows, :] * a_row
        run_f = jnp.dot((ri >= ci).astype(F32), dta, preferred_element_type=F32,
                        precision=lax.Precision.HIGHEST)
        run_b = jnp.dot((ri <= ci).astype(F32), dta, preferred_element_type=F32,
                        precision=lax.Precision.HIGHEST)
        acs = jnp.where(ci < SSD_HEADS, run_f, run_b)
        acs_s[c] = acs
        acst_s[c] = acs.T
        for g in range(SSD_GROUPS):
            cb_s[c, g] = _dot(cact[g, rows, :], bt[c, g * D_STATE:(g + 1) * D_STATE, :])

    for d_idx, st_ref in enumerate((state_f, state_b)):
        for m in range(NH2):
            if has_h0:
                st_ref[m] = h0_refs[d_idx][m].T
            else:
                st_ref[m] = jnp.zeros((D_STATE, LANE), F32)

    def chunk_rows(c):
        return pl.ds(pl.multiple_of(c * CHUNK, CHUNK), CHUNK)

    def chunk_step(c, fwd, add):
        mask = (ri >= ci) if fwd else (ri <= ci)
        col0 = 0 if fwd else SSD_HEADS
        last = CHUNK - 1 if fwd else 0
        xdt = xdt_f if fwd else xdt_b
        state = state_f if fwd else state_b
        rows = chunk_rows(c)
        acs = acs_s[c]
        acs_t = acst_s[c]
        acs_last = acs[last:last + 1, :]
        for g in range(SSD_GROUPS):
            btg = bt[c, g * D_STATE:(g + 1) * D_STATE, :]
            cg = cact[g, rows, :]
            cbm = jnp.where(mask, cb_s[c, g], 0.0)
            for pr in range(2):
                m = g * 2 + pr
                ha, hb = col0 + 2 * m, col0 + 2 * m + 1
                lanes = slice(m * LANE, (m + 1) * LANE)
                xd2 = xdt[m, rows, :]
                acol_a = jnp.broadcast_to(acs[:, ha:ha + 1], (CHUNK, CHUNK))
                acol_b = jnp.broadcast_to(acs[:, hb:hb + 1], (CHUNK, CHUNK))
                w_a = cbm * jnp.exp2(jnp.minimum(acol_a - acs_t[ha:ha + 1, :], 0.0))
                w_b = cbm * jnp.exp2(jnp.minimum(acol_b - acs_t[hb:hb + 1, :], 0.0))
                w2 = jnp.concatenate([w_a.astype(BF16), w_b.astype(BF16)], axis=1)
                zero = jnp.zeros_like(xd2)
                x_blk = jnp.concatenate([jnp.where(low_half, xd2, zero),
                                         jnp.where(low_half, zero, xd2)], axis=0)
                y_in = _dot(w2, x_blk)
                acol2 = jnp.where(low_half, acol_a, acol_b)
                last2 = jnp.where(low_row, acs_last[:, ha:ha + 1], acs_last[:, hb:hb + 1])
                h_prev = state[m]
                y_out = _dot(cg, h_prev.astype(BF16)) * jnp.exp2(acol2)
                to_end = jnp.exp2(last2 - acol2)
                st = _dot(btg, (to_end * xd2.astype(F32)).astype(BF16))
                state[m] = jnp.exp2(last2) * h_prev + st
                y2 = y_in + y_out
                if fwd:
                    y2 = y2 + dskip_ref[:, lanes] * xact[m, rows, :]
                if add:
                    y2 = y2 + yacc[rows, lanes]
                yacc[rows, lanes] = y2

    def finish(c):
        rows = chunk_rows(c)
        yt = yacc[rows, :] * _silu(z_ref[rows, :])
        y_ref[rows, :] = (_rms(yt) * norm_ref[...]).astype(y_ref.dtype)

    def first_half(s, carry):
        chunk_step(s, True, False)
        chunk_step(nc - 1 - s, False, False)
        return carry

    def second_half(s, carry):
        chunk_step(s, True, True)
        chunk_step(nc - 1 - s, False, True)
        finish(s)
        finish(nc - 1 - s)
        return carry

    lax.fori_loop(0, nc // 2, first_half, 0)
    lax.fori_loop(nc // 2, nc, second_half, 0)

    if emit_state:
        for d_idx, st_ref in enumerate((state_f, state_b)):
            for m in range(NH2):
                if all_slots:
                    hout_refs[d_idx][0, m] = st_ref[m].T
                    for later in range(1, N_EVEN):
                        hout_refs[d_idx][later, m] = jnp.zeros((LANE, D_STATE), F32)
                else:
                    hout_refs[d_idx][m] = st_ref[m].T


def _head_spread_tables():
    ch_head = np.arange(SSD_INNER) // SSD_HD
    out = []
    for d in range(2):
        k = np.arange(2 * DT_W) % DT_W
        out.append(jnp.asarray(k[:, None] == (d * SSD_HEADS + ch_head)[None, :], BF16))
    return out


def _ssd_parts(xbc, z, dtarr, states_prev, weights, e, *, L, nb, blk0, h0=None):
    emit_state = states_prev is not None
    all_slots = isinstance(states_prev, str)
    nblk = N_TOK // L
    nc = L // CHUNK
    vec = lambda w: pl.BlockSpec((1, w), lambda b: (0, 0))
    in_specs = [
        pl.BlockSpec((N_SLAB, L, LANE), lambda b: (0, blk0 + b, 0)),
        pl.BlockSpec((None, L, SSD_INNER), lambda b: (blk0 + b, 0, 0)),
        pl.BlockSpec((None, L, DT_W), lambda b: (blk0 + b, 0, 0)),
        pl.BlockSpec((8, CONV_CH), lambda b: (0, 0)),
        vec(CONV_CH), vec(DT_W), vec(DT_W), vec(SSD_INNER), vec(SSD_INNER),
        pl.BlockSpec((2 * DT_W, SSD_INNER), lambda b: (0, 0)),
        pl.BlockSpec((2 * DT_W, SSD_INNER), lambda b: (0, 0)),
    ]
    args = [xbc, z.reshape(nblk, L, SSD_INNER), dtarr.reshape(nblk, L, DT_W), *weights,
            *_head_spread_tables()]
    state_spec = pl.BlockSpec((None, None, NH2, LANE, D_STATE), lambda b: (b, e, 0, 0, 0))
    if h0 is not None:
        for h in h0:
            args.append(h.reshape(DEC_BATCH, N_EVEN, NH2, LANE, D_STATE))
            in_specs.append(state_spec)
    kern = functools.partial(_ssd_kernel, L=L, has_h0=h0 is not None, emit_state=emit_state,
                             all_slots=all_slots)
    out_specs = [pl.BlockSpec((None, L, SSD_INNER), lambda b: (b, 0, 0))]
    out_shape = [jax.ShapeDtypeStruct((nb, L, SSD_INNER), BF16)]
    io_alias = {}
    if emit_state:
        if all_slots:
            out_spec = pl.BlockSpec((None, N_EVEN, NH2, LANE, D_STATE), lambda b: (b, 0, 0, 0, 0))
        else:
            out_spec = state_spec
            io_alias = {len(args): 1, len(args) + 1: 2}
            kern = _drop_args(kern, len(args), 2)
            args += list(states_prev)
            in_specs += [pl.BlockSpec(memory_space=pl.ANY)] * 2
        out_specs += [out_spec, out_spec]
        out_shape += [jax.ShapeDtypeStruct((nb, N_EVEN, NH2, LANE, D_STATE), F32)] * 2
    scratch = [
        pltpu.VMEM((NH2, L, LANE), F32),
        pltpu.VMEM((NH2, L, LANE), BF16),
        pltpu.VMEM((NH2, L, LANE), BF16),
        pltpu.VMEM((nc, SSD_GROUPS * D_STATE, CHUNK), BF16),
        pltpu.VMEM((SSD_GROUPS, L, D_STATE), BF16),
        pltpu.VMEM((SSD_GROUPS, L, D_STATE), F32),
        pltpu.VMEM((L, SSD_INNER), F32),
        pltpu.VMEM((L, DT_W), F32),
        pltpu.VMEM((NH2, D_STATE, LANE), F32),
        pltpu.VMEM((NH2, D_STATE, LANE), F32),
        pltpu.VMEM((nc, CHUNK, LANE), F32),
        pltpu.VMEM((nc, LANE, CHUNK), F32),
        pltpu.VMEM((nc, SSD_GROUPS, CHUNK, CHUNK), F32),
    ]
    return dict(kernel=kern, args=args, in_specs=in_specs, out_specs=out_specs,
                out_shape=out_shape, scratch=scratch, alias=io_alias)


def _rope(x, cos, sin_a, sin_b):
    return (x * cos + pltpu.roll(x, LANE - ROPE_PAIRS, 1) * sin_a
            + pltpu.roll(x, ROPE_PAIRS, 1) * sin_b)


def _attn_kernel(*refs, L, n_ctx, lam_init, stack_maps, q_axis):
    it = iter(refs)
    q_ref, k_ref, v_ref = next(it), next(it), next(it)
    if n_ctx:
        ck_ref, cv_ref = next(it), next(it)
        cq, saq, sbq, ck, sak, sbk = (next(it) for _ in range(6))
    lam_ref, sub_ref = next(it), next(it)
    o_ref = next(it)
    kall, vext = next(it), next(it)
    lk = n_ctx + L
    tq = q_ref.shape[0]

    def prepare_keys_values():
        for h in range(ATT_HEADS):
            hs = slice(h * ATT_VD, (h + 1) * ATT_VD)
            if n_ctx:
                kall[h, 0:n_ctx, :] = ck_ref[:, hs].astype(BF16)
                vext[h, 0:n_ctx, 0:ATT_VD] = cv_ref[:, hs].astype(BF16)
                kh = _rope(k_ref[:, hs].astype(F32), ck[...], sak[...], sbk[...])
                kall[h, n_ctx:lk, :] = kh.astype(BF16)
            else:
                kall[h] = k_ref[:, hs]
            vext[h, n_ctx:lk, 0:ATT_VD] = v_ref[:, hs]
            vext[h, :, ATT_VD:MXU_N] = jnp.ones((lk, MXU_N - ATT_VD), BF16)

    if q_axis is None:
        prepare_keys_values()
    else:
        pl.when(pl.program_id(q_axis) == 0)(prepare_keys_values)

    lane = lax.broadcasted_iota(jnp.int32, (1, ATT_VD), 1)
    scale = ATT_HD ** -0.5 * math.log2(math.e)
    m1 = jnp.where(lane < ATT_HD, scale, 0.0)
    m2 = jnp.where(lane < ATT_HD, 0.0, scale)
    lq = lam_ref[...]
    lam = (jnp.exp(jnp.sum(lq[0:1, :] * lq[1:2, :], axis=-1, keepdims=True))
           - jnp.exp(jnp.sum(lq[2:3, :] * lq[3:4, :], axis=-1, keepdims=True)) + lam_init)

    for h in range(ATT_HEADS):
        hs = slice(h * ATT_VD, (h + 1) * ATT_VD)
        q = q_ref[:, hs].astype(F32)
        if n_ctx:
            q = _rope(q, cq[...], saq[...], sbq[...])
        def attend(qm):
            s = _dot_nt(qm, kall[h])
            p = jnp.exp2(s - jnp.max(s, axis=-1, keepdims=True)).astype(BF16)
            pv = _dot(p, vext[h])
            return pv[:, 0:ATT_VD] / pv[:, ATT_VD:MXU_N]

        q1, q2 = (q * m1).astype(BF16), (q * m2).astype(BF16)
        if stack_maps:
            on = attend(jnp.concatenate([q1, q2], axis=0))
            o = on[0:tq] - lam * on[tq:2 * tq]
        else:
            o = attend(q1) - lam * attend(q2)
        o_ref[:, hs] = (_rms(o) * sub_ref[...] * (1.0 - lam_init)).astype(o_ref.dtype)


def _attn_parts(pb, lam_qk, subln, lam_init, *, L, nb, blk0, tq, ctx=None, rope=None):
    nblk = N_TOK // L
    one_d = tq == L
    spec = lambda shape, f: pl.BlockSpec(shape, (lambda b: f(b, 0)) if one_d else f)
    pb3 = pb.reshape(nblk, L, PB_W)
    in_specs = [
        spec((None, tq, ATT_INNER), lambda b, i: (blk0 + b, i, 0)),
        spec((None, L, ATT_INNER), lambda b, i: (blk0 + b, 0, 1)),
        spec((None, L, ATT_INNER), lambda b, i: (blk0 + b, 0, 2)),
    ]
    args = [pb3, pb3, pb3]
    n_ctx = 0
    if ctx is not None:
        cache_k, cache_v, e = ctx
        n_ctx = cache_k.shape[2]
        for cch in (cache_k, cache_v):
            args.append(cch.reshape(DEC_BATCH, N_EVEN, n_ctx, ATT_INNER))
            in_specs.append(spec((None, None, n_ctx, ATT_INNER), lambda b, i: (b, e, 0, 0)))
        for t in rope:
            args.append(t)
            in_specs.append(spec((tq, ATT_VD), lambda b, i: (i, 0)))
        for t in rope:
            args.append(t)
            in_specs.append(spec((L, ATT_VD), lambda b, i: (0, 0)))
    args += [lam_qk, subln]
    in_specs += [spec((4, ATT_HD), lambda b, i: (0, 0)), spec((1, ATT_VD), lambda b, i: (0, 0))]
    kern = functools.partial(_attn_kernel, L=L, n_ctx=n_ctx, lam_init=lam_init,
                             stack_maps=n_ctx + L <= 2 * MXU_N,
                             q_axis=None if one_d else 1)
    return dict(kernel=kern, args=args, in_specs=in_specs,
                out_specs=[spec((None, tq, ATT_INNER), lambda b, i: (b, i, 0))],
                out_shape=[jax.ShapeDtypeStruct((nb, L, ATT_INNER), BF16)],
                scratch=[pltpu.VMEM((ATT_HEADS, n_ctx + L, ATT_VD), BF16),
                         pltpu.VMEM((ATT_HEADS, n_ctx + L, MXU_N), BF16)],
                alias={})


def _outproj_kernel(*refs, n_x, tm):
    x_refs = refs[:n_x]
    yp_ref, ys_ref, op_ref, os_ref, mod_ref, w_ref, out_ref = refs[n_x:]
    y = _tok_load((yp_ref, ys_ref), tm)
    o = _tok_load((op_ref, os_ref), tm)
    acc = _dot(y, w_ref[0:SSD_INNER, :]) + _dot(o, w_ref[SSD_INNER:, :])
    out_ref[...] = _tok_load(x_refs, tm) + mod_ref[2:3, :] * acc


def _outproj_call(x, y, o, mod, l, e, w):
    tm = 512
    row = _mod_row(tm)
    return pl.pallas_call(
        functools.partial(_outproj_kernel, n_x=len(x), tm=tm),
        grid=(N_TOK // tm,),
        in_specs=_tok_specs(x, tm) + _tok_specs(y, tm) + _tok_specs(o, tm) + [
            pl.BlockSpec((None, None, 6, D_MODEL), lambda i: (l, row(i), 0, 0)),
            pl.BlockSpec((None, SSD_INNER + ATT_INNER, D_MODEL), lambda i: (e, 0, 0))],
        out_specs=pl.BlockSpec((tm, D_MODEL), lambda i: (i, 0)),
        out_shape=jax.ShapeDtypeStruct((N_TOK, D_MODEL), F32),
        compiler_params=_params("arbitrary"),
        name="outproj",
    )(*x, *y, *o, mod, w)


def _ffn_kernel(*refs, n_x, tm, final):
    x_refs = refs[:n_x]
    if final:
        mod_ref, g_ref, win_ref, wout_ref, gf_ref, op_ref, os_ref, h_ref, a_ref = refs[n_x:]
    else:
        mod_ref, g_ref, win_ref, wout_ref, out_ref, h_ref, a_ref = refs[n_x:]
    x = _tok_load(x_refs, tm)
    h = _rms(x) * g_ref[...] * (1.0 + mod_ref[4:5, :]) + mod_ref[3:4, :]
    h_ref[...] = h.astype(BF16)
    for j in range(FFN_HIDDEN // MXU_N):
        cs = slice(j * MXU_N, (j + 1) * MXU_N)
        us = slice(FFN_HIDDEN + j * MXU_N, FFN_HIDDEN + (j + 1) * MXU_N)
        hb = h_ref[...]
        a_ref[:, cs] = (_silu(_dot(hb, win_ref[:, cs])) * _dot(hb, win_ref[:, us])).astype(BF16)
    x_new = _tok_load(x_refs, tm) + mod_ref[5:6, :] * _dot(a_ref[...], wout_ref[...])
    if final:
        y = _rms(x_new) * gf_ref[...]

        @pl.when(pl.program_id(0) < P_TOK // tm)
        def _():
            op_ref[...] = y

        @pl.when(pl.program_id(0) >= P_TOK // tm)
        def _():
            os_ref[...] = y
    else:
        out_ref[...] = x_new


def _ffn_call(x, mod, l, gain, w_in, w_out, final_gain=None):
    tm = 1024
    row = _mod_row(tm)
    npt = P_TOK // tm
    once = pl.Buffered(1)
    final = final_gain is not None
    args = [*x, mod, gain, w_in, w_out]
    in_specs = _tok_specs(x, tm) + [
        pl.BlockSpec((None, None, 6, D_MODEL), lambda i: (l, row(i), 0, 0)),
        pl.BlockSpec((1, D_MODEL), lambda i: (0, 0)),
        pl.BlockSpec((None, D_MODEL, 2 * FFN_HIDDEN), lambda i: (l, 0, 0), pipeline_mode=once),
        pl.BlockSpec((None, FFN_HIDDEN, D_MODEL), lambda i: (l, 0, 0), pipeline_mode=once),
    ]
    if final:
        args.append(final_gain)
        in_specs.append(pl.BlockSpec((1, D_MODEL), lambda i: (0, 0)))
        out_specs = [pl.BlockSpec((tm, D_MODEL), lambda i: (jnp.minimum(i, npt - 1), 0)),
                     pl.BlockSpec((tm, D_MODEL), lambda i: (jnp.maximum(i - npt, 0), 0))]
        out_shape = [jax.ShapeDtypeStruct((P_TOK, D_MODEL), F32),
                     jax.ShapeDtypeStruct((S_TOK, D_MODEL), F32)]
    else:
        out_specs = pl.BlockSpec((tm, D_MODEL), lambda i: (i, 0))
        out_shape = jax.ShapeDtypeStruct((N_TOK, D_MODEL), F32)
    return pl.pallas_call(
        functools.partial(_ffn_kernel, n_x=len(x), tm=tm, final=final),
        grid=(N_TOK // tm,),
        in_specs=in_specs,
        out_specs=out_specs,
        out_shape=out_shape,
        scratch_shapes=[pltpu.VMEM((tm, D_MODEL), BF16), pltpu.VMEM((tm, FFN_HIDDEN), BF16)],
        compiler_params=_params("arbitrary"),
        name="ffn",
    )(*args)


def _four_kernel(x_ref, mod_ref, g_ref, cc_ref, sc_ref, cs_ref, w_ref, b_ref, out_ref, stk_ref, *, L):
    x = x_ref[...]
    h = (_rms(x) * g_ref[...] * (1.0 + mod_ref[1:2, :]) + mod_ref[0:1, :]).astype(BF16)
    for g in range(FOUR_GROUPS):
        ls = slice(g * FOUR_GC, (g + 1) * FOUR_GC)
        hg = h[:, ls]
        stk_ref[0:L, ls] = _dot(hg, cc_ref[...]).astype(BF16)
        stk_ref[L:2 * L, ls] = _dot(hg, sc_ref[...]).astype(BF16)
    f = _dot(cs_ref[...], stk_ref[...]).astype(BF16)
    out_ref[...] = x + mod_ref[2:3, :] * (_dot(f, w_ref[...]) + b_ref[...])


def _dft_tables(L):
    n = FOUR_GC
    ang_c = 2.0 * np.pi * ((np.arange(n)[:, None] * np.arange(n)[None, :]) % n) / n
    ang_l = 2.0 * np.pi * ((np.arange(L)[:, None] * np.arange(L)[None, :]) % L) / L
    cc = np.cos(ang_c) / math.sqrt(n)
    sc = np.sin(ang_c) / math.sqrt(n)
    cs = np.concatenate([np.cos(ang_l), -np.sin(ang_l)], axis=1) / math.sqrt(L)
    return (jnp.asarray(cc, F32).astype(BF16), jnp.asarray(sc, F32).astype(BF16),
            jnp.asarray(cs, F32).astype(BF16))


def _four_call(x, mod, l, i_odd, gain, w, b, *, L, nb, blk0):
    nblk = N_TOK // L
    cc, sc, cs = _dft_tables(L)
    row = _mod_row(L)
    out = pl.pallas_call(
        functools.partial(_four_kernel, L=L),
        grid=(nb,),
        in_specs=[
            pl.BlockSpec((None, L, D_MODEL), lambda i: (blk0 + i, 0, 0)),
            pl.BlockSpec((None, None, 6, D_MODEL), lambda i: (l, row(blk0 + i), 0, 0)),
            pl.BlockSpec((1, D_MODEL), lambda i: (0, 0)),
            pl.BlockSpec((FOUR_GC, FOUR_GC), lambda i: (0, 0)),
            pl.BlockSpec((FOUR_GC, FOUR_GC), lambda i: (0, 0)),
            pl.BlockSpec((L, 2 * L), lambda i: (0, 0)),
            pl.BlockSpec((None, D_MODEL, D_MODEL), lambda i: (i_odd, 0, 0)),
            pl.BlockSpec((1, D_MODEL), lambda i: (0, 0)),
        ],
        out_specs=pl.BlockSpec((None, L, D_MODEL), lambda i: (i, 0, 0)),
        out_shape=jax.ShapeDtypeStruct((nb, L, D_MODEL), F32),
        scratch_shapes=[pltpu.VMEM((2 * L, D_MODEL), BF16)],
        compiler_params=_params("parallel"),
        name=f"fourier_L{L}",
    )(x.reshape(nblk, L, D_MODEL), mod, gain, cc, sc, cs, w, b)
    return out.reshape(nb * L, D_MODEL)


def _rope_lane_tables(L):
    t = jnp.arange(L)
    row = (t // GRID_W).astype(F32)
    col = (t % GRID_W).astype(F32)
    freq = ROPE_THETA ** (-jnp.arange(ROPE_PAIRS, dtype=F32) / ROPE_PAIRS)
    ar = row[:, None] * freq[None]
    ac = col[:, None] * freq[None]
    z = jnp.zeros_like(ar)
    cos32 = lambda a: jnp.concatenate([jnp.cos(a), jnp.cos(a)], axis=-1)
    sa32 = lambda a: jnp.concatenate([-jnp.sin(a), z], axis=-1)
    sb32 = lambda a: jnp.concatenate([z, jnp.sin(a)], axis=-1)
    tile = lambda f: jnp.concatenate([f(ar), f(ac), f(ar), f(ac)], axis=-1)
    return tile(cos32), tile(sa32), tile(sb32)


def _pad_lanes(v, width):
    v = v.reshape(1, -1).astype(F32)
    return jnp.pad(v, ((0, 0), (0, width - v.shape[1])))


def kernel(x_prompt, x_sample, cache_k, cache_v, state_ssd_fwd, state_ssd_bwd, c, c_ctx, w_ada, b_ada, norm_mix, norm_ffn, w_in_ab, conv_w, conv_b, dt_bias, a_log, d_skip, ssd_norm, lambda_qk, subln, w_out_ab, w_four, b_four, w_ffn_in, w_ffn_out, norm_final):
    x = (x_prompt.reshape(P_TOK, D_MODEL), x_sample.reshape(S_TOK, D_MODEL))
    cvec = jnp.concatenate([c_ctx[None, :], c, jnp.zeros((N_MOD - 1 - DEC_BATCH, D_MODEL), F32)], axis=0)
    mod = _ada_call(cvec, w_ada, b_ada).reshape(DEPTH, N_MOD, 6, D_MODEL)
    rope = _rope_lane_tables(DEC_SEQ)

    w_in_bf = _regroup_call(w_in_ab)
    w_out_bf = w_out_ab.astype(BF16)
    w_four_bf = w_four.astype(BF16)
    w_ffn_in_bf = w_ffn_in.astype(BF16)
    w_ffn_out_bf = w_ffn_out.astype(BF16)

    kv_new = None
    states = "new"
    blk_s = P_TOK // DEC_SEQ
    for l in range(DEPTH):
        gain_mix = norm_mix[l].reshape(1, D_MODEL)
        if l % 2 == 0:
            e = l // 2
            lam_init = 0.8 - 0.6 * math.exp(-0.3 * l)
            xbc, z, pb, dtarr, k_new, v_new = _inproj_call(x, mod, l, e, gain_mix, w_in_bf, kv_new)
            kv_new = (k_new, v_new)

            ssd_w = (jnp.pad(conv_w[e], ((0, 8 - CONV_W), (0, 0))), conv_b[e].reshape(1, CONV_CH),
                     _pad_lanes(dt_bias[e], DT_W), _pad_lanes(a_log[e], DT_W),
                     jnp.repeat(d_skip[e], SSD_HD).reshape(1, SSD_INNER),
                     ssd_norm[e].reshape(1, SSD_INNER))
            sub = subln[e].reshape(1, ATT_VD)
            ((y_p, hf, hb),) = _launch(
                [_ssd_parts(xbc, z, dtarr, states, ssd_w, e, L=SEQ, nb=BATCH, blk0=0)],
                (BATCH,), ("parallel",), "ssd_L256")
            states = (hf, hb)
            ((o_p,),) = _launch(
                [_attn_parts(pb, lambda_qk[e], sub, lam_init, L=SEQ, nb=BATCH, blk0=0, tq=SEQ)],
                (BATCH,), ("parallel",), "attn_L256")
            ((y_s,),) = _launch(
                [_ssd_parts(xbc, z, dtarr, None, ssd_w, e, L=DEC_SEQ, nb=DEC_BATCH, blk0=blk_s,
                            h0=(state_ssd_fwd, state_ssd_bwd))],
                (DEC_BATCH,), ("parallel",), "ssd_L1024")
            ((o_s,),) = _launch(
                [_attn_parts(pb, lambda_qk[e], sub, lam_init, L=DEC_SEQ, nb=DEC_BATCH, blk0=blk_s,
                             tq=256, ctx=(cache_k, cache_v, e), rope=rope)],
                (DEC_BATCH, DEC_SEQ // 256), ("parallel", "arbitrary"), "attn_L1024")
            halves = lambda a, b: (a.reshape(P_TOK, -1), b.reshape(S_TOK, -1))
            x = (_outproj_call(x, halves(y_p, y_s), halves(o_p, o_s), mod, l, e, w_out_bf),)
        else:
            i_odd = l // 2
            b4 = b_four[i_odd].reshape(1, D_MODEL)
            x = (_four_call(x[0], mod, l, i_odd, gain_mix, w_four_bf, b4, L=SEQ, nb=BATCH, blk0=0),
                 _four_call(x[0], mod, l, i_odd, gain_mix, w_four_bf, b4, L=DEC_SEQ, nb=DEC_BATCH,
                            blk0=blk_s))
        last = l == DEPTH - 1
        x = _ffn_call(x, mod, l, norm_ffn[l].reshape(1, D_MODEL), w_ffn_in_bf, w_ffn_out_bf,
                      final_gain=norm_final.reshape(1, D_MODEL) if last else None)
        x = tuple(x) if last else (x,)

    y_p, y_s = x
    return (y_p.reshape(BATCH, SEQ, D_MODEL), y_s.reshape(DEC_BATCH, DEC_SEQ, D_MODEL),
            kv_new[0].reshape(BATCH, N_EVEN, SEQ, ATT_HEADS, 2, ATT_HD),
            kv_new[1].reshape(BATCH, N_EVEN, SEQ, ATT_HEADS, ATT_VD),
            states[0].reshape(BATCH, N_EVEN, SSD_HEADS, SSD_HD, D_STATE),
            states[1].reshape(BATCH, N_EVEN, SSD_HEADS, SSD_HD, D_STATE))
```

```python
import functools
import math

import numpy as np
import jax
import jax.numpy as jnp
from jax import lax
from jax.experimental import pallas as pl
from jax.experimental.pallas import tpu as pltpu

D_MODEL = 1024
BATCH = 16
SEQ = 256
DEPTH = 4
N_EVEN = 2
DEC_BATCH = 4
DEC_SEQ = 1024
PAST_LEN = 512
GRID_W = 64
SSD_HEADS = 16
SSD_HD = 64
SSD_INNER = 1024
SSD_GROUPS = 4
D_STATE = 128
CONV_W = 5
CONV_CH = 2048
CHUNK = 128
ATT_HEADS = 8
ATT_HD = 64
ATT_VD = 128
ATT_INNER = 1024
ROPE_THETA = 10000.0
ROPE_PAIRS = 16
FOUR_GROUPS = 4
FOUR_GC = D_MODEL // FOUR_GROUPS
FFN_HIDDEN = 2816
EPS = 1e-6

P_TOK = BATCH * SEQ
S_TOK = DEC_BATCH * DEC_SEQ
N_TOK = P_TOK + S_TOK
N_MOD = 8
PA_W = CONV_CH + SSD_INNER
PB_W = 3 * ATT_INNER
DT_W = 128
NH2 = SSD_HEADS // 2
LANE = 128
SUBLANE = 8
N_SLAB = CONV_CH // LANE
QBLK = 32
MXU_N = 256
VMEM_LIMIT = 56 * 1024 * 1024
VMEM_LIMIT_INPROJ = 61 * 1024 * 1024

F32 = jnp.float32
BF16 = jnp.bfloat16


def _params(*sem, vmem=VMEM_LIMIT):
    return pltpu.CompilerParams(dimension_semantics=sem, vmem_limit_bytes=vmem)


def _dot(a, b):
    return jnp.dot(a, b, preferred_element_type=F32)


def _dot_nt(a, b):
    return lax.dot_general(a, b, (((1,), (1,)), ((), ())), preferred_element_type=F32)


def _silu(x):
    h = 0.5 * x
    return h + h * jnp.tanh(h)


def _rms(x):
    return x * lax.rsqrt(jnp.mean(x * x, axis=-1, keepdims=True) + EPS)


def _mod_row(tm):
    return lambda i: jnp.maximum((i * tm) // DEC_SEQ - (P_TOK // DEC_SEQ - 1), 0)


def _drop_args(kern, start, n):
    def wrapped(*refs):
        return kern(*(refs[:start] + refs[start + n:]))
    return wrapped


def _tok_specs(parts, tm):
    width = parts[0].shape[-1]
    if len(parts) == 1:
        return [pl.BlockSpec((tm, width), lambda i: (i, 0))]
    npt = P_TOK // tm
    return [pl.BlockSpec((tm, width), lambda i: (jnp.minimum(i, npt - 1), 0)),
            pl.BlockSpec((tm, width), lambda i: (jnp.maximum(i - npt, 0), 0))]


def _tok_load(refs, tm):
    if len(refs) == 1:
        return refs[0][...]
    return jnp.where(pl.program_id(0) < P_TOK // tm, refs[0][...], refs[1][...])


def _launch(parts, grid, sem, name):
    n_in = [len(p["args"]) for p in parts]
    n_out = [len(p["out_shape"]) for p in parts]
    n_scr = [len(p["scratch"]) for p in parts]

    def kern(*refs):
        ins = refs[:sum(n_in)]
        outs = refs[sum(n_in):sum(n_in) + sum(n_out)]
        scr = refs[sum(n_in) + sum(n_out):]
        i0 = o0 = s0 = 0
        for p, ni, no, ns in zip(parts, n_in, n_out, n_scr):
            p["kernel"](*ins[i0:i0 + ni], *outs[o0:o0 + no], *scr[s0:s0 + ns])
            i0, o0, s0 = i0 + ni, o0 + no, s0 + ns

    alias = {}
    i0 = o0 = 0
    for p, ni, no in zip(parts, n_in, n_out):
        alias.update({i0 + k: o0 + v for k, v in p["alias"].items()})
        i0, o0 = i0 + ni, o0 + no
    cat = lambda key: [v for p in parts for v in p[key]]
    outs = pl.pallas_call(
        kern, grid=grid, in_specs=cat("in_specs"), out_specs=cat("out_specs"),
        out_shape=cat("out_shape"), scratch_shapes=cat("scratch"),
        input_output_aliases=alias, compiler_params=_params(*sem), name=name,
    )(*cat("args"))
    res, o0 = [], 0
    for no in n_out:
        res.append(list(outs[o0:o0 + no]))
        o0 += no
    return res


def _ada_kernel(c_ref, w_ref, b_ref, o_ref):
    s = _silu(c_ref[...]).astype(BF16)
    o_ref[...] = _dot(s, w_ref[...].astype(BF16)) + b_ref[...]


def _ada_call(cvec, w_ada, b_ada):
    tn = 1024
    return pl.pallas_call(
        _ada_kernel,
        grid=(DEPTH, 6 * D_MODEL // tn),
        in_specs=[
            pl.BlockSpec((N_MOD, D_MODEL), lambda l, j: (0, 0)),
            pl.BlockSpec((None, D_MODEL, tn), lambda l, j: (l, 0, j)),
            pl.BlockSpec((None, 1, tn), lambda l, j: (l, 0, j)),
        ],
        out_specs=pl.BlockSpec((None, N_MOD, tn), lambda l, j: (l, 0, j)),
        out_shape=jax.ShapeDtypeStruct((DEPTH, N_MOD, 6 * D_MODEL), F32),
        compiler_params=_params("parallel", "parallel"),
        name="adaln",
    )(cvec, w_ada, b_ada.reshape(DEPTH, 1, 6 * D_MODEL))


def _regroup_kernel(w_ref, o_ref):
    o1, o2, o3 = SSD_INNER, SSD_INNER + CONV_CH, SSD_INNER + CONV_CH + 2 * SSD_HEADS
    o_ref[:, 0:CONV_CH] = w_ref[:, o1:o2]
    o_ref[:, CONV_CH:PA_W] = w_ref[:, 0:o1]
    tail = w_ref[:, o2:].astype(F32)
    o_ref[:, PA_W:PA_W + PB_W] = tail[:, o3 - o2:o3 - o2 + PB_W].astype(BF16)
    lane = lax.broadcasted_iota(jnp.int32, (1, DT_W), 1)
    o_ref[:, PA_W + PB_W:] = jnp.where(lane < o3 - o2, tail[:, 0:DT_W], 0.0).astype(BF16)


def _regroup_call(w_in_ab):
    tr = 128
    n_in = PA_W + PB_W + DT_W
    w_in_ab = jnp.pad(w_in_ab, ((0, 0), (0, 0), (0, n_in - w_in_ab.shape[-1]))).astype(BF16)
    return pl.pallas_call(
        _regroup_kernel,
        grid=(N_EVEN, D_MODEL // tr),
        in_specs=[pl.BlockSpec((None, tr, n_in), lambda e, i: (e, i, 0))],
        out_specs=pl.BlockSpec((None, tr, PA_W + PB_W + DT_W), lambda e, i: (e, i, 0)),
        out_shape=jax.ShapeDtypeStruct((N_EVEN, D_MODEL, PA_W + PB_W + DT_W), BF16),
        compiler_params=_params("parallel", "parallel"),
        name="regroup_w_in",
    )(w_in_ab)


def _inproj_kernel(*refs, n_x, tm, all_slots):
    x_refs = refs[:n_x]
    mod_ref, g_ref, w_ref, xbc_ref, z_ref, pb_ref, dt_ref, kn_ref, vn_ref = refs[n_x:]
    x = _tok_load(x_refs, tm)
    h = _rms(x) * g_ref[...] * (1.0 + mod_ref[1:2, :]) + mod_ref[0:1, :]
    hb = h.astype(BF16)
    tn = ATT_INNER
    for j in range(CONV_CH // tn):
        res = _dot(hb, w_ref[:, j * tn:(j + 1) * tn])
        for s in range(tn // LANE):
            xbc_ref[j * (tn // LANE) + s] = res[:, s * LANE:(s + 1) * LANE]
    z_ref[...] = _dot(hb, w_ref[:, CONV_CH:PA_W])
    q = _dot(hb, w_ref[:, PA_W:PA_W + tn])
    pb_ref[:, 0:tn] = q.astype(BF16)
    k = _dot(hb, w_ref[:, PA_W + tn:PA_W + 2 * tn])
    pb_ref[:, tn:2 * tn] = k.astype(BF16)
    v = _dot(hb, w_ref[:, PA_W + 2 * tn:PA_W + 3 * tn])
    pb_ref[:, 2 * tn:3 * tn] = v.astype(BF16)
    dt_ref[...] = _dot(hb, w_ref[:, PA_W + PB_W:PA_W + PB_W + DT_W])

    @pl.when(pl.program_id(0) < P_TOK // tm)
    def _():
        for new_ref, val in ((kn_ref, k), (vn_ref, v)):
            val = val.reshape(tm // SEQ, SEQ, ATT_INNER)
            if all_slots:
                new_ref[:, 0] = val
                for later in range(1, N_EVEN):
                    new_ref[:, later] = jnp.zeros_like(val)
            else:
                new_ref[...] = val


def _inproj_call(x, mod, l, e, gain, w_in, kv_prev):
    tm = 512
    row = _mod_row(tm)
    npt = P_TOK // tm
    n_x = len(x)
    args = [*x, mod, gain, w_in]
    in_specs = _tok_specs(x, tm) + [
        pl.BlockSpec((None, None, 6, D_MODEL), lambda i: (l, row(i), 0, 0)),
        pl.BlockSpec((1, D_MODEL), lambda i: (0, 0)),
        pl.BlockSpec((None, D_MODEL, PA_W + PB_W + DT_W), lambda i: (e, 0, 0),
                     pipeline_mode=pl.Buffered(1)),
    ]
    kern = functools.partial(_inproj_kernel, n_x=n_x, tm=tm, all_slots=kv_prev is None)
    if kv_prev is None:
        io_alias = {}
        kv_spec = pl.BlockSpec((tm // SEQ, N_EVEN, SEQ, ATT_INNER),
                               lambda i: (jnp.minimum(i, npt - 1), 0, 0, 0))
    else:
        io_alias = {n_x + 3: 4, n_x + 4: 5}
        args += list(kv_prev)
        in_specs += [pl.BlockSpec(memory_space=pl.ANY)] * 2
        kern = _drop_args(kern, n_x + 3, 2)
        kv_spec = pl.BlockSpec((tm // SEQ, None, SEQ, ATT_INNER),
                               lambda i: (jnp.minimum(i, npt - 1), e, 0, 0))
    kv_shape = jax.ShapeDtypeStruct((BATCH, N_EVEN, SEQ, ATT_INNER), F32)
    return pl.pallas_call(
        kern,
        grid=(N_TOK // tm,),
        in_specs=in_specs,
        out_specs=[
            pl.BlockSpec((N_SLAB, tm, LANE), lambda i: (0, i, 0)),
            pl.BlockSpec((tm, SSD_INNER), lambda i: (i, 0)),
            pl.BlockSpec((tm, PB_W), lambda i: (i, 0)),
            pl.BlockSpec((tm, DT_W), lambda i: (i, 0)),
            kv_spec, kv_spec,
        ],
        out_shape=[
            jax.ShapeDtypeStruct((N_SLAB, N_TOK, LANE), F32),
            jax.ShapeDtypeStruct((N_TOK, SSD_INNER), F32),
            jax.ShapeDtypeStruct((N_TOK, PB_W), BF16),
            jax.ShapeDtypeStruct((N_TOK, DT_W), F32),
            kv_shape, kv_shape,
        ],
        input_output_aliases=io_alias,
        compiler_params=_params("arbitrary", vmem=VMEM_LIMIT_INPROJ),
        name="inproj",
    )(*args)


def _ssd_kernel(*refs, L, has_h0, emit_state, all_slots):
    it = iter(refs)
    xbc_ref, z_ref, dt_ref = next(it), next(it), next(it)
    convw_ref, convb_ref, dtb_ref, alog_ref, dskip_ref, norm_ref = (next(it) for _ in range(6))
    sel_refs = (next(it), next(it))
    if has_h0:
        h0_refs = (next(it), next(it))
    y_ref = next(it)
    if emit_state:
        hout_refs = (next(it), next(it))
    (xact, xdt_f, xdt_b, bt, cact, tmp, yacc, dtv, state_f, state_b,
     acs_s, acst_s, cb_s) = (next(it) for _ in range(13))

    nc = L // CHUNK
    nq = L // SUBLANE
    half_w = CONV_W // 2

    rq = lax.broadcasted_iota(jnp.int32, (QBLK, LANE), 0)

    def phase(s, pe, q0):
        r0 = SUBLANE * q0 + pe
        if r0 < 0:
            a = xbc_ref[s, pl.ds(r0 + SUBLANE, QBLK, stride=SUBLANE), :]
            return jnp.where(rq == 0, 0.0, pltpu.roll(a, 1, 0))
        if r0 + SUBLANE * (QBLK - 1) >= L:
            a = xbc_ref[s, pl.ds(r0 - SUBLANE, QBLK, stride=SUBLANE), :]
            return jnp.where(rq == QBLK - 1, 0.0, pltpu.roll(a, QBLK - 1, 0))
        return xbc_ref[s, pl.ds(r0, QBLK, stride=SUBLANE), :]

    def conv_slab(s, dst, d_idx):
        ls = slice(s * LANE, (s + 1) * LANE)
        for q0 in range(0, nq, QBLK):
            ext = {pe: phase(s, pe, q0) for pe in range(-half_w, SUBLANE + half_w)}
            for p in range(SUBLANE):
                acc = jnp.broadcast_to(convb_ref[:, ls], (QBLK, LANE))
                for k in range(CONV_W):
                    acc = acc + convw_ref[k:k + 1, ls] * ext[p + k - half_w]
                dst[d_idx, pl.ds(SUBLANE * q0 + p, QBLK, stride=SUBLANE), :] = _silu(acc)

    for m in range(NH2):
        conv_slab(m, xact, m)
    for g in range(SSD_GROUPS):
        conv_slab(NH2 + g, tmp, g)
    for c in range(nc):
        for g in range(SSD_GROUPS):
            bt[c, g * D_STATE:(g + 1) * D_STATE, :] = (
                tmp[g, c * CHUNK:(c + 1) * CHUNK, :].T.astype(BF16))
    for g in range(SSD_GROUPS):
        conv_slab(NH2 + SSD_GROUPS + g, tmp, g)
    for c in range(nc):
        rows = slice(c * CHUNK, (c + 1) * CHUNK)
        for g in range(SSD_GROUPS):
            cact[g, rows, :] = tmp[g, rows, :].astype(BF16)

    dr = dt_ref[...] + dtb_ref[...]
    dtv[...] = jnp.maximum(dr, 0.0) + jnp.log1p(jnp.exp(-jnp.abs(dr)))
    a_row = -jnp.exp(alog_ref[...]) * math.log2(math.e)

    for c in range(nc):
        rows = slice(c * CHUNK, (c + 1) * CHUNK)
        d = dtv[rows, :]
        d_hi = d.astype(BF16)
        d_lo = (d - d_hi.astype(F32)).astype(BF16)
        d2 = jnp.concatenate([d_hi, d_lo], axis=1)
        spread_f = _dot(d2, sel_refs[0][...])
        spread_b = _dot(d2, sel_refs[1][...])
        for m in range(NH2):
            lanes = slice(m * LANE, (m + 1) * LANE)
            xm = xact[m, rows, :]
            xdt_f[m, rows, :] = (xm * spread_f[:, lanes]).astype(BF16)
            xdt_b[m, rows, :] = (xm * spread_b[:, lanes]).astype(BF16)

    ri = lax.broadcasted_iota(jnp.int32, (CHUNK, CHUNK), 0)
    ci = lax.broadcasted_iota(jnp.int32, (CHUNK, CHUNK), 1)
    low_half = ci < SSD_HD
    low_row = low_half[0:1, :]

    for c in range(nc):
        rows = slice(c * CHUNK, (c + 1) * CHUNK)
        dta = dtv[rows, :] * a_row
        run_f = jnp.dot((ri >= ci).astype(F32), dta, preferred_element_type=F32,
                        precision=lax.Precision.HIGHEST)
        run_b = jnp.dot((ri <= ci).astype(F32), dta, preferred_element_type=F32,
                        precision=lax.Precision.HIGHEST)
        acs = jnp.where(ci < SSD_HEADS, run_f, run_b)
        acs_s[c] = acs
        acst_s[c] = acs.T
        for g in range(SSD_GROUPS):
            cb_s[c, g] = _dot(cact[g, rows, :], bt[c, g * D_STATE:(g + 1) * D_STATE, :])

    for d_idx, st_ref in enumerate((state_f, state_b)):
        for m in range(NH2):
            if has_h0:
                st_ref[m] = h0_refs[d_idx][m].T
            else:
                st_ref[m] = jnp.zeros((D_STATE, LANE), F32)

    def chunk_rows(c):
        return pl.ds(pl.multiple_of(c * CHUNK, CHUNK), CHUNK)

    def chunk_step(c, fwd, add):
        mask = (ri >= ci) if fwd else (ri <= ci)
        col0 = 0 if fwd else SSD_HEADS
        last = CHUNK - 1 if fwd else 0
        xdt = xdt_f if fwd else xdt_b
        state = state_f if fwd else state_b
        rows = chunk_rows(c)
        acs = acs_s[c]
        acs_t = acst_s[c]
        acs_last = acs[last:last + 1, :]
        for g in range(SSD_GROUPS):
            btg = bt[c, g * D_STATE:(g + 1) * D_STATE, :]
            cg = cact[g, rows, :]
            cbm = jnp.where(mask, cb_s[c, g], 0.0)
            for pr in range(2):
                m = g * 2 + pr
                ha, hb = col0 + 2 * m, col0 + 2 * m + 1
                lanes = slice(m * LANE, (m + 1) * LANE)
                xd2 = xdt[m, rows, :]
                acol_a = jnp.broadcast_to(acs[:, ha:ha + 1], (CHUNK, CHUNK))
                acol_b = jnp.broadcast_to(acs[:, hb:hb + 1], (CHUNK, CHUNK))
                w_a = cbm * jnp.exp2(jnp.minimum(acol_a - acs_t[ha:ha + 1, :], 0.0))
                w_b = cbm * jnp.exp2(jnp.minimum(acol_b - acs_t[hb:hb + 1, :], 0.0))
                w2 = jnp.concatenate([w_a.astype(BF16), w_b.astype(BF16)], axis=1)
                zero = jnp.zeros_like(xd2)
                x_blk = jnp.concatenate([jnp.where(low_half, xd2, zero),
                                         jnp.where(low_half, zero, xd2)], axis=0)
                y_in = _dot(w2, x_blk)
                acol2 = jnp.where(low_half, acol_a, acol_b)
                last2 = jnp.where(low_row, acs_last[:, ha:ha + 1], acs_last[:, hb:hb + 1])
                h_prev = state[m]
                y_out = _dot(cg, h_prev.astype(BF16)) * jnp.exp2(acol2)
                to_end = jnp.exp2(last2 - acol2)
                st = _dot(btg, (to_end * xd2.astype(F32)).astype(BF16))
                state[m] = jnp.exp2(last2) * h_prev + st
                y2 = y_in + y_out
                if fwd:
                    y2 = y2 + dskip_ref[:, lanes] * xact[m, rows, :]
                if add:
                    y2 = y2 + yacc[rows, lanes]
                yacc[rows, lanes] = y2

    def finish(c):
        rows = chunk_rows(c)
        yt = yacc[rows, :] * _silu(z_ref[rows, :])
        y_ref[rows, :] = (_rms(yt) * norm_ref[...]).astype(y_ref.dtype)

    def first_half(s, carry):
        chunk_step(s, True, False)
        chunk_step(nc - 1 - s, False, False)
        return carry

    def second_half(s, carry):
        chunk_step(s, True, True)
        chunk_step(nc - 1 - s, False, True)
        finish(s)
        finish(nc - 1 - s)
        return carry

    lax.fori_loop(0, nc // 2, first_half, 0)
    lax.fori_loop(nc // 2, nc, second_half, 0)

    if emit_state:
        for d_idx, st_ref in enumerate((state_f, state_b)):
            for m in range(NH2):
                if all_slots:
                    hout_refs[d_idx][0, m] = st_ref[m].T
                    for later in range(1, N_EVEN):
                        hout_refs[d_idx][later, m] = jnp.zeros((LANE, D_STATE), F32)
                else:
                    hout_refs[d_idx][m] = st_ref[m].T


def _head_spread_tables():
    ch_head = np.arange(SSD_INNER) // SSD_HD
    out = []
    for d in range(2):
        k = np.arange(2 * DT_W) % DT_W
        out.append(jnp.asarray(k[:, None] == (d * SSD_HEADS + ch_head)[None, :], BF16))
    return out


def _ssd_parts(xbc, z, dtarr, states_prev, weights, e, *, L, nb, blk0, h0=None):
    emit_state = states_prev is not None
    all_slots = isinstance(states_prev, str)
    nblk = N_TOK // L
    nc = L // CHUNK
    vec = lambda w: pl.BlockSpec((1, w), lambda b: (0, 0))
    in_specs = [
        pl.BlockSpec((N_SLAB, L, LANE), lambda b: (0, blk0 + b, 0)),
        pl.BlockSpec((None, L, SSD_INNER), lambda b: (blk0 + b, 0, 0)),
        pl.BlockSpec((None, L, DT_W), lambda b: (blk0 + b, 0, 0)),
        pl.BlockSpec((8, CONV_CH), lambda b: (0, 0)),
        vec(CONV_CH), vec(DT_W), vec(DT_W), vec(SSD_INNER), vec(SSD_INNER),
        pl.BlockSpec((2 * DT_W, SSD_INNER), lambda b: (0, 0)),
        pl.BlockSpec((2 * DT_W, SSD_INNER), lambda b: (0, 0)),
    ]
    args = [xbc, z.reshape(nblk, L, SSD_INNER), dtarr.reshape(nblk, L, DT_W), *weights,
            *_head_spread_tables()]
    state_spec = pl.BlockSpec((None, None, NH2, LANE, D_STATE), lambda b: (b, e, 0, 0, 0))
    if h0 is not None:
        for h in h0:
            args.append(h.reshape(DEC_BATCH, N_EVEN, NH2, LANE, D_STATE))
            in_specs.append(state_spec)
    kern = functools.partial(_ssd_kernel, L=L, has_h0=h0 is not None, emit_state=emit_state,
                             all_slots=all_slots)
    out_specs = [pl.BlockSpec((None, L, SSD_INNER), lambda b: (b, 0, 0))]
    out_shape = [jax.ShapeDtypeStruct((nb, L, SSD_INNER), BF16)]
    io_alias = {}
    if emit_state:
        if all_slots:
            out_spec = pl.BlockSpec((None, N_EVEN, NH2, LANE, D_STATE), lambda b: (b, 0, 0, 0, 0))
        else:
            out_spec = state_spec
            io_alias = {len(args): 1, len(args) + 1: 2}
            kern = _drop_args(kern, len(args), 2)
            args += list(states_prev)
            in_specs += [pl.BlockSpec(memory_space=pl.ANY)] * 2
        out_specs += [out_spec, out_spec]
        out_shape += [jax.ShapeDtypeStruct((nb, N_EVEN, NH2, LANE, D_STATE), F32)] * 2
    scratch = [
        pltpu.VMEM((NH2, L, LANE), F32),
        pltpu.VMEM((NH2, L, LANE), BF16),
        pltpu.VMEM((NH2, L, LANE), BF16),
        pltpu.VMEM((nc, SSD_GROUPS * D_STATE, CHUNK), BF16),
        pltpu.VMEM((SSD_GROUPS, L, D_STATE), BF16),
        pltpu.VMEM((SSD_GROUPS, L, D_STATE), F32),
        pltpu.VMEM((L, SSD_INNER), F32),
        pltpu.VMEM((L, DT_W), F32),
        pltpu.VMEM((NH2, D_STATE, LANE), F32),
        pltpu.VMEM((NH2, D_STATE, LANE), F32),
        pltpu.VMEM((nc, CHUNK, LANE), F32),
        pltpu.VMEM((nc, LANE, CHUNK), F32),
        pltpu.VMEM((nc, SSD_GROUPS, CHUNK, CHUNK), F32),
    ]
    return dict(kernel=kern, args=args, in_specs=in_specs, out_specs=out_specs,
                out_shape=out_shape, scratch=scratch, alias=io_alias)


def _rope(x, cos, sin_a, sin_b):
    return (x * cos + pltpu.roll(x, LANE - ROPE_PAIRS, 1) * sin_a
            + pltpu.roll(x, ROPE_PAIRS, 1) * sin_b)


def _attn_kernel(*refs, L, n_ctx, lam_init, stack_maps, q_axis):
    it = iter(refs)
    q_ref, k_ref, v_ref = next(it), next(it), next(it)
    if n_ctx:
        ck_ref, cv_ref = next(it), next(it)
        cq, saq, sbq, ck, sak, sbk = (next(it) for _ in range(6))
    lam_ref, sub_ref = next(it), next(it)
    o_ref = next(it)
    kall, vext = next(it), next(it)
    lk = n_ctx + L
    tq = q_ref.shape[0]

    def prepare_keys_values():
        for h in range(ATT_HEADS):
            hs = slice(h * ATT_VD, (h + 1) * ATT_VD)
            if n_ctx:
                kall[h, 0:n_ctx, :] = ck_ref[:, hs].astype(BF16)
                vext[h, 0:n_ctx, 0:ATT_VD] = cv_ref[:, hs].astype(BF16)
                kh = _rope(k_ref[:, hs].astype(F32), ck[...], sak[...], sbk[...])
                kall[h, n_ctx:lk, :] = kh.astype(BF16)
            else:
                kall[h] = k_ref[:, hs]
            vext[h, n_ctx:lk, 0:ATT_VD] = v_ref[:, hs]
            vext[h, :, ATT_VD:MXU_N] = jnp.ones((lk, MXU_N - ATT_VD), BF16)

    if q_axis is None:
        prepare_keys_values()
    else:
        pl.when(pl.program_id(q_axis) == 0)(prepare_keys_values)

    lane = lax.broadcasted_iota(jnp.int32, (1, ATT_VD), 1)
    scale = ATT_HD ** -0.5 * math.log2(math.e)
    m1 = jnp.where(lane < ATT_HD, scale, 0.0)
    m2 = jnp.where(lane < ATT_HD, 0.0, scale)
    lq = lam_ref[...]
    lam = (jnp.exp(jnp.sum(lq[0:1, :] * lq[1:2, :], axis=-1, keepdims=True))
           - jnp.exp(jnp.sum(lq[2:3, :] * lq[3:4, :], axis=-1, keepdims=True)) + lam_init)

    for h in range(ATT_HEADS):
        hs = slice(h * ATT_VD, (h + 1) * ATT_VD)
        q = q_ref[:, hs].astype(F32)
        if n_ctx:
            q = _rope(q, cq[...], saq[...], sbq[...])
        def attend(qm):
            s = _dot_nt(qm, kall[h])
            p = jnp.exp2(s - jnp.max(s, axis=-1, keepdims=True)).astype(BF16)
            pv = _dot(p, vext[h])
            return pv[:, 0:ATT_VD] / pv[:, ATT_VD:MXU_N]

        q1, q2 = (q * m1).astype(BF16), (q * m2).astype(BF16)
        if stack_maps:
            on = attend(jnp.concatenate([q1, q2], axis=0))
            o = on[0:tq] - lam * on[tq:2 * tq]
        else:
            o = attend(q1) - lam * attend(q2)
        o_ref[:, hs] = (_rms(o) * sub_ref[...] * (1.0 - lam_init)).astype(o_ref.dtype)


def _attn_parts(pb, lam_qk, subln, lam_init, *, L, nb, blk0, tq, ctx=None, rope=None):
    nblk = N_TOK // L
    one_d = tq == L
    spec = lambda shape, f: pl.BlockSpec(shape, (lambda b: f(b, 0)) if one_d else f)
    pb3 = pb.reshape(nblk, L, PB_W)
    in_specs = [
        spec((None, tq, ATT_INNER), lambda b, i: (blk0 + b, i, 0)),
        spec((None, L, ATT_INNER), lambda b, i: (blk0 + b, 0, 1)),
        spec((None, L, ATT_INNER), lambda b, i: (blk0 + b, 0, 2)),
    ]
    args = [pb3, pb3, pb3]
    n_ctx = 0
    if ctx is not None:
        cache_k, cache_v, e = ctx
        n_ctx = cache_k.shape[2]
        for cch in (cache_k, cache_v):
            args.append(cch.reshape(DEC_BATCH, N_EVEN, n_ctx, ATT_INNER))
            in_specs.append(spec((None, None, n_ctx, ATT_INNER), lambda b, i: (b, e, 0, 0)))
        for t in rope:
            args.append(t)
            in_specs.append(spec((tq, ATT_VD), lambda b, i: (i, 0)))
        for t in rope:
            args.append(t)
            in_specs.append(spec((L, ATT_VD), lambda b, i: (0, 0)))
    args += [lam_qk, subln]
    in_specs += [spec((4, ATT_HD), lambda b, i: (0, 0)), spec((1, ATT_VD), lambda b, i: (0, 0))]
    kern = functools.partial(_attn_kernel, L=L, n_ctx=n_ctx, lam_init=lam_init,
                             stack_maps=n_ctx + L <= 2 * MXU_N,
                             q_axis=None if one_d else 1)
    return dict(kernel=kern, args=args, in_specs=in_specs,
                out_specs=[spec((None, tq, ATT_INNER), lambda b, i: (b, i, 0))],
                out_shape=[jax.ShapeDtypeStruct((nb, L, ATT_INNER), BF16)],
                scratch=[pltpu.VMEM((ATT_HEADS, n_ctx + L, ATT_VD), BF16),
                         pltpu.VMEM((ATT_HEADS, n_ctx + L, MXU_N), BF16)],
                alias={})


def _outproj_kernel(*refs, n_x, tm):
    x_refs = refs[:n_x]
    yp_ref, ys_ref, op_ref, os_ref, mod_ref, w_ref, out_ref = refs[n_x:]
    y = _tok_load((yp_ref, ys_ref), tm)
    o = _tok_load((op_ref, os_ref), tm)
    acc = _dot(y, w_ref[0:SSD_INNER, :]) + _dot(o, w_ref[SSD_INNER:, :])
    out_ref[...] = _tok_load(x_refs, tm) + mod_ref[2:3, :] * acc


def _outproj_call(x, y, o, mod, l, e, w):
    tm = 512
    row = _mod_row(tm)
    return pl.pallas_call(
        functools.partial(_outproj_kernel, n_x=len(x), tm=tm),
        grid=(N_TOK // tm,),
        in_specs=_tok_specs(x, tm) + _tok_specs(y, tm) + _tok_specs(o, tm) + [
            pl.BlockSpec((None, None, 6, D_MODEL), lambda i: (l, row(i), 0, 0)),
            pl.BlockSpec((None, SSD_INNER + ATT_INNER, D_MODEL), lambda i: (e, 0, 0))],
        out_specs=pl.BlockSpec((tm, D_MODEL), lambda i: (i, 0)),
        out_shape=jax.ShapeDtypeStruct((N_TOK, D_MODEL), F32),
        compiler_params=_params("arbitrary"),
        name="outproj",
    )(*x, *y, *o, mod, w)


FFN_SUB = FFN_HIDDEN // MXU_N


def _ffn_kernel(*refs, n_x, tm, final):
    x_refs = refs[:n_x]
    mod_ref, g_ref, wg_ref, wu_ref, wo_ref = refs[n_x:n_x + 5]
    if final:
        gf_ref, op_ref, os_ref, h_ref, a_ref, win_s, wout_s = refs[n_x + 5:]
    else:
        out_ref, h_ref, a_ref, win_s, wout_s = refs[n_x + 5:]
    s = pl.program_id(0)
    tile = jnp.maximum(s - (FFN_SUB - 1), 0)
    npt = P_TOK // tm

    def load_x():
        if n_x == 1:
            return x_refs[0][...]
        return jnp.where(tile < npt, x_refs[0][...], x_refs[1][...])

    def modulated():
        h = _rms(load_x()) * g_ref[...] * (1.0 + mod_ref[4:5, :]) + mod_ref[3:4, :]
        return h.astype(BF16)

    def hidden(j, hb, wg, wu):
        a_ref[:, j * MXU_N:(j + 1) * MXU_N] = (_silu(_dot(hb, wg)) * _dot(hb, wu)).astype(BF16)

    for j in range(FFN_SUB):
        @pl.when(s == j)
        def _(j=j):
            if j == 0:
                h_ref[...] = modulated()
            cs = slice(j * MXU_N, (j + 1) * MXU_N)
            us = slice(FFN_HIDDEN + j * MXU_N, FFN_HIDDEN + (j + 1) * MXU_N)
            wg, wu = wg_ref[...].astype(BF16), wu_ref[...].astype(BF16)
            win_s[:, cs] = wg
            win_s[:, us] = wu
            wout_s[cs, :] = wo_ref[...].astype(BF16)
            hidden(j, h_ref[...], wg, wu)

    @pl.when(s >= FFN_SUB)
    def _():
        h_ref[...] = modulated()
        for j in range(FFN_SUB):
            cs = slice(j * MXU_N, (j + 1) * MXU_N)
            us = slice(FFN_HIDDEN + j * MXU_N, FFN_HIDDEN + (j + 1) * MXU_N)
            hidden(j, h_ref[...], win_s[:, cs], win_s[:, us])

    @pl.when(s >= FFN_SUB - 1)
    def _():
        x_new = load_x() + mod_ref[5:6, :] * _dot(a_ref[...], wout_s[...])
        if final:
            y = _rms(x_new) * gf_ref[...]

            @pl.when(tile < npt)
            def _():
                op_ref[...] = y

            @pl.when(tile >= npt)
            def _():
                os_ref[...] = y
        else:
            out_ref[...] = x_new


def _ffn_call(x, mod, l, gain, w_in, w_out, final_gain=None):
    tm = 512
    row = _mod_row(tm)
    npt = P_TOK // tm
    tile = lambda s: jnp.maximum(s - (FFN_SUB - 1), 0)
    chunk = lambda s: jnp.minimum(s, FFN_SUB - 1)
    final = final_gain is not None
    if len(x) == 1:
        x_specs = [pl.BlockSpec((tm, D_MODEL), lambda s: (tile(s), 0))]
    else:
        x_specs = [pl.BlockSpec((tm, D_MODEL), lambda s: (jnp.minimum(tile(s), npt - 1), 0)),
                   pl.BlockSpec((tm, D_MODEL), lambda s: (jnp.maximum(tile(s) - npt, 0), 0))]
    args = [*x, mod, gain, w_in, w_in, w_out]
    in_specs = x_specs + [
        pl.BlockSpec((None, None, 6, D_MODEL), lambda s: (l, row(tile(s)), 0, 0)),
        pl.BlockSpec((1, D_MODEL), lambda s: (0, 0)),
        pl.BlockSpec((None, D_MODEL, MXU_N), lambda s: (l, 0, chunk(s))),
        pl.BlockSpec((None, D_MODEL, MXU_N), lambda s: (l, 0, FFN_SUB + chunk(s))),
        pl.BlockSpec((None, MXU_N, D_MODEL), lambda s: (l, chunk(s), 0)),
    ]
    if final:
        args.append(final_gain)
        in_specs.append(pl.BlockSpec((1, D_MODEL), lambda s: (0, 0)))
        out_specs = [pl.BlockSpec((tm, D_MODEL), lambda s: (jnp.minimum(tile(s), npt - 1), 0)),
                     pl.BlockSpec((tm, D_MODEL), lambda s: (jnp.maximum(tile(s) - npt, 0), 0))]
        out_shape = [jax.ShapeDtypeStruct((P_TOK, D_MODEL), F32),
                     jax.ShapeDtypeStruct((S_TOK, D_MODEL), F32)]
    else:
        out_specs = pl.BlockSpec((tm, D_MODEL), lambda s: (tile(s), 0))
        out_shape = jax.ShapeDtypeStruct((N_TOK, D_MODEL), F32)
    return pl.pallas_call(
        functools.partial(_ffn_kernel, n_x=len(x), tm=tm, final=final),
        grid=(FFN_SUB - 1 + N_TOK // tm,),
        in_specs=in_specs,
        out_specs=out_specs,
        out_shape=out_shape,
        scratch_shapes=[pltpu.VMEM((tm, D_MODEL), BF16), pltpu.VMEM((tm, FFN_HIDDEN), BF16),
                        pltpu.VMEM((D_MODEL, 2 * FFN_HIDDEN), BF16),
                        pltpu.VMEM((FFN_HIDDEN, D_MODEL), BF16)],
        compiler_params=_params("arbitrary"),
        name="ffn",
    )(*args)


def _four_kernel(x_ref, mod_ref, g_ref, cc_ref, sc_ref, cs_ref, w_ref, b_ref, out_ref, stk_ref, *, L):
    x = x_ref[...]
    h = (_rms(x) * g_ref[...] * (1.0 + mod_ref[1:2, :]) + mod_ref[0:1, :]).astype(BF16)
    for g in range(FOUR_GROUPS):
        ls = slice(g * FOUR_GC, (g + 1) * FOUR_GC)
        hg = h[:, ls]
        stk_ref[0:L, ls] = _dot(hg, cc_ref[...]).astype(BF16)
        stk_ref[L:2 * L, ls] = _dot(hg, sc_ref[...]).astype(BF16)
    f = _dot(cs_ref[...], stk_ref[...]).astype(BF16)
    out_ref[...] = x + mod_ref[2:3, :] * (_dot(f, w_ref[...]) + b_ref[...])


def _dft_tables(L):
    n = FOUR_GC
    ang_c = 2.0 * np.pi * ((np.arange(n)[:, None] * np.arange(n)[None, :]) % n) / n
    ang_l = 2.0 * np.pi * ((np.arange(L)[:, None] * np.arange(L)[None, :]) % L) / L
    cc = np.cos(ang_c) / math.sqrt(n)
    sc = np.sin(ang_c) / math.sqrt(n)
    cs = np.concatenate([np.cos(ang_l), -np.sin(ang_l)], axis=1) / math.sqrt(L)
    return (jnp.asarray(cc, F32).astype(BF16), jnp.asarray(sc, F32).astype(BF16),
            jnp.asarray(cs, F32).astype(BF16))


def _four_call(x, mod, l, i_odd, gain, w, b, *, L, nb, blk0):
    nblk = N_TOK // L
    cc, sc, cs = _dft_tables(L)
    row = _mod_row(L)
    out = pl.pallas_call(
        functools.partial(_four_kernel, L=L),
        grid=(nb,),
        in_specs=[
            pl.BlockSpec((None, L, D_MODEL), lambda i: (blk0 + i, 0, 0)),
            pl.BlockSpec((None, None, 6, D_MODEL), lambda i: (l, row(blk0 + i), 0, 0)),
            pl.BlockSpec((1, D_MODEL), lambda i: (0, 0)),
            pl.BlockSpec((FOUR_GC, FOUR_GC), lambda i: (0, 0)),
            pl.BlockSpec((FOUR_GC, FOUR_GC), lambda i: (0, 0)),
            pl.BlockSpec((L, 2 * L), lambda i: (0, 0)),
            pl.BlockSpec((None, D_MODEL, D_MODEL), lambda i: (i_odd, 0, 0)),
            pl.BlockSpec((1, D_MODEL), lambda i: (0, 0)),
        ],
        out_specs=pl.BlockSpec((None, L, D_MODEL), lambda i: (i, 0, 0)),
        out_shape=jax.ShapeDtypeStruct((nb, L, D_MODEL), F32),
        scratch_shapes=[pltpu.VMEM((2 * L, D_MODEL), BF16)],
        compiler_params=_params("parallel"),
        name=f"fourier_L{L}",
    )(x.reshape(nblk, L, D_MODEL), mod, gain, cc, sc, cs, w, b)
    return out.reshape(nb * L, D_MODEL)


def _rope_lane_tables(L):
    t = jnp.arange(L)
    row = (t // GRID_W).astype(F32)
    col = (t % GRID_W).astype(F32)
    freq = ROPE_THETA ** (-jnp.arange(ROPE_PAIRS, dtype=F32) / ROPE_PAIRS)
    ar = row[:, None] * freq[None]
    ac = col[:, None] * freq[None]
    z = jnp.zeros_like(ar)
    cos32 = lambda a: jnp.concatenate([jnp.cos(a), jnp.cos(a)], axis=-1)
    sa32 = lambda a: jnp.concatenate([-jnp.sin(a), z], axis=-1)
    sb32 = lambda a: jnp.concatenate([z, jnp.sin(a)], axis=-1)
    tile = lambda f: jnp.concatenate([f(ar), f(ac), f(ar), f(ac)], axis=-1)
    return tile(cos32), tile(sa32), tile(sb32)


def _pad_lanes(v, width):
    v = v.reshape(1, -1).astype(F32)
    return jnp.pad(v, ((0, 0), (0, width - v.shape[1])))


def kernel(x_prompt, x_sample, cache_k, cache_v, state_ssd_fwd, state_ssd_bwd, c, c_ctx, w_ada, b_ada, norm_mix, norm_ffn, w_in_ab, conv_w, conv_b, dt_bias, a_log, d_skip, ssd_norm, lambda_qk, subln, w_out_ab, w_four, b_four, w_ffn_in, w_ffn_out, norm_final):
    x = (x_prompt.reshape(P_TOK, D_MODEL), x_sample.reshape(S_TOK, D_MODEL))
    cvec = jnp.concatenate([c_ctx[None, :], c, jnp.zeros((N_MOD - 1 - DEC_BATCH, D_MODEL), F32)], axis=0)
    mod = _ada_call(cvec, w_ada, b_ada).reshape(DEPTH, N_MOD, 6, D_MODEL)
    rope = _rope_lane_tables(DEC_SEQ)

    w_in_bf = _regroup_call(w_in_ab)
    w_out_bf = w_out_ab.astype(BF16)
    w_four_bf = w_four.astype(BF16)

    kv_new = None
    states = "new"
    blk_s = P_TOK // DEC_SEQ
    for l in range(DEPTH):
        gain_mix = norm_mix[l].reshape(1, D_MODEL)
        if l % 2 == 0:
            e = l // 2
            lam_init = 0.8 - 0.6 * math.exp(-0.3 * l)
            xbc, z, pb, dtarr, k_new, v_new = _inproj_call(x, mod, l, e, gain_mix, w_in_bf, kv_new)
            kv_new = (k_new, v_new)

            ssd_w = (jnp.pad(conv_w[e], ((0, 8 - CONV_W), (0, 0))), conv_b[e].reshape(1, CONV_CH),
                     _pad_lanes(dt_bias[e], DT_W), _pad_lanes(a_log[e], DT_W),
                     jnp.repeat(d_skip[e], SSD_HD).reshape(1, SSD_INNER),
                     ssd_norm[e].reshape(1, SSD_INNER))
            sub = subln[e].reshape(1, ATT_VD)
            ((y_p, hf, hb),) = _launch(
                [_ssd_parts(xbc, z, dtarr, states, ssd_w, e, L=SEQ, nb=BATCH, blk0=0)],
                (BATCH,), ("parallel",), "ssd_L256")
            states = (hf, hb)
            ((o_p,),) = _launch(
                [_attn_parts(pb, lambda_qk[e], sub, lam_init, L=SEQ, nb=BATCH, blk0=0, tq=SEQ)],
                (BATCH,), ("parallel",), "attn_L256")
            ((y_s,),) = _launch(
                [_ssd_parts(xbc, z, dtarr, None, ssd_w, e, L=DEC_SEQ, nb=DEC_BATCH, blk0=blk_s,
                            h0=(state_ssd_fwd, state_ssd_bwd))],
                (DEC_BATCH,), ("parallel",), "ssd_L1024")
            ((o_s,),) = _launch(
                [_attn_parts(pb, lambda_qk[e], sub, lam_init, L=DEC_SEQ, nb=DEC_BATCH, blk0=blk_s,
                             tq=256, ctx=(cache_k, cache_v, e), rope=rope)],
                (DEC_BATCH, DEC_SEQ // 256), ("parallel", "arbitrary"), "attn_L1024")
            halves = lambda a, b: (a.reshape(P_TOK, -1), b.reshape(S_TOK, -1))
            x = (_outproj_call(x, halves(y_p, y_s), halves(o_p, o_s), mod, l, e, w_out_bf),)
        else:
            i_odd = l // 2
            b4 = b_four[i_odd].reshape(1, D_MODEL)
            x = (_four_call(x[0], mod, l, i_odd, gain_mix, w_four_bf, b4, L=SEQ, nb=BATCH, blk0=0),
                 _four_call(x[0], mod, l, i_odd, gain_mix, w_four_bf, b4, L=DEC_SEQ, nb=DEC_BATCH,
                            blk0=blk_s))
        last = l == DEPTH - 1
        x = _ffn_call(x, mod, l, norm_ffn[l].reshape(1, D_MODEL), w_ffn_in, w_ffn_out,
                      final_gain=norm_final.reshape(1, D_MODEL) if last else None)
        x = tuple(x) if last else (x,)

    y_p, y_s = x
    return (y_p.reshape(BATCH, SEQ, D_MODEL), y_s.reshape(DEC_BATCH, DEC_SEQ, D_MODEL),
            kv_new[0].reshape(BATCH, N_EVEN, SEQ, ATT_HEADS, 2, ATT_HD),
            kv_new[1].reshape(BATCH, N_EVEN, SEQ, ATT_HEADS, ATT_VD),
            states[0].reshape(BATCH, N_EVEN, SSD_HEADS, SSD_HD, D_STATE),
            states[1].reshape(BATCH, N_EVEN, SSD_HEADS, SSD_HD, D_STATE))
```

```python
import functools
import math

import numpy as np
import jax
import jax.numpy as jnp
from jax import lax
from jax.experimental import pallas as pl
from jax.experimental.pallas import tpu as pltpu

D_MODEL = 1024
BATCH = 16
SEQ = 256
DEPTH = 4
N_EVEN = 2
DEC_BATCH = 4
DEC_SEQ = 1024
PAST_LEN = 512
GRID_W = 64
SSD_HEADS = 16
SSD_HD = 64
SSD_INNER = 1024
SSD_GROUPS = 4
D_STATE = 128
CONV_W = 5
CONV_CH = 2048
CHUNK = 128
ATT_HEADS = 8
ATT_HD = 64
ATT_VD = 128
ATT_INNER = 1024
ROPE_THETA = 10000.0
ROPE_PAIRS = 16
FOUR_GROUPS = 4
FOUR_GC = D_MODEL // FOUR_GROUPS
FFN_HIDDEN = 2816
EPS = 1e-6

P_TOK = BATCH * SEQ
S_TOK = DEC_BATCH * DEC_SEQ
N_TOK = P_TOK + S_TOK
N_MOD = 8
PA_W = CONV_CH + SSD_INNER
PB_W = 3 * ATT_INNER
DT_W = 128
NH2 = SSD_HEADS // 2
LANE = 128
SUBLANE = 8
N_SLAB = CONV_CH // LANE
QBLK = 32
MXU_N = 256
VMEM_LIMIT = 56 * 1024 * 1024
VMEM_LIMIT_INPROJ = 61 * 1024 * 1024

F32 = jnp.float32
BF16 = jnp.bfloat16


def _params(*sem, vmem=VMEM_LIMIT):
    return pltpu.CompilerParams(dimension_semantics=sem, vmem_limit_bytes=vmem)


def _dot(a, b):
    return jnp.dot(a, b, preferred_element_type=F32)


def _dot_nt(a, b):
    return lax.dot_general(a, b, (((1,), (1,)), ((), ())), preferred_element_type=F32)


def _silu(x):
    h = 0.5 * x
    return h + h * jnp.tanh(h)


def _rms(x):
    return x * lax.rsqrt(jnp.mean(x * x, axis=-1, keepdims=True) + EPS)


def _mod_row(tm):
    return lambda i: jnp.maximum((i * tm) // DEC_SEQ - (P_TOK // DEC_SEQ - 1), 0)


def _drop_args(kern, start, n):
    def wrapped(*refs):
        return kern(*(refs[:start] + refs[start + n:]))
    return wrapped


def _tok_specs(parts, tm):
    width = parts[0].shape[-1]
    if len(parts) == 1:
        return [pl.BlockSpec((tm, width), lambda i: (i, 0))]
    npt = P_TOK // tm
    return [pl.BlockSpec((tm, width), lambda i: (jnp.minimum(i, npt - 1), 0)),
            pl.BlockSpec((tm, width), lambda i: (jnp.maximum(i - npt, 0), 0))]


def _tok_load(refs, tm):
    if len(refs) == 1:
        return refs[0][...]
    return jnp.where(pl.program_id(0) < P_TOK // tm, refs[0][...], refs[1][...])


def _launch(parts, grid, sem, name):
    n_in = [len(p["args"]) for p in parts]
    n_out = [len(p["out_shape"]) for p in parts]
    n_scr = [len(p["scratch"]) for p in parts]

    def kern(*refs):
        ins = refs[:sum(n_in)]
        outs = refs[sum(n_in):sum(n_in) + sum(n_out)]
        scr = refs[sum(n_in) + sum(n_out):]
        i0 = o0 = s0 = 0
        for p, ni, no, ns in zip(parts, n_in, n_out, n_scr):
            p["kernel"](*ins[i0:i0 + ni], *outs[o0:o0 + no], *scr[s0:s0 + ns])
            i0, o0, s0 = i0 + ni, o0 + no, s0 + ns

    alias = {}
    i0 = o0 = 0
    for p, ni, no in zip(parts, n_in, n_out):
        alias.update({i0 + k: o0 + v for k, v in p["alias"].items()})
        i0, o0 = i0 + ni, o0 + no
    cat = lambda key: [v for p in parts for v in p[key]]
    outs = pl.pallas_call(
        kern, grid=grid, in_specs=cat("in_specs"), out_specs=cat("out_specs"),
        out_shape=cat("out_shape"), scratch_shapes=cat("scratch"),
        input_output_aliases=alias, compiler_params=_params(*sem), name=name,
    )(*cat("args"))
    res, o0 = [], 0
    for no in n_out:
        res.append(list(outs[o0:o0 + no]))
        o0 += no
    return res


def _ada_kernel(c_ref, w_ref, b_ref, o_ref):
    s = _silu(c_ref[...]).astype(BF16)
    o_ref[...] = _dot(s, w_ref[...].astype(BF16)) + b_ref[...]


def _ada_call(cvec, w_ada, b_ada):
    tn = 2048
    return pl.pallas_call(
        _ada_kernel,
        grid=(DEPTH, 6 * D_MODEL // tn),
        in_specs=[
            pl.BlockSpec((N_MOD, D_MODEL), lambda l, j: (0, 0)),
            pl.BlockSpec((None, D_MODEL, tn), lambda l, j: (l, 0, j)),
            pl.BlockSpec((None, 1, tn), lambda l, j: (l, 0, j)),
        ],
        out_specs=pl.BlockSpec((None, N_MOD, tn), lambda l, j: (l, 0, j)),
        out_shape=jax.ShapeDtypeStruct((DEPTH, N_MOD, 6 * D_MODEL), F32),
        compiler_params=_params("parallel", "parallel"),
        name="adaln",
    )(cvec, w_ada, b_ada.reshape(DEPTH, 1, 6 * D_MODEL))


def _regroup_kernel(w_ref, o_ref):
    o1, o2, o3 = SSD_INNER, SSD_INNER + CONV_CH, SSD_INNER + CONV_CH + 2 * SSD_HEADS
    o_ref[:, 0:CONV_CH] = w_ref[:, o1:o2]
    o_ref[:, CONV_CH:PA_W] = w_ref[:, 0:o1]
    tail = w_ref[:, o2:].astype(F32)
    o_ref[:, PA_W:PA_W + PB_W] = tail[:, o3 - o2:o3 - o2 + PB_W].astype(BF16)
    lane = lax.broadcasted_iota(jnp.int32, (1, DT_W), 1)
    o_ref[:, PA_W + PB_W:] = jnp.where(lane < o3 - o2, tail[:, 0:DT_W], 0.0).astype(BF16)


def _regroup_call(w_in_ab):
    tr = 128
    n_in = PA_W + PB_W + DT_W
    w_in_ab = jnp.pad(w_in_ab, ((0, 0), (0, 0), (0, n_in - w_in_ab.shape[-1]))).astype(BF16)
    return pl.pallas_call(
        _regroup_kernel,
        grid=(N_EVEN, D_MODEL // tr),
        in_specs=[pl.BlockSpec((None, tr, n_in), lambda e, i: (e, i, 0))],
        out_specs=pl.BlockSpec((None, tr, PA_W + PB_W + DT_W), lambda e, i: (e, i, 0)),
        out_shape=jax.ShapeDtypeStruct((N_EVEN, D_MODEL, PA_W + PB_W + DT_W), BF16),
        compiler_params=_params("parallel", "parallel"),
        name="regroup_w_in",
    )(w_in_ab)


def _inproj_kernel(*refs, n_x, tm, all_slots):
    x_refs = refs[:n_x]
    mod_ref, g_ref, w_ref, xbc_ref, z_ref, pb_ref, dt_ref, kn_ref, vn_ref = refs[n_x:]
    x = _tok_load(x_refs, tm)
    h = _rms(x) * g_ref[...] * (1.0 + mod_ref[1:2, :]) + mod_ref[0:1, :]
    hb = h.astype(BF16)
    tn = ATT_INNER
    for j in range(CONV_CH // tn):
        res = _dot(hb, w_ref[:, j * tn:(j + 1) * tn])
        for s in range(tn // LANE):
            xbc_ref[j * (tn // LANE) + s] = res[:, s * LANE:(s + 1) * LANE]
    z_ref[...] = _dot(hb, w_ref[:, CONV_CH:PA_W])
    q = _dot(hb, w_ref[:, PA_W:PA_W + tn])
    pb_ref[:, 0:tn] = q.astype(BF16)
    k = _dot(hb, w_ref[:, PA_W + tn:PA_W + 2 * tn])
    pb_ref[:, tn:2 * tn] = k.astype(BF16)
    v = _dot(hb, w_ref[:, PA_W + 2 * tn:PA_W + 3 * tn])
    pb_ref[:, 2 * tn:3 * tn] = v.astype(BF16)
    dt_ref[...] = _dot(hb, w_ref[:, PA_W + PB_W:PA_W + PB_W + DT_W])

    @pl.when(pl.program_id(0) < P_TOK // tm)
    def _():
        for new_ref, val in ((kn_ref, k), (vn_ref, v)):
            val = val.reshape(tm // SEQ, SEQ, ATT_INNER)
            if all_slots:
                new_ref[:, 0] = val
                for later in range(1, N_EVEN):
                    new_ref[:, later] = jnp.zeros_like(val)
            else:
                new_ref[...] = val


def _inproj_call(x, mod, l, e, gain, w_in, kv_prev):
    tm = 512
    row = _mod_row(tm)
    npt = P_TOK // tm
    n_x = len(x)
    args = [*x, mod, gain, w_in]
    in_specs = _tok_specs(x, tm) + [
        pl.BlockSpec((None, None, 6, D_MODEL), lambda i: (l, row(i), 0, 0)),
        pl.BlockSpec((1, D_MODEL), lambda i: (0, 0)),
        pl.BlockSpec((None, D_MODEL, PA_W + PB_W + DT_W), lambda i: (e, 0, 0),
                     pipeline_mode=pl.Buffered(1)),
    ]
    kern = functools.partial(_inproj_kernel, n_x=n_x, tm=tm, all_slots=kv_prev is None)
    if kv_prev is None:
        io_alias = {}
        kv_spec = pl.BlockSpec((tm // SEQ, N_EVEN, SEQ, ATT_INNER),
                               lambda i: (jnp.minimum(i, npt - 1), 0, 0, 0))
    else:
        io_alias = {n_x + 3: 4, n_x + 4: 5}
        args += list(kv_prev)
        in_specs += [pl.BlockSpec(memory_space=pl.ANY)] * 2
        kern = _drop_args(kern, n_x + 3, 2)
        kv_spec = pl.BlockSpec((tm // SEQ, None, SEQ, ATT_INNER),
                               lambda i: (jnp.minimum(i, npt - 1), e, 0, 0))
    kv_shape = jax.ShapeDtypeStruct((BATCH, N_EVEN, SEQ, ATT_INNER), F32)
    return pl.pallas_call(
        kern,
        grid=(N_TOK // tm,),
        in_specs=in_specs,
        out_specs=[
            pl.BlockSpec((N_SLAB, tm, LANE), lambda i: (0, i, 0)),
            pl.BlockSpec((tm, SSD_INNER), lambda i: (i, 0)),
            pl.BlockSpec((tm, PB_W), lambda i: (i, 0)),
            pl.BlockSpec((tm, DT_W), lambda i: (i, 0)),
            kv_spec, kv_spec,
        ],
        out_shape=[
            jax.ShapeDtypeStruct((N_SLAB, N_TOK, LANE), F32),
            jax.ShapeDtypeStruct((N_TOK, SSD_INNER), F32),
            jax.ShapeDtypeStruct((N_TOK, PB_W), BF16),
            jax.ShapeDtypeStruct((N_TOK, DT_W), F32),
            kv_shape, kv_shape,
        ],
        input_output_aliases=io_alias,
        compiler_params=_params("arbitrary", vmem=VMEM_LIMIT_INPROJ),
        name="inproj",
    )(*args)


def _ssd_kernel(*refs, L, has_h0, emit_state, all_slots):
    it = iter(refs)
    xbc_ref, z_ref, dt_ref = next(it), next(it), next(it)
    convw_ref, convb_ref, dtb_ref, alog_ref, dskip_ref, norm_ref = (next(it) for _ in range(6))
    sel_refs = (next(it), next(it))
    if has_h0:
        h0_refs = (next(it), next(it))
    y_ref = next(it)
    if emit_state:
        hout_refs = (next(it), next(it))
    (xact, xdt_f, xdt_b, bt, cact, tmp, yacc, dtv, state_f, state_b,
     acs_s, acst_s, cb_s) = (next(it) for _ in range(13))

    nc = L // CHUNK
    nq = L // SUBLANE
    half_w = CONV_W // 2

    rq = lax.broadcasted_iota(jnp.int32, (QBLK, LANE), 0)

    def phase(s, pe, q0):
        r0 = SUBLANE * q0 + pe
        if r0 < 0:
            a = xbc_ref[s, pl.ds(r0 + SUBLANE, QBLK, stride=SUBLANE), :]
            return jnp.where(rq == 0, 0.0, pltpu.roll(a, 1, 0))
        if r0 + SUBLANE * (QBLK - 1) >= L:
            a = xbc_ref[s, pl.ds(r0 - SUBLANE, QBLK, stride=SUBLANE), :]
            return jnp.where(rq == QBLK - 1, 0.0, pltpu.roll(a, QBLK - 1, 0))
        return xbc_ref[s, pl.ds(r0, QBLK, stride=SUBLANE), :]

    def conv_slab(s, dst, d_idx):
        ls = slice(s * LANE, (s + 1) * LANE)
        for q0 in range(0, nq, QBLK):
            ext = {pe: phase(s, pe, q0) for pe in range(-half_w, SUBLANE + half_w)}
            for p in range(SUBLANE):
                acc = jnp.broadcast_to(convb_ref[:, ls], (QBLK, LANE))
                for k in range(CONV_W):
                    acc = acc + convw_ref[k:k + 1, ls] * ext[p + k - half_w]
                dst[d_idx, pl.ds(SUBLANE * q0 + p, QBLK, stride=SUBLANE), :] = _silu(acc)

    for m in range(NH2):
        conv_slab(m, xact, m)
    for g in range(SSD_GROUPS):
        conv_slab(NH2 + g, tmp, g)
    for c in range(nc):
        for g in range(SSD_GROUPS):
            bt[c, g * D_STATE:(g + 1) * D_STATE, :] = (
                tmp[g, c * CHUNK:(c + 1) * CHUNK, :].T.astype(BF16))
    for g in range(SSD_GROUPS):
        conv_slab(NH2 + SSD_GROUPS + g, tmp, g)
    for c in range(nc):
        rows = slice(c * CHUNK, (c + 1) * CHUNK)
        for g in range(SSD_GROUPS):
            cact[g, rows, :] = tmp[g, rows, :].astype(BF16)

    dr = dt_ref[...] + dtb_ref[...]
    dtv[...] = jnp.maximum(dr, 0.0) + jnp.log1p(jnp.exp(-jnp.abs(dr)))
    a_row = -jnp.exp(alog_ref[...]) * math.log2(math.e)

    for c in range(nc):
        rows = slice(c * CHUNK, (c + 1) * CHUNK)
        d = dtv[rows, :]
        d_hi = d.astype(BF16)
        d_lo = (d - d_hi.astype(F32)).astype(BF16)
        d2 = jnp.concatenate([d_hi, d_lo], axis=1)
        spread_f = _dot(d2, sel_refs[0][...])
        spread_b = _dot(d2, sel_refs[1][...])
        for m in range(NH2):
            lanes = slice(m * LANE, (m + 1) * LANE)
            xm = xact[m, rows, :]
            xdt_f[m, rows, :] = (xm * spread_f[:, lanes]).astype(BF16)
            xdt_b[m, rows, :] = (xm * spread_b[:, lanes]).astype(BF16)

    ri = lax.broadcasted_iota(jnp.int32, (CHUNK, CHUNK), 0)
    ci = lax.broadcasted_iota(jnp.int32, (CHUNK, CHUNK), 1)
    low_half = ci < SSD_HD
    low_row = low_half[0:1, :]

    for c in range(nc):
        rows = slice(c * CHUNK, (c + 1) * CHUNK)
        dta = dtv[rows, :] * a_row
        run_f = jnp.dot((ri >= ci).astype(F32), dta, preferred_element_type=F32,
                        precision=lax.Precision.HIGHEST)
        run_b = jnp.dot((ri <= ci).astype(F32), dta, preferred_element_type=F32,
                        precision=lax.Precision.HIGHEST)
        acs = jnp.where(ci < SSD_HEADS, run_f, run_b)
        acs_s[c] = acs
        acst_s[c] = acs.T
        for g in range(SSD_GROUPS):
            cb_s[c, g] = _dot(cact[g, rows, :], bt[c, g * D_STATE:(g + 1) * D_STATE, :])

    for d_idx, st_ref in enumerate((state_f, state_b)):
        for m in range(NH2):
            if has_h0:
                st_ref[m] = h0_refs[d_idx][m].T
            else:
                st_ref[m] = jnp.zeros((D_STATE, LANE), F32)

    def chunk_rows(c):
        return pl.ds(pl.multiple_of(c * CHUNK, CHUNK), CHUNK)

    def chunk_step(c, fwd, add):
        mask = (ri >= ci) if fwd else (ri <= ci)
        col0 = 0 if fwd else SSD_HEADS
        last = CHUNK - 1 if fwd else 0
        xdt = xdt_f if fwd else xdt_b
        state = state_f if fwd else state_b
        rows = chunk_rows(c)
        acs = acs_s[c]
        acs_t = acst_s[c]
        acs_last = acs[last:last + 1, :]
        for g in range(SSD_GROUPS):
            btg = bt[c, g * D_STATE:(g + 1) * D_STATE, :]
            cg = cact[g, rows, :]
            cbm = jnp.where(mask, cb_s[c, g], 0.0)
            for pr in range(2):
                m = g * 2 + pr
                ha, hb = col0 + 2 * m, col0 + 2 * m + 1
                lanes = slice(m * LANE, (m + 1) * LANE)
                xd2 = xdt[m, rows, :]
                acol_a = jnp.broadcast_to(acs[:, ha:ha + 1], (CHUNK, CHUNK))
                acol_b = jnp.broadcast_to(acs[:, hb:hb + 1], (CHUNK, CHUNK))
                w_a = cbm * jnp.exp2(jnp.minimum(acol_a - acs_t[ha:ha + 1, :], 0.0))
                w_b = cbm * jnp.exp2(jnp.minimum(acol_b - acs_t[hb:hb + 1, :], 0.0))
                w2 = jnp.concatenate([w_a.astype(BF16), w_b.astype(BF16)], axis=1)
                zero = jnp.zeros_like(xd2)
                x_blk = jnp.concatenate([jnp.where(low_half, xd2, zero),
                                         jnp.where(low_half, zero, xd2)], axis=0)
                y_in = _dot(w2, x_blk)
                acol2 = jnp.where(low_half, acol_a, acol_b)
                last2 = jnp.where(low_row, acs_last[:, ha:ha + 1], acs_last[:, hb:hb + 1])
                h_prev = state[m]
                y_out = _dot(cg, h_prev.astype(BF16)) * jnp.exp2(acol2)
                to_end = jnp.exp2(last2 - acol2)
                st = _dot(btg, (to_end * xd2.astype(F32)).astype(BF16))
                state[m] = jnp.exp2(last2) * h_prev + st
                y2 = y_in + y_out
                if fwd:
                    y2 = y2 + dskip_ref[:, lanes] * xact[m, rows, :]
                if add:
                    y2 = y2 + yacc[rows, lanes]
                yacc[rows, lanes] = y2

    def finish(c):
        rows = chunk_rows(c)
        yt = yacc[rows, :] * _silu(z_ref[rows, :])
        y_ref[rows, :] = (_rms(yt) * norm_ref[...]).astype(y_ref.dtype)

    def first_half(s, carry):
        chunk_step(s, True, False)
        chunk_step(nc - 1 - s, False, False)
        return carry

    def second_half(s, carry):
        chunk_step(s, True, True)
        chunk_step(nc - 1 - s, False, True)
        finish(s)
        finish(nc - 1 - s)
        return carry

    lax.fori_loop(0, nc // 2, first_half, 0)
    lax.fori_loop(nc // 2, nc, second_half, 0)

    if emit_state:
        for d_idx, st_ref in enumerate((state_f, state_b)):
            for m in range(NH2):
                if all_slots:
                    hout_refs[d_idx][0, m] = st_ref[m].T
                    for later in range(1, N_EVEN):
                        hout_refs[d_idx][later, m] = jnp.zeros((LANE, D_STATE), F32)
                else:
                    hout_refs[d_idx][m] = st_ref[m].T


def _head_spread_tables():
    ch_head = np.arange(SSD_INNER) // SSD_HD
    out = []
    for d in range(2):
        k = np.arange(2 * DT_W) % DT_W
        out.append(jnp.asarray(k[:, None] == (d * SSD_HEADS + ch_head)[None, :], BF16))
    return out


def _ssd_parts(xbc, z, dtarr, states_prev, weights, e, *, L, nb, blk0, h0=None):
    emit_state = states_prev is not None
    all_slots = isinstance(states_prev, str)
    nblk = N_TOK // L
    nc = L // CHUNK
    vec = lambda w: pl.BlockSpec((1, w), lambda b: (0, 0))
    in_specs = [
        pl.BlockSpec((N_SLAB, L, LANE), lambda b: (0, blk0 + b, 0)),
        pl.BlockSpec((None, L, SSD_INNER), lambda b: (blk0 + b, 0, 0)),
        pl.BlockSpec((None, L, DT_W), lambda b: (blk0 + b, 0, 0)),
        pl.BlockSpec((8, CONV_CH), lambda b: (0, 0)),
        vec(CONV_CH), vec(DT_W), vec(DT_W), vec(SSD_INNER), vec(SSD_INNER),
        pl.BlockSpec((2 * DT_W, SSD_INNER), lambda b: (0, 0)),
        pl.BlockSpec((2 * DT_W, SSD_INNER), lambda b: (0, 0)),
    ]
    args = [xbc, z.reshape(nblk, L, SSD_INNER), dtarr.reshape(nblk, L, DT_W), *weights,
            *_head_spread_tables()]
    state_spec = pl.BlockSpec((None, None, NH2, LANE, D_STATE), lambda b: (b, e, 0, 0, 0))
    if h0 is not None:
        for h in h0:
            args.append(h.reshape(DEC_BATCH, N_EVEN, NH2, LANE, D_STATE))
            in_specs.append(state_spec)
    kern = functools.partial(_ssd_kernel, L=L, has_h0=h0 is not None, emit_state=emit_state,
                             all_slots=all_slots)
    out_specs = [pl.BlockSpec((None, L, SSD_INNER), lambda b: (b, 0, 0))]
    out_shape = [jax.ShapeDtypeStruct((nb, L, SSD_INNER), BF16)]
    io_alias = {}
    if emit_state:
        if all_slots:
            out_spec = pl.BlockSpec((None, N_EVEN, NH2, LANE, D_STATE), lambda b: (b, 0, 0, 0, 0))
        else:
            out_spec = state_spec
            io_alias = {len(args): 1, len(args) + 1: 2}
            kern = _drop_args(kern, len(args), 2)
            args += list(states_prev)
            in_specs += [pl.BlockSpec(memory_space=pl.ANY)] * 2
        out_specs += [out_spec, out_spec]
        out_shape += [jax.ShapeDtypeStruct((nb, N_EVEN, NH2, LANE, D_STATE), F32)] * 2
    scratch = [
        pltpu.VMEM((NH2, L, LANE), F32),
        pltpu.VMEM((NH2, L, LANE), BF16),
        pltpu.VMEM((NH2, L, LANE), BF16),
        pltpu.VMEM((nc, SSD_GROUPS * D_STATE, CHUNK), BF16),
        pltpu.VMEM((SSD_GROUPS, L, D_STATE), BF16),
        pltpu.VMEM((SSD_GROUPS, L, D_STATE), F32),
        pltpu.VMEM((L, SSD_INNER), F32),
        pltpu.VMEM((L, DT_W), F32),
        pltpu.VMEM((NH2, D_STATE, LANE), F32),
        pltpu.VMEM((NH2, D_STATE, LANE), F32),
        pltpu.VMEM((nc, CHUNK, LANE), F32),
        pltpu.VMEM((nc, LANE, CHUNK), F32),
        pltpu.VMEM((nc, SSD_GROUPS, CHUNK, CHUNK), F32),
    ]
    return dict(kernel=kern, args=args, in_specs=in_specs, out_specs=out_specs,
                out_shape=out_shape, scratch=scratch, alias=io_alias)


def _rope(x, cos, sin_a, sin_b):
    return (x * cos + pltpu.roll(x, LANE - ROPE_PAIRS, 1) * sin_a
            + pltpu.roll(x, ROPE_PAIRS, 1) * sin_b)


def _attn_kernel(*refs, L, n_ctx, lam_init, stack_maps, q_axis):
    it = iter(refs)
    q_ref, k_ref, v_ref = next(it), next(it), next(it)
    if n_ctx:
        ck_ref, cv_ref = next(it), next(it)
        cq, saq, sbq, ck, sak, sbk = (next(it) for _ in range(6))
    lam_ref, sub_ref = next(it), next(it)
    o_ref = next(it)
    kall, vext = next(it), next(it)
    lk = n_ctx + L
    tq = q_ref.shape[0]

    def prepare_keys_values():
        for h in range(ATT_HEADS):
            hs = slice(h * ATT_VD, (h + 1) * ATT_VD)
            if n_ctx:
                kall[h, 0:n_ctx, :] = ck_ref[:, hs].astype(BF16)
                vext[h, 0:n_ctx, 0:ATT_VD] = cv_ref[:, hs].astype(BF16)
                kh = _rope(k_ref[:, hs].astype(F32), ck[...], sak[...], sbk[...])
                kall[h, n_ctx:lk, :] = kh.astype(BF16)
            else:
                kall[h] = k_ref[:, hs]
            vext[h, n_ctx:lk, 0:ATT_VD] = v_ref[:, hs]
            vext[h, :, ATT_VD:MXU_N] = jnp.ones((lk, MXU_N - ATT_VD), BF16)

    if q_axis is None:
        prepare_keys_values()
    else:
        pl.when(pl.program_id(q_axis) == 0)(prepare_keys_values)

    lane = lax.broadcasted_iota(jnp.int32, (1, ATT_VD), 1)
    scale = ATT_HD ** -0.5 * math.log2(math.e)
    m1 = jnp.where(lane < ATT_HD, scale, 0.0)
    m2 = jnp.where(lane < ATT_HD, 0.0, scale)
    lq = lam_ref[...]
    lam = (jnp.exp(jnp.sum(lq[0:1, :] * lq[1:2, :], axis=-1, keepdims=True))
           - jnp.exp(jnp.sum(lq[2:3, :] * lq[3:4, :], axis=-1, keepdims=True)) + lam_init)

    for h in range(ATT_HEADS):
        hs = slice(h * ATT_VD, (h + 1) * ATT_VD)
        q = q_ref[:, hs].astype(F32)
        if n_ctx:
            q = _rope(q, cq[...], saq[...], sbq[...])
        def attend(qm):
            s = _dot_nt(qm, kall[h])
            p = jnp.exp2(s - jnp.max(s, axis=-1, keepdims=True)).astype(BF16)
            pv = _dot(p, vext[h])
            return pv[:, 0:ATT_VD] / pv[:, ATT_VD:MXU_N]

        q1, q2 = (q * m1).astype(BF16), (q * m2).astype(BF16)
        if stack_maps:
            on = attend(jnp.concatenate([q1, q2], axis=0))
            o = on[0:tq] - lam * on[tq:2 * tq]
        else:
            o = attend(q1) - lam * attend(q2)
        o_ref[:, hs] = (_rms(o) * sub_ref[...] * (1.0 - lam_init)).astype(o_ref.dtype)


def _attn_parts(pb, lam_qk, subln, lam_init, *, L, nb, blk0, tq, ctx=None, rope=None):
    nblk = N_TOK // L
    one_d = tq == L
    spec = lambda shape, f: pl.BlockSpec(shape, (lambda b: f(b, 0)) if one_d else f)
    pb3 = pb.reshape(nblk, L, PB_W)
    in_specs = [
        spec((None, tq, ATT_INNER), lambda b, i: (blk0 + b, i, 0)),
        spec((None, L, ATT_INNER), lambda b, i: (blk0 + b, 0, 1)),
        spec((None, L, ATT_INNER), lambda b, i: (blk0 + b, 0, 2)),
    ]
    args = [pb3, pb3, pb3]
    n_ctx = 0
    if ctx is not None:
        cache_k, cache_v, e = ctx
        n_ctx = cache_k.shape[2]
        for cch in (cache_k, cache_v):
            args.append(cch.reshape(DEC_BATCH, N_EVEN, n_ctx, ATT_INNER))
            in_specs.append(spec((None, None, n_ctx, ATT_INNER), lambda b, i: (b, e, 0, 0)))
        for t in rope:
            args.append(t)
            in_specs.append(spec((tq, ATT_VD), lambda b, i: (i, 0)))
        for t in rope:
            args.append(t)
            in_specs.append(spec((L, ATT_VD), lambda b, i: (0, 0)))
    args += [lam_qk, subln]
    in_specs += [spec((4, ATT_HD), lambda b, i: (0, 0)), spec((1, ATT_VD), lambda b, i: (0, 0))]
    kern = functools.partial(_attn_kernel, L=L, n_ctx=n_ctx, lam_init=lam_init,
                             stack_maps=n_ctx + L <= 2 * MXU_N,
                             q_axis=None if one_d else 1)
    return dict(kernel=kern, args=args, in_specs=in_specs,
                out_specs=[spec((None, tq, ATT_INNER), lambda b, i: (b, i, 0))],
                out_shape=[jax.ShapeDtypeStruct((nb, L, ATT_INNER), BF16)],
                scratch=[pltpu.VMEM((ATT_HEADS, n_ctx + L, ATT_VD), BF16),
                         pltpu.VMEM((ATT_HEADS, n_ctx + L, MXU_N), BF16)],
                alias={})


FFN_SUB = FFN_HIDDEN // MXU_N


def _ffn_kernel(*refs, n_x, tm, final, mixer):
    x_refs, k = refs[:n_x], n_x
    if mixer:
        y_refs, o_refs, wmix_ref, k = refs[k:k + 2], refs[k + 2:k + 4], refs[k + 4], k + 5
    mod_ref, g_ref, wg_ref, wu_ref, wo_ref = refs[k:k + 5]
    rest = list(refs[k + 5:])
    if final:
        gf_ref, op_ref, os_ref = rest[:3]
        rest = rest[3:]
    else:
        out_ref = rest.pop(0)
    h_ref, a_ref, win_s, wout_s, xin_ref = rest
    s = pl.program_id(0)
    tile = jnp.maximum(s - (FFN_SUB - 1), 0)
    npt = P_TOK // tm

    def load(parts):
        if len(parts) == 1:
            return parts[0][...]
        return jnp.where(tile < npt, parts[0][...], parts[1][...])

    def modulated():
        x = load(x_refs)
        if mixer:
            acc = (_dot(load(y_refs), wmix_ref[0:SSD_INNER, :])
                   + _dot(load(o_refs), wmix_ref[SSD_INNER:, :]))
            x = x + mod_ref[2:3, :] * acc
        xin_ref[...] = x
        h = _rms(x) * g_ref[...] * (1.0 + mod_ref[4:5, :]) + mod_ref[3:4, :]
        return h.astype(BF16)

    def hidden(j, hb, wg, wu):
        a_ref[:, j * MXU_N:(j + 1) * MXU_N] = (_silu(_dot(hb, wg)) * _dot(hb, wu)).astype(BF16)

    for j in range(FFN_SUB):
        @pl.when(s == j)
        def _(j=j):
            if j == 0:
                h_ref[...] = modulated()
            cs = slice(j * MXU_N, (j + 1) * MXU_N)
            us = slice(FFN_HIDDEN + j * MXU_N, FFN_HIDDEN + (j + 1) * MXU_N)
            wg, wu = wg_ref[...].astype(BF16), wu_ref[...].astype(BF16)
            win_s[:, cs] = wg
            win_s[:, us] = wu
            wout_s[cs, :] = wo_ref[...].astype(BF16)
            hidden(j, h_ref[...], wg, wu)

    @pl.when(s >= FFN_SUB)
    def _():
        h_ref[...] = modulated()
        for j in range(FFN_SUB):
            cs = slice(j * MXU_N, (j + 1) * MXU_N)
            us = slice(FFN_HIDDEN + j * MXU_N, FFN_HIDDEN + (j + 1) * MXU_N)
            hidden(j, h_ref[...], win_s[:, cs], win_s[:, us])

    @pl.when(s >= FFN_SUB - 1)
    def _():
        x_new = xin_ref[...] + mod_ref[5:6, :] * _dot(a_ref[...], wout_s[...])
        if final:
            y = _rms(x_new) * gf_ref[...]

            @pl.when(tile < npt)
            def _():
                op_ref[...] = y

            @pl.when(tile >= npt)
            def _():
                os_ref[...] = y
        else:
            out_ref[...] = x_new


def _ffn_call(x, mod, l, gain, w_in, w_out, final_gain=None, mixer=None):
    tm = 512
    row = _mod_row(tm)
    npt = P_TOK // tm
    tile = lambda s: jnp.maximum(s - (FFN_SUB - 1), 0)
    chunk = lambda s: jnp.minimum(s, FFN_SUB - 1)
    final = final_gain is not None

    def tok_specs(parts):
        if len(parts) == 1:
            return [pl.BlockSpec((tm, D_MODEL), lambda s: (tile(s), 0))]
        return [pl.BlockSpec((tm, D_MODEL), lambda s: (jnp.minimum(tile(s), npt - 1), 0)),
                pl.BlockSpec((tm, D_MODEL), lambda s: (jnp.maximum(tile(s) - npt, 0), 0))]

    args, in_specs = [*x], tok_specs(x)
    if mixer is not None:
        y, o, w_mix, e = mixer
        args += [*y, *o, w_mix]
        in_specs += tok_specs(y) + tok_specs(o) + [
            pl.BlockSpec((None, SSD_INNER + ATT_INNER, D_MODEL), lambda s: (e, 0, 0),
                         pipeline_mode=pl.Buffered(1))]
    args += [mod, gain, w_in, w_in, w_out]
    in_specs += [
        pl.BlockSpec((None, None, 6, D_MODEL), lambda s: (l, row(tile(s)), 0, 0)),
        pl.BlockSpec((1, D_MODEL), lambda s: (0, 0)),
        pl.BlockSpec((None, D_MODEL, MXU_N), lambda s: (l, 0, chunk(s))),
        pl.BlockSpec((None, D_MODEL, MXU_N), lambda s: (l, 0, FFN_SUB + chunk(s))),
        pl.BlockSpec((None, MXU_N, D_MODEL), lambda s: (l, chunk(s), 0)),
    ]
    if final:
        args.append(final_gain)
        in_specs.append(pl.BlockSpec((1, D_MODEL), lambda s: (0, 0)))
        out_specs = [pl.BlockSpec((tm, D_MODEL), lambda s: (jnp.minimum(tile(s), npt - 1), 0)),
                     pl.BlockSpec((tm, D_MODEL), lambda s: (jnp.maximum(tile(s) - npt, 0), 0))]
        out_shape = [jax.ShapeDtypeStruct((P_TOK, D_MODEL), F32),
                     jax.ShapeDtypeStruct((S_TOK, D_MODEL), F32)]
    else:
        out_specs = pl.BlockSpec((tm, D_MODEL), lambda s: (tile(s), 0))
        out_shape = jax.ShapeDtypeStruct((N_TOK, D_MODEL), F32)
    return pl.pallas_call(
        functools.partial(_ffn_kernel, n_x=len(x), tm=tm, final=final, mixer=mixer is not None),
        grid=(FFN_SUB - 1 + N_TOK // tm,),
        in_specs=in_specs,
        out_specs=out_specs,
        out_shape=out_shape,
        scratch_shapes=[pltpu.VMEM((tm, D_MODEL), BF16), pltpu.VMEM((tm, FFN_HIDDEN), BF16),
                        pltpu.VMEM((D_MODEL, 2 * FFN_HIDDEN), BF16),
                        pltpu.VMEM((FFN_HIDDEN, D_MODEL), BF16),
                        pltpu.VMEM((tm, D_MODEL), F32)],
        compiler_params=_params("arbitrary"),
        name="ffn",
    )(*args)


def _four_kernel(x_ref, mod_ref, g_ref, cc_ref, sc_ref, cs_ref, w_ref, b_ref, out_ref, stk_ref, *, L):
    x = x_ref[...]
    h = (_rms(x) * g_ref[...] * (1.0 + mod_ref[1:2, :]) + mod_ref[0:1, :]).astype(BF16)
    for g in range(FOUR_GROUPS):
        ls = slice(g * FOUR_GC, (g + 1) * FOUR_GC)
        hg = h[:, ls]
        stk_ref[0:L, ls] = _dot(hg, cc_ref[...]).astype(BF16)
        stk_ref[L:2 * L, ls] = _dot(hg, sc_ref[...]).astype(BF16)
    f = _dot(cs_ref[...], stk_ref[...]).astype(BF16)
    out_ref[...] = x + mod_ref[2:3, :] * (_dot(f, w_ref[...]) + b_ref[...])


def _dft_tables(L):
    n = FOUR_GC
    ang_c = 2.0 * np.pi * ((np.arange(n)[:, None] * np.arange(n)[None, :]) % n) / n
    ang_l = 2.0 * np.pi * ((np.arange(L)[:, None] * np.arange(L)[None, :]) % L) / L
    cc = np.cos(ang_c) / math.sqrt(n)
    sc = np.sin(ang_c) / math.sqrt(n)
    cs = np.concatenate([np.cos(ang_l), -np.sin(ang_l)], axis=1) / math.sqrt(L)
    return (jnp.asarray(cc, F32).astype(BF16), jnp.asarray(sc, F32).astype(BF16),
            jnp.asarray(cs, F32).astype(BF16))


def _four_call(x, mod, l, i_odd, gain, w, b, *, L, nb, blk0):
    nblk = N_TOK // L
    cc, sc, cs = _dft_tables(L)
    row = _mod_row(L)
    out = pl.pallas_call(
        functools.partial(_four_kernel, L=L),
        grid=(nb,),
        in_specs=[
            pl.BlockSpec((None, L, D_MODEL), lambda i: (blk0 + i, 0, 0)),
            pl.BlockSpec((None, None, 6, D_MODEL), lambda i: (l, row(blk0 + i), 0, 0)),
            pl.BlockSpec((1, D_MODEL), lambda i: (0, 0)),
            pl.BlockSpec((FOUR_GC, FOUR_GC), lambda i: (0, 0)),
            pl.BlockSpec((FOUR_GC, FOUR_GC), lambda i: (0, 0)),
            pl.BlockSpec((L, 2 * L), lambda i: (0, 0)),
            pl.BlockSpec((None, D_MODEL, D_MODEL), lambda i: (i_odd, 0, 0)),
            pl.BlockSpec((1, D_MODEL), lambda i: (0, 0)),
        ],
        out_specs=pl.BlockSpec((None, L, D_MODEL), lambda i: (i, 0, 0)),
        out_shape=jax.ShapeDtypeStruct((nb, L, D_MODEL), F32),
        scratch_shapes=[pltpu.VMEM((2 * L, D_MODEL), BF16)],
        compiler_params=_params("parallel"),
        name=f"fourier_L{L}",
    )(x.reshape(nblk, L, D_MODEL), mod, gain, cc, sc, cs, w, b)
    return out.reshape(nb * L, D_MODEL)


def _rope_lane_tables(L):
    t = jnp.arange(L)
    row = (t // GRID_W).astype(F32)
    col = (t % GRID_W).astype(F32)
    freq = ROPE_THETA ** (-jnp.arange(ROPE_PAIRS, dtype=F32) / ROPE_PAIRS)
    ar = row[:, None] * freq[None]
    ac = col[:, None] * freq[None]
    z = jnp.zeros_like(ar)
    cos32 = lambda a: jnp.concatenate([jnp.cos(a), jnp.cos(a)], axis=-1)
    sa32 = lambda a: jnp.concatenate([-jnp.sin(a), z], axis=-1)
    sb32 = lambda a: jnp.concatenate([z, jnp.sin(a)], axis=-1)
    tile = lambda f: jnp.concatenate([f(ar), f(ac), f(ar), f(ac)], axis=-1)
    return tile(cos32), tile(sa32), tile(sb32)


def _pad_lanes(v, width):
    v = v.reshape(1, -1).astype(F32)
    return jnp.pad(v, ((0, 0), (0, width - v.shape[1])))


def kernel(x_prompt, x_sample, cache_k, cache_v, state_ssd_fwd, state_ssd_bwd, c, c_ctx, w_ada, b_ada, norm_mix, norm_ffn, w_in_ab, conv_w, conv_b, dt_bias, a_log, d_skip, ssd_norm, lambda_qk, subln, w_out_ab, w_four, b_four, w_ffn_in, w_ffn_out, norm_final):
    x = (x_prompt.reshape(P_TOK, D_MODEL), x_sample.reshape(S_TOK, D_MODEL))
    cvec = jnp.concatenate([c_ctx[None, :], c, jnp.zeros((N_MOD - 1 - DEC_BATCH, D_MODEL), F32)], axis=0)
    mod = _ada_call(cvec, w_ada, b_ada).reshape(DEPTH, N_MOD, 6, D_MODEL)
    rope = _rope_lane_tables(DEC_SEQ)

    w_in_bf = _regroup_call(w_in_ab)
    w_out_bf = w_out_ab.astype(BF16)
    w_four_bf = w_four.astype(BF16)

    kv_new = None
    states = "new"
    blk_s = P_TOK // DEC_SEQ
    for l in range(DEPTH):
        gain_mix = norm_mix[l].reshape(1, D_MODEL)
        if l % 2 == 0:
            e = l // 2
            lam_init = 0.8 - 0.6 * math.exp(-0.3 * l)
            xbc, z, pb, dtarr, k_new, v_new = _inproj_call(x, mod, l, e, gain_mix, w_in_bf, kv_new)
            kv_new = (k_new, v_new)

            ssd_w = (jnp.pad(conv_w[e], ((0, 8 - CONV_W), (0, 0))), conv_b[e].reshape(1, CONV_CH),
                     _pad_lanes(dt_bias[e], DT_W), _pad_lanes(a_log[e], DT_W),
                     jnp.repeat(d_skip[e], SSD_HD).reshape(1, SSD_INNER),
                     ssd_norm[e].reshape(1, SSD_INNER))
            sub = subln[e].reshape(1, ATT_VD)
            ((y_p, hf, hb),) = _launch(
                [_ssd_parts(xbc, z, dtarr, states, ssd_w, e, L=SEQ, nb=BATCH, blk0=0)],
                (BATCH,), ("parallel",), "ssd_L256")
            states = (hf, hb)
            ((o_p,),) = _launch(
                [_attn_parts(pb, lambda_qk[e], sub, lam_init, L=SEQ, nb=BATCH, blk0=0, tq=SEQ)],
                (BATCH,), ("parallel",), "attn_L256")
            ((y_s,),) = _launch(
                [_ssd_parts(xbc, z, dtarr, None, ssd_w, e, L=DEC_SEQ, nb=DEC_BATCH, blk0=blk_s,
                            h0=(state_ssd_fwd, state_ssd_bwd))],
                (DEC_BATCH,), ("parallel",), "ssd_L1024")
            ((o_s,),) = _launch(
                [_attn_parts(pb, lambda_qk[e], sub, lam_init, L=DEC_SEQ, nb=DEC_BATCH, blk0=blk_s,
                             tq=256, ctx=(cache_k, cache_v, e), rope=rope)],
                (DEC_BATCH, DEC_SEQ // 256), ("parallel", "arbitrary"), "attn_L1024")
            halves = lambda a, b: (a.reshape(P_TOK, -1), b.reshape(S_TOK, -1))
            mixer = (halves(y_p, y_s), halves(o_p, o_s), w_out_bf, e)
        else:
            mixer = None
            i_odd = l // 2
            b4 = b_four[i_odd].reshape(1, D_MODEL)
            x = (_four_call(x[0], mod, l, i_odd, gain_mix, w_four_bf, b4, L=SEQ, nb=BATCH, blk0=0),
                 _four_call(x[0], mod, l, i_odd, gain_mix, w_four_bf, b4, L=DEC_SEQ, nb=DEC_BATCH,
                            blk0=blk_s))
        last = l == DEPTH - 1
        x = _ffn_call(x, mod, l, norm_ffn[l].reshape(1, D_MODEL), w_ffn_in, w_ffn_out,
                      final_gain=norm_final.reshape(1, D_MODEL) if last else None, mixer=mixer)
        x = tuple(x) if last else (x,)

    y_p, y_s = x
    return (y_p.reshape(BATCH, SEQ, D_MODEL), y_s.reshape(DEC_BATCH, DEC_SEQ, D_MODEL),
            kv_new[0].reshape(BATCH, N_EVEN, SEQ, ATT_HEADS, 2, ATT_HD),
            kv_new[1].reshape(BATCH, N_EVEN, SEQ, ATT_HEADS, ATT_VD),
            states[0].reshape(BATCH, N_EVEN, SSD_HEADS, SSD_HD, D_STATE),
            states[1].reshape(BATCH, N_EVEN, SSD_HEADS, SSD_HD, D_STATE))
```

```python
import functools
import math

import numpy as np
import jax
import jax.numpy as jnp
from jax import lax
from jax.experimental import pallas as pl
from jax.experimental.pallas import tpu as pltpu

D_MODEL = 1024
BATCH = 16
SEQ = 256
DEPTH = 4
N_EVEN = 2
DEC_BATCH = 4
DEC_SEQ = 1024
PAST_LEN = 512
GRID_W = 64
SSD_HEADS = 16
SSD_HD = 64
SSD_INNER = 1024
SSD_GROUPS = 4
D_STATE = 128
CONV_W = 5
CONV_CH = 2048
CHUNK = 128
ATT_HEADS = 8
ATT_HD = 64
ATT_VD = 128
ATT_INNER = 1024
ROPE_THETA = 10000.0
ROPE_PAIRS = 16
FOUR_GROUPS = 4
FOUR_GC = D_MODEL // FOUR_GROUPS
FFN_HIDDEN = 2816
EPS = 1e-6

P_TOK = BATCH * SEQ
S_TOK = DEC_BATCH * DEC_SEQ
N_TOK = P_TOK + S_TOK
N_MOD = 8
PA_W = CONV_CH + SSD_INNER
PB_W = 3 * ATT_INNER
DT_W = 128
NH2 = SSD_HEADS // 2
LANE = 128
SUBLANE = 8
N_SLAB = CONV_CH // LANE
QBLK = 32
MXU_N = 256
VMEM_LIMIT = 56 * 1024 * 1024
VMEM_LIMIT_INPROJ = 61 * 1024 * 1024

F32 = jnp.float32
BF16 = jnp.bfloat16


def _params(*sem, vmem=VMEM_LIMIT):
    return pltpu.CompilerParams(dimension_semantics=sem, vmem_limit_bytes=vmem)


def _dot(a, b):
    return jnp.dot(a, b, preferred_element_type=F32)


def _dot_nt(a, b):
    return lax.dot_general(a, b, (((1,), (1,)), ((), ())), preferred_element_type=F32)


def _silu(x):
    h = 0.5 * x
    return h + h * jnp.tanh(h)


def _rms(x):
    return x * lax.rsqrt(jnp.mean(x * x, axis=-1, keepdims=True) + EPS)


def _mod_row(tm):
    return lambda i: jnp.maximum((i * tm) // DEC_SEQ - (P_TOK // DEC_SEQ - 1), 0)


def _drop_args(kern, start, n):
    def wrapped(*refs):
        return kern(*(refs[:start] + refs[start + n:]))
    return wrapped


def _tok_specs(parts, tm):
    width = parts[0].shape[-1]
    if len(parts) == 1:
        return [pl.BlockSpec((tm, width), lambda i: (i, 0))]
    npt = P_TOK // tm
    return [pl.BlockSpec((tm, width), lambda i: (jnp.minimum(i, npt - 1), 0)),
            pl.BlockSpec((tm, width), lambda i: (jnp.maximum(i - npt, 0), 0))]


def _tok_load(refs, tm):
    if len(refs) == 1:
        return refs[0][...]
    return jnp.where(pl.program_id(0) < P_TOK // tm, refs[0][...], refs[1][...])


def _launch(parts, grid, sem, name):
    n_in = [len(p["args"]) for p in parts]
    n_out = [len(p["out_shape"]) for p in parts]
    n_scr = [len(p["scratch"]) for p in parts]

    def kern(*refs):
        ins = refs[:sum(n_in)]
        outs = refs[sum(n_in):sum(n_in) + sum(n_out)]
        scr = refs[sum(n_in) + sum(n_out):]
        i0 = o0 = s0 = 0
        for p, ni, no, ns in zip(parts, n_in, n_out, n_scr):
            p["kernel"](*ins[i0:i0 + ni], *outs[o0:o0 + no], *scr[s0:s0 + ns])
            i0, o0, s0 = i0 + ni, o0 + no, s0 + ns

    alias = {}
    i0 = o0 = 0
    for p, ni, no in zip(parts, n_in, n_out):
        alias.update({i0 + k: o0 + v for k, v in p["alias"].items()})
        i0, o0 = i0 + ni, o0 + no
    cat = lambda key: [v for p in parts for v in p[key]]
    outs = pl.pallas_call(
        kern, grid=grid, in_specs=cat("in_specs"), out_specs=cat("out_specs"),
        out_shape=cat("out_shape"), scratch_shapes=cat("scratch"),
        input_output_aliases=alias, compiler_params=_params(*sem), name=name,
    )(*cat("args"))
    res, o0 = [], 0
    for no in n_out:
        res.append(list(outs[o0:o0 + no]))
        o0 += no
    return res


def _ada_kernel(c_ref, w_ref, b_ref, o_ref):
    s = _silu(c_ref[...]).astype(BF16)
    o_ref[...] = _dot(s, w_ref[...].astype(BF16)) + b_ref[...]


def _ada_call(cvec, w_ada, b_ada):
    tn = 2048
    return pl.pallas_call(
        _ada_kernel,
        grid=(DEPTH, 6 * D_MODEL // tn),
        in_specs=[
            pl.BlockSpec((N_MOD, D_MODEL), lambda l, j: (0, 0)),
            pl.BlockSpec((None, D_MODEL, tn), lambda l, j: (l, 0, j)),
            pl.BlockSpec((None, 1, tn), lambda l, j: (l, 0, j)),
        ],
        out_specs=pl.BlockSpec((None, N_MOD, tn), lambda l, j: (l, 0, j)),
        out_shape=jax.ShapeDtypeStruct((DEPTH, N_MOD, 6 * D_MODEL), F32),
        compiler_params=_params("parallel", "parallel"),
        name="adaln",
    )(cvec, w_ada, b_ada.reshape(DEPTH, 1, 6 * D_MODEL))


def _inproj_kernel(*refs, n_x, tm, all_slots):
    x_refs = refs[:n_x]
    (mod_ref, g_ref, wzx_ref, wqkv_ref, wdt_ref,
     xbc_ref, z_ref, pb_ref, dt_ref, kn_ref, vn_ref) = refs[n_x:]
    x = _tok_load(x_refs, tm)
    h = _rms(x) * g_ref[...] * (1.0 + mod_ref[1:2, :]) + mod_ref[0:1, :]
    hb = h.astype(BF16)
    tn = ATT_INNER
    for j in range(CONV_CH // tn):
        res = _dot(hb, wzx_ref[:, SSD_INNER + j * tn:SSD_INNER + (j + 1) * tn])
        for s in range(tn // LANE):
            xbc_ref[j * (tn // LANE) + s] = res[:, s * LANE:(s + 1) * LANE]
    z_ref[...] = _dot(hb, wzx_ref[:, 0:SSD_INNER])
    q = _dot(hb, wqkv_ref[:, 0:tn])
    pb_ref[:, 0:tn] = q.astype(BF16)
    k = _dot(hb, wqkv_ref[:, tn:2 * tn])
    pb_ref[:, tn:2 * tn] = k.astype(BF16)
    v = _dot(hb, wqkv_ref[:, 2 * tn:3 * tn])
    pb_ref[:, 2 * tn:3 * tn] = v.astype(BF16)
    dt_ref[...] = _dot(hb, wdt_ref[...])

    @pl.when(pl.program_id(0) < P_TOK // tm)
    def _():
        for new_ref, val in ((kn_ref, k), (vn_ref, v)):
            val = val.reshape(tm // SEQ, SEQ, ATT_INNER)
            if all_slots:
                new_ref[:, 0] = val
                for later in range(1, N_EVEN):
                    new_ref[:, later] = jnp.zeros_like(val)
            else:
                new_ref[...] = val


def _inproj_call(x, mod, l, e, gain, w_in, kv_prev):
    tm = 512
    row = _mod_row(tm)
    npt = P_TOK // tm
    n_x = len(x)
    once = pl.Buffered(1)
    args = [*x, mod, gain, *w_in]
    in_specs = _tok_specs(x, tm) + [
        pl.BlockSpec((None, None, 6, D_MODEL), lambda i: (l, row(i), 0, 0)),
        pl.BlockSpec((1, D_MODEL), lambda i: (0, 0)),
        pl.BlockSpec((None, D_MODEL, PA_W), lambda i: (e, 0, 0), pipeline_mode=once),
        pl.BlockSpec((None, D_MODEL, PB_W), lambda i: (e, 0, 0), pipeline_mode=once),
        pl.BlockSpec((None, D_MODEL, DT_W), lambda i: (e, 0, 0), pipeline_mode=once),
    ]
    n_in = len(args)
    kern = functools.partial(_inproj_kernel, n_x=n_x, tm=tm, all_slots=kv_prev is None)
    if kv_prev is None:
        io_alias = {}
        kv_spec = pl.BlockSpec((tm // SEQ, N_EVEN, SEQ, ATT_INNER),
                               lambda i: (jnp.minimum(i, npt - 1), 0, 0, 0))
    else:
        io_alias = {n_in: 4, n_in + 1: 5}
        args += list(kv_prev)
        in_specs += [pl.BlockSpec(memory_space=pl.ANY)] * 2
        kern = _drop_args(kern, n_in, 2)
        kv_spec = pl.BlockSpec((tm // SEQ, None, SEQ, ATT_INNER),
                               lambda i: (jnp.minimum(i, npt - 1), e, 0, 0))
    kv_shape = jax.ShapeDtypeStruct((BATCH, N_EVEN, SEQ, ATT_INNER), F32)
    return pl.pallas_call(
        kern,
        grid=(N_TOK // tm,),
        in_specs=in_specs,
        out_specs=[
            pl.BlockSpec((N_SLAB, tm, LANE), lambda i: (0, i, 0)),
            pl.BlockSpec((tm, SSD_INNER), lambda i: (i, 0)),
            pl.BlockSpec((tm, PB_W), lambda i: (i, 0)),
            pl.BlockSpec((tm, DT_W), lambda i: (i, 0)),
            kv_spec, kv_spec,
        ],
        out_shape=[
            jax.ShapeDtypeStruct((N_SLAB, N_TOK, LANE), F32),
            jax.ShapeDtypeStruct((N_TOK, SSD_INNER), F32),
            jax.ShapeDtypeStruct((N_TOK, PB_W), BF16),
            jax.ShapeDtypeStruct((N_TOK, DT_W), F32),
            kv_shape, kv_shape,
        ],
        input_output_aliases=io_alias,
        compiler_params=_params("arbitrary", vmem=VMEM_LIMIT_INPROJ),
        name="inproj",
    )(*args)


def _ssd_kernel(*refs, L, has_h0, emit_state, all_slots):
    it = iter(refs)
    xbc_ref, z_ref, dt_ref = next(it), next(it), next(it)
    convw_ref, convb_ref, dtb_ref, alog_ref, dskip_ref, norm_ref = (next(it) for _ in range(6))
    sel_refs = (next(it), next(it))
    if has_h0:
        h0_refs = (next(it), next(it))
    y_ref = next(it)
    if emit_state:
        hout_refs = (next(it), next(it))
    (xact, xdt_f, xdt_b, bt, cact, tmp, yacc, dtv, state_f, state_b,
     acs_s, acst_s, cb_s) = (next(it) for _ in range(13))

    nc = L // CHUNK
    nq = L // SUBLANE
    half_w = CONV_W // 2

    rq = lax.broadcasted_iota(jnp.int32, (QBLK, LANE), 0)

    def phase(s, pe, q0):
        r0 = SUBLANE * q0 + pe
        if r0 < 0:
            a = xbc_ref[s, pl.ds(r0 + SUBLANE, QBLK, stride=SUBLANE), :]
            return jnp.where(rq == 0, 0.0, pltpu.roll(a, 1, 0))
        if r0 + SUBLANE * (QBLK - 1) >= L:
            a = xbc_ref[s, pl.ds(r0 - SUBLANE, QBLK, stride=SUBLANE), :]
            return jnp.where(rq == QBLK - 1, 0.0, pltpu.roll(a, QBLK - 1, 0))
        return xbc_ref[s, pl.ds(r0, QBLK, stride=SUBLANE), :]

    def conv_slab(s, dst, d_idx):
        ls = slice(s * LANE, (s + 1) * LANE)
        for q0 in range(0, nq, QBLK):
            ext = {pe: phase(s, pe, q0) for pe in range(-half_w, SUBLANE + half_w)}
            for p in range(SUBLANE):
                acc = jnp.broadcast_to(convb_ref[:, ls], (QBLK, LANE))
                for k in range(CONV_W):
                    acc = acc + convw_ref[k:k + 1, ls] * ext[p + k - half_w]
                dst[d_idx, pl.ds(SUBLANE * q0 + p, QBLK, stride=SUBLANE), :] = _silu(acc)

    for m in range(NH2):
        conv_slab(m, xact, m)
    for g in range(SSD_GROUPS):
        conv_slab(NH2 + g, tmp, g)
    for c in range(nc):
        for g in range(SSD_GROUPS):
            bt[c, g * D_STATE:(g + 1) * D_STATE, :] = (
                tmp[g, c * CHUNK:(c + 1) * CHUNK, :].T.astype(BF16))
    for g in range(SSD_GROUPS):
        conv_slab(NH2 + SSD_GROUPS + g, tmp, g)
    for c in range(nc):
        rows = slice(c * CHUNK, (c + 1) * CHUNK)
        for g in range(SSD_GROUPS):
            cact[g, rows, :] = tmp[g, rows, :].astype(BF16)

    dr = dt_ref[...] + dtb_ref[...]
    dtv[...] = jnp.maximum(dr, 0.0) + jnp.log1p(jnp.exp(-jnp.abs(dr)))
    a_row = -jnp.exp(alog_ref[...]) * math.log2(math.e)

    for c in range(nc):
        rows = slice(c * CHUNK, (c + 1) * CHUNK)
        d = dtv[rows, :]
        d_hi = d.astype(BF16)
        d_lo = (d - d_hi.astype(F32)).astype(BF16)
        d2 = jnp.concatenate([d_hi, d_lo], axis=1)
        spread_f = _dot(d2, sel_refs[0][...])
        spread_b = _dot(d2, sel_refs[1][...])
        for m in range(NH2):
            lanes = slice(m * LANE, (m + 1) * LANE)
            xm = xact[m, rows, :]
            xdt_f[m, rows, :] = (xm * spread_f[:, lanes]).astype(BF16)
            xdt_b[m, rows, :] = (xm * spread_b[:, lanes]).astype(BF16)

    ri = lax.broadcasted_iota(jnp.int32, (CHUNK, CHUNK), 0)
    ci = lax.broadcasted_iota(jnp.int32, (CHUNK, CHUNK), 1)
    low_half = ci < SSD_HD
    low_row = low_half[0:1, :]

    for c in range(nc):
        rows = slice(c * CHUNK, (c + 1) * CHUNK)
        dta = dtv[rows, :] * a_row
        run_f = jnp.dot((ri >= ci).astype(F32), dta, preferred_element_type=F32,
                        precision=lax.Precision.HIGHEST)
        run_b = jnp.dot((ri <= ci).astype(F32), dta, preferred_element_type=F32,
                        precision=lax.Precision.HIGHEST)
        acs = jnp.where(ci < SSD_HEADS, run_f, run_b)
        acs_s[c] = acs
        acst_s[c] = acs.T
        for g in range(SSD_GROUPS):
            cb_s[c, g] = _dot(cact[g, rows, :], bt[c, g * D_STATE:(g + 1) * D_STATE, :])

    for d_idx, st_ref in enumerate((state_f, state_b)):
        for m in range(NH2):
            if has_h0:
                st_ref[m] = h0_refs[d_idx][m].T
            else:
                st_ref[m] = jnp.zeros((D_STATE, LANE), F32)

    def chunk_rows(c):
        return pl.ds(pl.multiple_of(c * CHUNK, CHUNK), CHUNK)

    def chunk_step(c, fwd, add):
        mask = (ri >= ci) if fwd else (ri <= ci)
        col0 = 0 if fwd else SSD_HEADS
        last = CHUNK - 1 if fwd else 0
        xdt = xdt_f if fwd else xdt_b
        state = state_f if fwd else state_b
        rows = chunk_rows(c)
        acs = acs_s[c]
        acs_t = acst_s[c]
        acs_last = acs[last:last + 1, :]
        for g in range(SSD_GROUPS):
            btg = bt[c, g * D_STATE:(g + 1) * D_STATE, :]
            cg = cact[g, rows, :]
            cbm = jnp.where(mask, cb_s[c, g], 0.0)
            for pr in range(2):
                m = g * 2 + pr
                ha, hb = col0 + 2 * m, col0 + 2 * m + 1
                lanes = slice(m * LANE, (m + 1) * LANE)
                xd2 = xdt[m, rows, :]
                acol_a = jnp.broadcast_to(acs[:, ha:ha + 1], (CHUNK, CHUNK))
                acol_b = jnp.broadcast_to(acs[:, hb:hb + 1], (CHUNK, CHUNK))
                w_a = cbm * jnp.exp2(jnp.minimum(acol_a - acs_t[ha:ha + 1, :], 0.0))
                w_b = cbm * jnp.exp2(jnp.minimum(acol_b - acs_t[hb:hb + 1, :], 0.0))
                w2 = jnp.concatenate([w_a.astype(BF16), w_b.astype(BF16)], axis=1)
                zero = jnp.zeros_like(xd2)
                x_blk = jnp.concatenate([jnp.where(low_half, xd2, zero),
                                         jnp.where(low_half, zero, xd2)], axis=0)
                y_in = _dot(w2, x_blk)
                acol2 = jnp.where(low_half, acol_a, acol_b)
                last2 = jnp.where(low_row, acs_last[:, ha:ha + 1], acs_last[:, hb:hb + 1])
                h_prev = state[m]
                y_out = _dot(cg, h_prev.astype(BF16)) * jnp.exp2(acol2)
                to_end = jnp.exp2(last2 - acol2)
                st = _dot(btg, (to_end * xd2.astype(F32)).astype(BF16))
                state[m] = jnp.exp2(last2) * h_prev + st
                y2 = y_in + y_out
                if fwd:
                    y2 = y2 + dskip_ref[:, lanes] * xact[m, rows, :]
                if add:
                    y2 = y2 + yacc[rows, lanes]
                yacc[rows, lanes] = y2

    def finish(c):
        rows = chunk_rows(c)
        yt = yacc[rows, :] * _silu(z_ref[rows, :])
        y_ref[rows, :] = (_rms(yt) * norm_ref[...]).astype(y_ref.dtype)

    def first_half(s, carry):
        chunk_step(s, True, False)
        chunk_step(nc - 1 - s, False, False)
        return carry

    def second_half(s, carry):
        chunk_step(s, True, True)
        chunk_step(nc - 1 - s, False, True)
        finish(s)
        finish(nc - 1 - s)
        return carry

    lax.fori_loop(0, nc // 2, first_half, 0)
    lax.fori_loop(nc // 2, nc, second_half, 0)

    if emit_state:
        for d_idx, st_ref in enumerate((state_f, state_b)):
            for m in range(NH2):
                if all_slots:
                    hout_refs[d_idx][0, m] = st_ref[m].T
                    for later in range(1, N_EVEN):
                        hout_refs[d_idx][later, m] = jnp.zeros((LANE, D_STATE), F32)
                else:
                    hout_refs[d_idx][m] = st_ref[m].T


def _head_spread_tables():
    ch_head = np.arange(SSD_INNER) // SSD_HD
    out = []
    for d in range(2):
        k = np.arange(2 * DT_W) % DT_W
        out.append(jnp.asarray(k[:, None] == (d * SSD_HEADS + ch_head)[None, :], BF16))
    return out


def _ssd_parts(xbc, z, dtarr, states_prev, weights, e, *, L, nb, blk0, h0=None):
    emit_state = states_prev is not None
    all_slots = isinstance(states_prev, str)
    nblk = N_TOK // L
    nc = L // CHUNK
    vec = lambda w: pl.BlockSpec((1, w), lambda b: (0, 0))
    in_specs = [
        pl.BlockSpec((N_SLAB, L, LANE), lambda b: (0, blk0 + b, 0)),
        pl.BlockSpec((None, L, SSD_INNER), lambda b: (blk0 + b, 0, 0)),
        pl.BlockSpec((None, L, DT_W), lambda b: (blk0 + b, 0, 0)),
        pl.BlockSpec((8, CONV_CH), lambda b: (0, 0)),
        vec(CONV_CH), vec(DT_W), vec(DT_W), vec(SSD_INNER), vec(SSD_INNER),
        pl.BlockSpec((2 * DT_W, SSD_INNER), lambda b: (0, 0)),
        pl.BlockSpec((2 * DT_W, SSD_INNER), lambda b: (0, 0)),
    ]
    args = [xbc, z.reshape(nblk, L, SSD_INNER), dtarr.reshape(nblk, L, DT_W), *weights,
            *_head_spread_tables()]
    state_spec = pl.BlockSpec((None, None, NH2, LANE, D_STATE), lambda b: (b, e, 0, 0, 0))
    if h0 is not None:
        for h in h0:
            args.append(h.reshape(DEC_BATCH, N_EVEN, NH2, LANE, D_STATE))
            in_specs.append(state_spec)
    kern = functools.partial(_ssd_kernel, L=L, has_h0=h0 is not None, emit_state=emit_state,
                             all_slots=all_slots)
    out_specs = [pl.BlockSpec((None, L, SSD_INNER), lambda b: (b, 0, 0))]
    out_shape = [jax.ShapeDtypeStruct((nb, L, SSD_INNER), BF16)]
    io_alias = {}
    if emit_state:
        if all_slots:
            out_spec = pl.BlockSpec((None, N_EVEN, NH2, LANE, D_STATE), lambda b: (b, 0, 0, 0, 0))
        else:
            out_spec = state_spec
            io_alias = {len(args): 1, len(args) + 1: 2}
            kern = _drop_args(kern, len(args), 2)
            args += list(states_prev)
            in_specs += [pl.BlockSpec(memory_space=pl.ANY)] * 2
        out_specs += [out_spec, out_spec]
        out_shape += [jax.ShapeDtypeStruct((nb, N_EVEN, NH2, LANE, D_STATE), F32)] * 2
    scratch = [
        pltpu.VMEM((NH2, L, LANE), F32),
        pltpu.VMEM((NH2, L, LANE), BF16),
        pltpu.VMEM((NH2, L, LANE), BF16),
        pltpu.VMEM((nc, SSD_GROUPS * D_STATE, CHUNK), BF16),
        pltpu.VMEM((SSD_GROUPS, L, D_STATE), BF16),
        pltpu.VMEM((SSD_GROUPS, L, D_STATE), F32),
        pltpu.VMEM((L, SSD_INNER), F32),
        pltpu.VMEM((L, DT_W), F32),
        pltpu.VMEM((NH2, D_STATE, LANE), F32),
        pltpu.VMEM((NH2, D_STATE, LANE), F32),
        pltpu.VMEM((nc, CHUNK, LANE), F32),
        pltpu.VMEM((nc, LANE, CHUNK), F32),
        pltpu.VMEM((nc, SSD_GROUPS, CHUNK, CHUNK), F32),
    ]
    return dict(kernel=kern, args=args, in_specs=in_specs, out_specs=out_specs,
                out_shape=out_shape, scratch=scratch, alias=io_alias)


def _rope(x, cos, sin_a, sin_b):
    return (x * cos + pltpu.roll(x, LANE - ROPE_PAIRS, 1) * sin_a
            + pltpu.roll(x, ROPE_PAIRS, 1) * sin_b)


def _attn_kernel(*refs, L, n_ctx, lam_init, stack_maps, q_axis):
    it = iter(refs)
    q_ref, k_ref, v_ref = next(it), next(it), next(it)
    if n_ctx:
        ck_ref, cv_ref = next(it), next(it)
        cq, saq, sbq, ck, sak, sbk = (next(it) for _ in range(6))
    lam_ref, sub_ref = next(it), next(it)
    o_ref = next(it)
    kall, vext = next(it), next(it)
    lk = n_ctx + L
    tq = q_ref.shape[0]

    def prepare_keys_values():
        for h in range(ATT_HEADS):
            hs = slice(h * ATT_VD, (h + 1) * ATT_VD)
            if n_ctx:
                kall[h, 0:n_ctx, :] = ck_ref[:, hs].astype(BF16)
                vext[h, 0:n_ctx, 0:ATT_VD] = cv_ref[:, hs].astype(BF16)
                kh = _rope(k_ref[:, hs].astype(F32), ck[...], sak[...], sbk[...])
                kall[h, n_ctx:lk, :] = kh.astype(BF16)
            else:
                kall[h] = k_ref[:, hs]
            vext[h, n_ctx:lk, 0:ATT_VD] = v_ref[:, hs]
            vext[h, :, ATT_VD:MXU_N] = jnp.ones((lk, MXU_N - ATT_VD), BF16)

    if q_axis is None:
        prepare_keys_values()
    else:
        pl.when(pl.program_id(q_axis) == 0)(prepare_keys_values)

    lane = lax.broadcasted_iota(jnp.int32, (1, ATT_VD), 1)
    scale = ATT_HD ** -0.5 * math.log2(math.e)
    m1 = jnp.where(lane < ATT_HD, scale, 0.0)
    m2 = jnp.where(lane < ATT_HD, 0.0, scale)
    lq = lam_ref[...]
    lam = (jnp.exp(jnp.sum(lq[0:1, :] * lq[1:2, :], axis=-1, keepdims=True))
           - jnp.exp(jnp.sum(lq[2:3, :] * lq[3:4, :], axis=-1, keepdims=True)) + lam_init)

    for h in range(ATT_HEADS):
        hs = slice(h * ATT_VD, (h + 1) * ATT_VD)
        q = q_ref[:, hs].astype(F32)
        if n_ctx:
            q = _rope(q, cq[...], saq[...], sbq[...])
        def attend(qm):
            s = _dot_nt(qm, kall[h])
            p = jnp.exp2(s - jnp.max(s, axis=-1, keepdims=True)).astype(BF16)
            pv = _dot(p, vext[h])
            return pv[:, 0:ATT_VD] / pv[:, ATT_VD:MXU_N]

        q1, q2 = (q * m1).astype(BF16), (q * m2).astype(BF16)
        if stack_maps:
            on = attend(jnp.concatenate([q1, q2], axis=0))
            o = on[0:tq] - lam * on[tq:2 * tq]
        else:
            o = attend(q1) - lam * attend(q2)
        o_ref[:, hs] = (_rms(o) * sub_ref[...] * (1.0 - lam_init)).astype(o_ref.dtype)


def _attn_parts(pb, lam_qk, subln, lam_init, *, L, nb, blk0, tq, ctx=None, rope=None):
    nblk = N_TOK // L
    one_d = tq == L
    spec = lambda shape, f: pl.BlockSpec(shape, (lambda b: f(b, 0)) if one_d else f)
    pb3 = pb.reshape(nblk, L, PB_W)
    in_specs = [
        spec((None, tq, ATT_INNER), lambda b, i: (blk0 + b, i, 0)),
        spec((None, L, ATT_INNER), lambda b, i: (blk0 + b, 0, 1)),
        spec((None, L, ATT_INNER), lambda b, i: (blk0 + b, 0, 2)),
    ]
    args = [pb3, pb3, pb3]
    n_ctx = 0
    if ctx is not None:
        cache_k, cache_v, e = ctx
        n_ctx = cache_k.shape[2]
        for cch in (cache_k, cache_v):
            args.append(cch.reshape(DEC_BATCH, N_EVEN, n_ctx, ATT_INNER))
            in_specs.append(spec((None, None, n_ctx, ATT_INNER), lambda b, i: (b, e, 0, 0)))
        for t in rope:
            args.append(t)
            in_specs.append(spec((tq, ATT_VD), lambda b, i: (i, 0)))
        for t in rope:
            args.append(t)
            in_specs.append(spec((L, ATT_VD), lambda b, i: (0, 0)))
    args += [lam_qk, subln]
    in_specs += [spec((4, ATT_HD), lambda b, i: (0, 0)), spec((1, ATT_VD), lambda b, i: (0, 0))]
    kern = functools.partial(_attn_kernel, L=L, n_ctx=n_ctx, lam_init=lam_init,
                             stack_maps=n_ctx + L <= 2 * MXU_N,
                             q_axis=None if one_d else 1)
    return dict(kernel=kern, args=args, in_specs=in_specs,
                out_specs=[spec((None, tq, ATT_INNER), lambda b, i: (b, i, 0))],
                out_shape=[jax.ShapeDtypeStruct((nb, L, ATT_INNER), BF16)],
                scratch=[pltpu.VMEM((ATT_HEADS, n_ctx + L, ATT_VD), BF16),
                         pltpu.VMEM((ATT_HEADS, n_ctx + L, MXU_N), BF16)],
                alias={})


FFN_SUB = FFN_HIDDEN // MXU_N


def _ffn_kernel(*refs, n_x, tm, final, mixer):
    x_refs, k = refs[:n_x], n_x
    if mixer:
        y_refs, o_refs, wmix_ref, k = refs[k:k + 2], refs[k + 2:k + 4], refs[k + 4], k + 5
    mod_ref, g_ref, wg_ref, wu_ref, wo_ref = refs[k:k + 5]
    rest = list(refs[k + 5:])
    if final:
        gf_ref, op_ref, os_ref = rest[:3]
        rest = rest[3:]
    else:
        out_ref = rest.pop(0)
    h_ref, a_ref, win_s, wout_s, xin_ref = rest
    s = pl.program_id(0)
    tile = jnp.maximum(s - (FFN_SUB - 1), 0)
    npt = P_TOK // tm

    def load(parts):
        if len(parts) == 1:
            return parts[0][...]
        return jnp.where(tile < npt, parts[0][...], parts[1][...])

    def modulated():
        x = load(x_refs)
        if mixer:
            acc = (_dot(load(y_refs), wmix_ref[0:SSD_INNER, :])
                   + _dot(load(o_refs), wmix_ref[SSD_INNER:, :]))
            x = x + mod_ref[2:3, :] * acc
        xin_ref[...] = x
        h = _rms(x) * g_ref[...] * (1.0 + mod_ref[4:5, :]) + mod_ref[3:4, :]
        return h.astype(BF16)

    def hidden(j, hb, wg, wu):
        a_ref[:, j * MXU_N:(j + 1) * MXU_N] = (_silu(_dot(hb, wg)) * _dot(hb, wu)).astype(BF16)

    for j in range(FFN_SUB):
        @pl.when(s == j)
        def _(j=j):
            if j == 0:
                h_ref[...] = modulated()
            cs = slice(j * MXU_N, (j + 1) * MXU_N)
            us = slice(FFN_HIDDEN + j * MXU_N, FFN_HIDDEN + (j + 1) * MXU_N)
            wg, wu = wg_ref[...].astype(BF16), wu_ref[...].astype(BF16)
            win_s[:, cs] = wg
            win_s[:, us] = wu
            wout_s[cs, :] = wo_ref[...].astype(BF16)
            hidden(j, h_ref[...], wg, wu)

    @pl.when(s >= FFN_SUB)
    def _():
        h_ref[...] = modulated()
        for j in range(FFN_SUB):
            cs = slice(j * MXU_N, (j + 1) * MXU_N)
            us = slice(FFN_HIDDEN + j * MXU_N, FFN_HIDDEN + (j + 1) * MXU_N)
            hidden(j, h_ref[...], win_s[:, cs], win_s[:, us])

    @pl.when(s >= FFN_SUB - 1)
    def _():
        x_new = xin_ref[...] + mod_ref[5:6, :] * _dot(a_ref[...], wout_s[...])
        if final:
            y = _rms(x_new) * gf_ref[...]

            @pl.when(tile < npt)
            def _():
                op_ref[...] = y

            @pl.when(tile >= npt)
            def _():
                os_ref[...] = y
        else:
            out_ref[...] = x_new


def _ffn_call(x, mod, l, gain, w_in, w_out, final_gain=None, mixer=None):
    tm = 512
    row = _mod_row(tm)
    npt = P_TOK // tm
    tile = lambda s: jnp.maximum(s - (FFN_SUB - 1), 0)
    chunk = lambda s: jnp.minimum(s, FFN_SUB - 1)
    final = final_gain is not None

    def tok_specs(parts):
        if len(parts) == 1:
            return [pl.BlockSpec((tm, D_MODEL), lambda s: (tile(s), 0))]
        return [pl.BlockSpec((tm, D_MODEL), lambda s: (jnp.minimum(tile(s), npt - 1), 0)),
                pl.BlockSpec((tm, D_MODEL), lambda s: (jnp.maximum(tile(s) - npt, 0), 0))]

    args, in_specs = [*x], tok_specs(x)
    if mixer is not None:
        y, o, w_mix, e = mixer
        args += [*y, *o, w_mix]
        in_specs += tok_specs(y) + tok_specs(o) + [
            pl.BlockSpec((None, SSD_INNER + ATT_INNER, D_MODEL), lambda s: (e, 0, 0),
                         pipeline_mode=pl.Buffered(1))]
    args += [mod, gain, w_in, w_in, w_out]
    in_specs += [
        pl.BlockSpec((None, None, 6, D_MODEL), lambda s: (l, row(tile(s)), 0, 0)),
        pl.BlockSpec((1, D_MODEL), lambda s: (0, 0)),
        pl.BlockSpec((None, D_MODEL, MXU_N), lambda s: (l, 0, chunk(s))),
        pl.BlockSpec((None, D_MODEL, MXU_N), lambda s: (l, 0, FFN_SUB + chunk(s))),
        pl.BlockSpec((None, MXU_N, D_MODEL), lambda s: (l, chunk(s), 0)),
    ]
    if final:
        args.append(final_gain)
        in_specs.append(pl.BlockSpec((1, D_MODEL), lambda s: (0, 0)))
        out_specs = [pl.BlockSpec((tm, D_MODEL), lambda s: (jnp.minimum(tile(s), npt - 1), 0)),
                     pl.BlockSpec((tm, D_MODEL), lambda s: (jnp.maximum(tile(s) - npt, 0), 0))]
        out_shape = [jax.ShapeDtypeStruct((P_TOK, D_MODEL), F32),
                     jax.ShapeDtypeStruct((S_TOK, D_MODEL), F32)]
    else:
        out_specs = pl.BlockSpec((tm, D_MODEL), lambda s: (tile(s), 0))
        out_shape = jax.ShapeDtypeStruct((N_TOK, D_MODEL), F32)
    return pl.pallas_call(
        functools.partial(_ffn_kernel, n_x=len(x), tm=tm, final=final, mixer=mixer is not None),
        grid=(FFN_SUB - 1 + N_TOK // tm,),
        in_specs=in_specs,
        out_specs=out_specs,
        out_shape=out_shape,
        scratch_shapes=[pltpu.VMEM((tm, D_MODEL), BF16), pltpu.VMEM((tm, FFN_HIDDEN), BF16),
                        pltpu.VMEM((D_MODEL, 2 * FFN_HIDDEN), BF16),
                        pltpu.VMEM((FFN_HIDDEN, D_MODEL), BF16),
                        pltpu.VMEM((tm, D_MODEL), F32)],
        compiler_params=_params("arbitrary"),
        name="ffn",
    )(*args)


def _four_kernel(x_ref, mod_ref, g_ref, cc_ref, sc_ref, cs_ref, w_ref, b_ref, out_ref, stk_ref, *, L):
    x = x_ref[...]
    h = (_rms(x) * g_ref[...] * (1.0 + mod_ref[1:2, :]) + mod_ref[0:1, :]).astype(BF16)
    for g in range(FOUR_GROUPS):
        ls = slice(g * FOUR_GC, (g + 1) * FOUR_GC)
        hg = h[:, ls]
        stk_ref[0:L, ls] = _dot(hg, cc_ref[...]).astype(BF16)
        stk_ref[L:2 * L, ls] = _dot(hg, sc_ref[...]).astype(BF16)
    f = _dot(cs_ref[...], stk_ref[...]).astype(BF16)
    out_ref[...] = x + mod_ref[2:3, :] * (_dot(f, w_ref[...]) + b_ref[...])


def _dft_tables(L):
    n = FOUR_GC
    ang_c = 2.0 * np.pi * ((np.arange(n)[:, None] * np.arange(n)[None, :]) % n) / n
    ang_l = 2.0 * np.pi * ((np.arange(L)[:, None] * np.arange(L)[None, :]) % L) / L
    cc = np.cos(ang_c) / math.sqrt(n)
    sc = np.sin(ang_c) / math.sqrt(n)
    cs = np.concatenate([np.cos(ang_l), -np.sin(ang_l)], axis=1) / math.sqrt(L)
    return (jnp.asarray(cc, F32).astype(BF16), jnp.asarray(sc, F32).astype(BF16),
            jnp.asarray(cs, F32).astype(BF16))


def _four_call(x, mod, l, i_odd, gain, w, b, *, L, nb, blk0):
    nblk = N_TOK // L
    cc, sc, cs = _dft_tables(L)
    row = _mod_row(L)
    out = pl.pallas_call(
        functools.partial(_four_kernel, L=L),
        grid=(nb,),
        in_specs=[
            pl.BlockSpec((None, L, D_MODEL), lambda i: (blk0 + i, 0, 0)),
            pl.BlockSpec((None, None, 6, D_MODEL), lambda i: (l, row(blk0 + i), 0, 0)),
            pl.BlockSpec((1, D_MODEL), lambda i: (0, 0)),
            pl.BlockSpec((FOUR_GC, FOUR_GC), lambda i: (0, 0)),
            pl.BlockSpec((FOUR_GC, FOUR_GC), lambda i: (0, 0)),
            pl.BlockSpec((L, 2 * L), lambda i: (0, 0)),
            pl.BlockSpec((None, D_MODEL, D_MODEL), lambda i: (i_odd, 0, 0)),
            pl.BlockSpec((1, D_MODEL), lambda i: (0, 0)),
        ],
        out_specs=pl.BlockSpec((None, L, D_MODEL), lambda i: (i, 0, 0)),
        out_shape=jax.ShapeDtypeStruct((nb, L, D_MODEL), F32),
        scratch_shapes=[pltpu.VMEM((2 * L, D_MODEL), BF16)],
        compiler_params=_params("parallel"),
        name=f"fourier_L{L}",
    )(x.reshape(nblk, L, D_MODEL), mod, gain, cc, sc, cs, w, b)
    return out.reshape(nb * L, D_MODEL)


def _rope_lane_tables(L):
    t = jnp.arange(L)
    row = (t // GRID_W).astype(F32)
    col = (t % GRID_W).astype(F32)
    freq = ROPE_THETA ** (-jnp.arange(ROPE_PAIRS, dtype=F32) / ROPE_PAIRS)
    ar = row[:, None] * freq[None]
    ac = col[:, None] * freq[None]
    z = jnp.zeros_like(ar)
    cos32 = lambda a: jnp.concatenate([jnp.cos(a), jnp.cos(a)], axis=-1)
    sa32 = lambda a: jnp.concatenate([-jnp.sin(a), z], axis=-1)
    sb32 = lambda a: jnp.concatenate([z, jnp.sin(a)], axis=-1)
    tile = lambda f: jnp.concatenate([f(ar), f(ac), f(ar), f(ac)], axis=-1)
    return tile(cos32), tile(sa32), tile(sb32)


def _pad_lanes(v, width):
    v = v.reshape(1, -1).astype(F32)
    return jnp.pad(v, ((0, 0), (0, width - v.shape[1])))


def kernel(x_prompt, x_sample, cache_k, cache_v, state_ssd_fwd, state_ssd_bwd, c, c_ctx, w_ada, b_ada, norm_mix, norm_ffn, w_in_ab, conv_w, conv_b, dt_bias, a_log, d_skip, ssd_norm, lambda_qk, subln, w_out_ab, w_four, b_four, w_ffn_in, w_ffn_out, norm_final):
    x = (x_prompt.reshape(P_TOK, D_MODEL), x_sample.reshape(S_TOK, D_MODEL))
    cvec = jnp.concatenate([c_ctx[None, :], c, jnp.zeros((N_MOD - 1 - DEC_BATCH, D_MODEL), F32)], axis=0)
    mod = _ada_call(cvec, w_ada, b_ada).reshape(DEPTH, N_MOD, 6, D_MODEL)
    rope = _rope_lane_tables(DEC_SEQ)

    o2, o3 = SSD_INNER + CONV_CH, SSD_INNER + CONV_CH + 2 * SSD_HEADS
    w_in_bf = (w_in_ab[:, :, :o2].astype(BF16), w_in_ab[:, :, o3:].astype(BF16),
               jnp.pad(w_in_ab[:, :, o2:o3], ((0, 0), (0, 0), (0, DT_W - (o3 - o2)))).astype(BF16))
    w_out_bf = w_out_ab.astype(BF16)
    w_four_bf = w_four.astype(BF16)

    kv_new = None
    states = "new"
    blk_s = P_TOK // DEC_SEQ
    for l in range(DEPTH):
        gain_mix = norm_mix[l].reshape(1, D_MODEL)
        if l % 2 == 0:
            e = l // 2
            lam_init = 0.8 - 0.6 * math.exp(-0.3 * l)
            xbc, z, pb, dtarr, k_new, v_new = _inproj_call(x, mod, l, e, gain_mix, w_in_bf, kv_new)
            kv_new = (k_new, v_new)

            ssd_w = (jnp.pad(conv_w[e], ((0, 8 - CONV_W), (0, 0))), conv_b[e].reshape(1, CONV_CH),
                     _pad_lanes(dt_bias[e], DT_W), _pad_lanes(a_log[e], DT_W),
                     jnp.repeat(d_skip[e], SSD_HD).reshape(1, SSD_INNER),
                     ssd_norm[e].reshape(1, SSD_INNER))
            sub = subln[e].reshape(1, ATT_VD)
            ((y_p, hf, hb),) = _launch(
                [_ssd_parts(xbc, z, dtarr, states, ssd_w, e, L=SEQ, nb=BATCH, blk0=0)],
                (BATCH,), ("parallel",), "ssd_L256")
            states = (hf, hb)
            ((o_p,),) = _launch(
                [_attn_parts(pb, lambda_qk[e], sub, lam_init, L=SEQ, nb=BATCH, blk0=0, tq=SEQ)],
                (BATCH,), ("parallel",), "attn_L256")
            ((y_s,),) = _launch(
                [_ssd_parts(xbc, z, dtarr, None, ssd_w, e, L=DEC_SEQ, nb=DEC_BATCH, blk0=blk_s,
                            h0=(state_ssd_fwd, state_ssd_bwd))],
                (DEC_BATCH,), ("parallel",), "ssd_L1024")
            ((o_s,),) = _launch(
                [_attn_parts(pb, lambda_qk[e], sub, lam_init, L=DEC_SEQ, nb=DEC_BATCH, blk0=blk_s,
                             tq=256, ctx=(cache_k, cache_v, e), rope=rope)],
                (DEC_BATCH, DEC_SEQ // 256), ("parallel", "arbitrary"), "attn_L1024")
            halves = lambda a, b: (a.reshape(P_TOK, -1), b.reshape(S_TOK, -1))
            mixer = (halves(y_p, y_s), halves(o_p, o_s), w_out_bf, e)
        else:
            mixer = None
            i_odd = l // 2
            b4 = b_four[i_odd].reshape(1, D_MODEL)
            x = (_four_call(x[0], mod, l, i_odd, gain_mix, w_four_bf, b4, L=SEQ, nb=BATCH, blk0=0),
                 _four_call(x[0], mod, l, i_odd, gain_mix, w_four_bf, b4, L=DEC_SEQ, nb=DEC_BATCH,
                            blk0=blk_s))
        last = l == DEPTH - 1
        x = _ffn_call(x, mod, l, norm_ffn[l].reshape(1, D_MODEL), w_ffn_in, w_ffn_out,
                      final_gain=norm_final.reshape(1, D_MODEL) if last else None, mixer=mixer)
        x = tuple(x) if last else (x,)

    y_p, y_s = x
    return (y_p.reshape(BATCH, SEQ, D_MODEL), y_s.reshape(DEC_BATCH, DEC_SEQ, D_MODEL),
            kv_new[0].reshape(BATCH, N_EVEN, SEQ, ATT_HEADS, 2, ATT_HD),
            kv_new[1].reshape(BATCH, N_EVEN, SEQ, ATT_HEADS, ATT_VD),
            states[0].reshape(BATCH, N_EVEN, SSD_HEADS, SSD_HD, D_STATE),
            states[1].reshape(BATCH, N_EVEN, SSD_HEADS, SSD_HD, D_STATE))
```

```python
import functools
import math

import numpy as np
import jax
import jax.numpy as jnp
from jax import lax
from jax.experimental import pallas as pl
from jax.experimental.pallas import tpu as pltpu

D_MODEL = 1024
BATCH = 16
SEQ = 256
DEPTH = 4
N_EVEN = 2
DEC_BATCH = 4
DEC_SEQ = 1024
PAST_LEN = 512
GRID_W = 64
SSD_HEADS = 16
SSD_HD = 64
SSD_INNER = 1024
SSD_GROUPS = 4
D_STATE = 128
CONV_W = 5
CONV_CH = 2048
CHUNK = 128
ATT_HEADS = 8
ATT_HD = 64
ATT_VD = 128
ATT_INNER = 1024
ROPE_THETA = 10000.0
ROPE_PAIRS = 16
FOUR_GROUPS = 4
FOUR_GC = D_MODEL // FOUR_GROUPS
FFN_HIDDEN = 2816
EPS = 1e-6

P_TOK = BATCH * SEQ
S_TOK = DEC_BATCH * DEC_SEQ
N_TOK = P_TOK + S_TOK
N_MOD = 8
PA_W = CONV_CH + SSD_INNER
PB_W = 3 * ATT_INNER
DT_W = 128
NH2 = SSD_HEADS // 2
LANE = 128
SUBLANE = 8
N_SLAB = CONV_CH // LANE
QBLK = 32
MXU_N = 256
ATT_TQ = 512
VMEM_LIMIT = 56 * 1024 * 1024
VMEM_LIMIT_INPROJ = 61 * 1024 * 1024

F32 = jnp.float32
BF16 = jnp.bfloat16


def _params(*sem, vmem=VMEM_LIMIT):
    return pltpu.CompilerParams(dimension_semantics=sem, vmem_limit_bytes=vmem)


def _dot(a, b):
    return jnp.dot(a, b, preferred_element_type=F32)


def _dot_nt(a, b):
    return lax.dot_general(a, b, (((1,), (1,)), ((), ())), preferred_element_type=F32)


def _silu(x):
    h = 0.5 * x
    return h + h * jnp.tanh(h)


def _rms(x):
    return x * lax.rsqrt(jnp.mean(x * x, axis=-1, keepdims=True) + EPS)


def _mod_row(tm):
    return lambda i: jnp.maximum((i * tm) // DEC_SEQ - (P_TOK // DEC_SEQ - 1), 0)


def _drop_args(kern, start, n):
    def wrapped(*refs):
        return kern(*(refs[:start] + refs[start + n:]))
    return wrapped


def _tok_specs(parts, tm):
    width = parts[0].shape[-1]
    if len(parts) == 1:
        return [pl.BlockSpec((tm, width), lambda i: (i, 0))]
    npt = P_TOK // tm
    return [pl.BlockSpec((tm, width), lambda i: (jnp.minimum(i, npt - 1), 0)),
            pl.BlockSpec((tm, width), lambda i: (jnp.maximum(i - npt, 0), 0))]


def _tok_load(refs, tm):
    if len(refs) == 1:
        return refs[0][...]
    return jnp.where(pl.program_id(0) < P_TOK // tm, refs[0][...], refs[1][...])


def _launch(parts, grid, sem, name):
    n_in = [len(p["args"]) for p in parts]
    n_out = [len(p["out_shape"]) for p in parts]
    n_scr = [len(p["scratch"]) for p in parts]

    def kern(*refs):
        ins = refs[:sum(n_in)]
        outs = refs[sum(n_in):sum(n_in) + sum(n_out)]
        scr = refs[sum(n_in) + sum(n_out):]
        i0 = o0 = s0 = 0
        for p, ni, no, ns in zip(parts, n_in, n_out, n_scr):
            p["kernel"](*ins[i0:i0 + ni], *outs[o0:o0 + no], *scr[s0:s0 + ns])
            i0, o0, s0 = i0 + ni, o0 + no, s0 + ns

    alias = {}
    i0 = o0 = 0
    for p, ni, no in zip(parts, n_in, n_out):
        alias.update({i0 + k: o0 + v for k, v in p["alias"].items()})
        i0, o0 = i0 + ni, o0 + no
    cat = lambda key: [v for p in parts for v in p[key]]
    outs = pl.pallas_call(
        kern, grid=grid, in_specs=cat("in_specs"), out_specs=cat("out_specs"),
        out_shape=cat("out_shape"), scratch_shapes=cat("scratch"),
        input_output_aliases=alias, compiler_params=_params(*sem), name=name,
    )(*cat("args"))
    res, o0 = [], 0
    for no in n_out:
        res.append(list(outs[o0:o0 + no]))
        o0 += no
    return res


def _ada_kernel(c_ref, w_ref, b_ref, o_ref):
    s = _silu(c_ref[...]).astype(BF16)
    o_ref[...] = _dot(s, w_ref[...].astype(BF16)) + b_ref[...]


def _ada_call(cvec, w_ada, b_ada):
    tn = 2048
    return pl.pallas_call(
        _ada_kernel,
        grid=(DEPTH, 6 * D_MODEL // tn),
        in_specs=[
            pl.BlockSpec((N_MOD, D_MODEL), lambda l, j: (0, 0)),
            pl.BlockSpec((None, D_MODEL, tn), lambda l, j: (l, 0, j)),
            pl.BlockSpec((None, 1, tn), lambda l, j: (l, 0, j)),
        ],
        out_specs=pl.BlockSpec((None, N_MOD, tn), lambda l, j: (l, 0, j)),
        out_shape=jax.ShapeDtypeStruct((DEPTH, N_MOD, 6 * D_MODEL), F32),
        compiler_params=_params("parallel", "parallel"),
        name="adaln",
    )(cvec, w_ada, b_ada.reshape(DEPTH, 1, 6 * D_MODEL))


def _inproj_kernel(*refs, n_x, tm, all_slots):
    x_refs = refs[:n_x]
    (mod_ref, g_ref, wzx_ref, wqkv_ref, wdt_ref,
     xbc_ref, z_ref, pb_ref, dt_ref, kn_ref, vn_ref) = refs[n_x:]
    x = _tok_load(x_refs, tm)
    h = _rms(x) * g_ref[...] * (1.0 + mod_ref[1:2, :]) + mod_ref[0:1, :]
    hb = h.astype(BF16)
    tn = ATT_INNER
    for j in range(CONV_CH // tn):
        res = _dot(hb, wzx_ref[:, SSD_INNER + j * tn:SSD_INNER + (j + 1) * tn])
        for s in range(tn // LANE):
            xbc_ref[j * (tn // LANE) + s] = res[:, s * LANE:(s + 1) * LANE]
    z_ref[...] = _dot(hb, wzx_ref[:, 0:SSD_INNER])
    q = _dot(hb, wqkv_ref[:, 0:tn])
    pb_ref[:, 0:tn] = q.astype(BF16)
    k = _dot(hb, wqkv_ref[:, tn:2 * tn])
    pb_ref[:, tn:2 * tn] = k.astype(BF16)
    v = _dot(hb, wqkv_ref[:, 2 * tn:3 * tn])
    pb_ref[:, 2 * tn:3 * tn] = v.astype(BF16)
    dt_ref[...] = _dot(hb, wdt_ref[...])

    @pl.when(pl.program_id(0) < P_TOK // tm)
    def _():
        for new_ref, val in ((kn_ref, k), (vn_ref, v)):
            val = val.reshape(tm // SEQ, SEQ, ATT_INNER)
            if all_slots:
                new_ref[:, 0] = val
                for later in range(1, N_EVEN):
                    new_ref[:, later] = jnp.zeros_like(val)
            else:
                new_ref[...] = val


def _inproj_call(x, mod, l, e, gain, w_in, kv_prev):
    tm = 512
    row = _mod_row(tm)
    npt = P_TOK // tm
    n_x = len(x)
    once = pl.Buffered(1)
    args = [*x, mod, gain, *w_in]
    in_specs = _tok_specs(x, tm) + [
        pl.BlockSpec((None, None, 6, D_MODEL), lambda i: (l, row(i), 0, 0)),
        pl.BlockSpec((1, D_MODEL), lambda i: (0, 0)),
        pl.BlockSpec((None, D_MODEL, PA_W), lambda i: (e, 0, 0), pipeline_mode=once),
        pl.BlockSpec((None, D_MODEL, PB_W), lambda i: (e, 0, 0), pipeline_mode=once),
        pl.BlockSpec((None, D_MODEL, DT_W), lambda i: (e, 0, 0), pipeline_mode=once),
    ]
    n_in = len(args)
    kern = functools.partial(_inproj_kernel, n_x=n_x, tm=tm, all_slots=kv_prev is None)
    if kv_prev is None:
        io_alias = {}
        kv_spec = pl.BlockSpec((tm // SEQ, N_EVEN, SEQ, ATT_INNER),
                               lambda i: (jnp.minimum(i, npt - 1), 0, 0, 0))
    else:
        io_alias = {n_in: 4, n_in + 1: 5}
        args += list(kv_prev)
        in_specs += [pl.BlockSpec(memory_space=pl.ANY)] * 2
        kern = _drop_args(kern, n_in, 2)
        kv_spec = pl.BlockSpec((tm // SEQ, None, SEQ, ATT_INNER),
                               lambda i: (jnp.minimum(i, npt - 1), e, 0, 0))
    kv_shape = jax.ShapeDtypeStruct((BATCH, N_EVEN, SEQ, ATT_INNER), F32)
    return pl.pallas_call(
        kern,
        grid=(N_TOK // tm,),
        in_specs=in_specs,
        out_specs=[
            pl.BlockSpec((N_SLAB, tm, LANE), lambda i: (0, i, 0)),
            pl.BlockSpec((tm, SSD_INNER), lambda i: (i, 0)),
            pl.BlockSpec((tm, PB_W), lambda i: (i, 0)),
            pl.BlockSpec((tm, DT_W), lambda i: (i, 0)),
            kv_spec, kv_spec,
        ],
        out_shape=[
            jax.ShapeDtypeStruct((N_SLAB, N_TOK, LANE), F32),
            jax.ShapeDtypeStruct((N_TOK, SSD_INNER), F32),
            jax.ShapeDtypeStruct((N_TOK, PB_W), BF16),
            jax.ShapeDtypeStruct((N_TOK, DT_W), F32),
            kv_shape, kv_shape,
        ],
        input_output_aliases=io_alias,
        compiler_params=_params("arbitrary", vmem=VMEM_LIMIT_INPROJ),
        name="inproj",
    )(*args)


def _ssd_kernel(*refs, L, has_h0, emit_state, all_slots):
    it = iter(refs)
    xbc_ref, z_ref, dt_ref = next(it), next(it), next(it)
    convw_ref, convb_ref, dtb_ref, alog_ref, dskip_ref, norm_ref = (next(it) for _ in range(6))
    sel_refs = (next(it), next(it))
    if has_h0:
        h0_refs = (next(it), next(it))
    y_ref = next(it)
    if emit_state:
        hout_refs = (next(it), next(it))
    (xact, xdt_f, xdt_b, bt, cact, tmp, yacc, dtv, state_f, state_b,
     acs_s, acst_s, cb_s) = (next(it) for _ in range(13))

    nc = L // CHUNK
    nq = L // SUBLANE
    half_w = CONV_W // 2

    rq = lax.broadcasted_iota(jnp.int32, (QBLK, LANE), 0)

    def phase(s, pe, q0):
        r0 = SUBLANE * q0 + pe
        if r0 < 0:
            a = xbc_ref[s, pl.ds(r0 + SUBLANE, QBLK, stride=SUBLANE), :]
            return jnp.where(rq == 0, 0.0, pltpu.roll(a, 1, 0))
        if r0 + SUBLANE * (QBLK - 1) >= L:
            a = xbc_ref[s, pl.ds(r0 - SUBLANE, QBLK, stride=SUBLANE), :]
            return jnp.where(rq == QBLK - 1, 0.0, pltpu.roll(a, QBLK - 1, 0))
        return xbc_ref[s, pl.ds(r0, QBLK, stride=SUBLANE), :]

    def conv_slab(s, dst, d_idx):
        ls = slice(s * LANE, (s + 1) * LANE)
        for q0 in range(0, nq, QBLK):
            ext = {pe: phase(s, pe, q0) for pe in range(-half_w, SUBLANE + half_w)}
            for p in range(SUBLANE):
                acc = jnp.broadcast_to(convb_ref[:, ls], (QBLK, LANE))
                for k in range(CONV_W):
                    acc = acc + convw_ref[k:k + 1, ls] * ext[p + k - half_w]
                dst[d_idx, pl.ds(SUBLANE * q0 + p, QBLK, stride=SUBLANE), :] = _silu(acc)

    for m in range(NH2):
        conv_slab(m, xact, m)
    for g in range(SSD_GROUPS):
        conv_slab(NH2 + g, tmp, g)
    for c in range(nc):
        for g in range(SSD_GROUPS):
            bt[c, g * D_STATE:(g + 1) * D_STATE, :] = (
                tmp[g, c * CHUNK:(c + 1) * CHUNK, :].T.astype(BF16))
    for g in range(SSD_GROUPS):
        conv_slab(NH2 + SSD_GROUPS + g, tmp, g)
    for c in range(nc):
        rows = slice(c * CHUNK, (c + 1) * CHUNK)
        for g in range(SSD_GROUPS):
            cact[g, rows, :] = tmp[g, rows, :].astype(BF16)

    dr = dt_ref[...] + dtb_ref[...]
    dtv[...] = jnp.maximum(dr, 0.0) + jnp.log1p(jnp.exp(-jnp.abs(dr)))
    a_row = -jnp.exp(alog_ref[...]) * math.log2(math.e)

    for c in range(nc):
        rows = slice(c * CHUNK, (c + 1) * CHUNK)
        d = dtv[rows, :]
        d_hi = d.astype(BF16)
        d_lo = (d - d_hi.astype(F32)).astype(BF16)
        d2 = jnp.concatenate([d_hi, d_lo], axis=1)
        spread_f = _dot(d2, sel_refs[0][...])
        spread_b = _dot(d2, sel_refs[1][...])
        for m in range(NH2):
            lanes = slice(m * LANE, (m + 1) * LANE)
            xm = xact[m, rows, :]
            xdt_f[m, rows, :] = (xm * spread_f[:, lanes]).astype(BF16)
            xdt_b[m, rows, :] = (xm * spread_b[:, lanes]).astype(BF16)

    ri = lax.broadcasted_iota(jnp.int32, (CHUNK, CHUNK), 0)
    ci = lax.broadcasted_iota(jnp.int32, (CHUNK, CHUNK), 1)
    low_half = ci < SSD_HD
    low_row = low_half[0:1, :]

    for c in range(nc):
        rows = slice(c * CHUNK, (c + 1) * CHUNK)
        dta = dtv[rows, :] * a_row
        run_f = jnp.dot((ri >= ci).astype(F32), dta, preferred_element_type=F32,
                        precision=lax.Precision.HIGHEST)
        run_b = jnp.dot((ri <= ci).astype(F32), dta, preferred_element_type=F32,
                        precision=lax.Precision.HIGHEST)
        acs = jnp.where(ci < SSD_HEADS, run_f, run_b)
        acs_s[c] = acs
        acst_s[c] = acs.T
        for g in range(SSD_GROUPS):
            cb_s[c, g] = _dot(cact[g, rows, :], bt[c, g * D_STATE:(g + 1) * D_STATE, :])

    for d_idx, st_ref in enumerate((state_f, state_b)):
        for m in range(NH2):
            if has_h0:
                st_ref[m] = h0_refs[d_idx][m].T
            else:
                st_ref[m] = jnp.zeros((D_STATE, LANE), F32)

    def chunk_rows(c):
        return pl.ds(pl.multiple_of(c * CHUNK, CHUNK), CHUNK)

    def chunk_step(c, fwd, add):
        mask = (ri >= ci) if fwd else (ri <= ci)
        col0 = 0 if fwd else SSD_HEADS
        last = CHUNK - 1 if fwd else 0
        xdt = xdt_f if fwd else xdt_b
        state = state_f if fwd else state_b
        rows = chunk_rows(c)
        acs = acs_s[c]
        acs_t = acst_s[c]
        acs_last = acs[last:last + 1, :]
        for g in range(SSD_GROUPS):
            btg = bt[c, g * D_STATE:(g + 1) * D_STATE, :]
            cg = cact[g, rows, :]
            cbm = jnp.where(mask, cb_s[c, g], 0.0)
            for pr in range(2):
                m = g * 2 + pr
                ha, hb = col0 + 2 * m, col0 + 2 * m + 1
                lanes = slice(m * LANE, (m + 1) * LANE)
                xd2 = xdt[m, rows, :]
                acol_a = jnp.broadcast_to(acs[:, ha:ha + 1], (CHUNK, CHUNK))
                acol_b = jnp.broadcast_to(acs[:, hb:hb + 1], (CHUNK, CHUNK))
                w_a = cbm * jnp.exp2(jnp.minimum(acol_a - acs_t[ha:ha + 1, :], 0.0))
                w_b = cbm * jnp.exp2(jnp.minimum(acol_b - acs_t[hb:hb + 1, :], 0.0))
                w2 = jnp.concatenate([w_a.astype(BF16), w_b.astype(BF16)], axis=1)
                zero = jnp.zeros_like(xd2)
                x_blk = jnp.concatenate([jnp.where(low_half, xd2, zero),
                                         jnp.where(low_half, zero, xd2)], axis=0)
                y_in = _dot(w2, x_blk)
                acol2 = jnp.where(low_half, acol_a, acol_b)
                last2 = jnp.where(low_row, acs_last[:, ha:ha + 1], acs_last[:, hb:hb + 1])
                h_prev = state[m]
                y_out = _dot(cg, h_prev.astype(BF16)) * jnp.exp2(acol2)
                to_end = jnp.exp2(last2 - acol2)
                st = _dot(btg, (to_end * xd2.astype(F32)).astype(BF16))
                state[m] = jnp.exp2(last2) * h_prev + st
                y2 = y_in + y_out
                if fwd:
                    y2 = y2 + dskip_ref[:, lanes] * xact[m, rows, :]
                if add:
                    y2 = y2 + yacc[rows, lanes]
                yacc[rows, lanes] = y2

    def finish(c):
        rows = chunk_rows(c)
        yt = yacc[rows, :] * _silu(z_ref[rows, :])
        y_ref[rows, :] = (_rms(yt) * norm_ref[...]).astype(y_ref.dtype)

    def first_half(s, carry):
        chunk_step(s, True, False)
        chunk_step(nc - 1 - s, False, False)
        return carry

    def second_half(s, carry):
        chunk_step(s, True, True)
        chunk_step(nc - 1 - s, False, True)
        finish(s)
        finish(nc - 1 - s)
        return carry

    lax.fori_loop(0, nc // 2, first_half, 0)
    lax.fori_loop(nc // 2, nc, second_half, 0)

    if emit_state:
        for d_idx, st_ref in enumerate((state_f, state_b)):
            for m in range(NH2):
                if all_slots:
                    hout_refs[d_idx][0, m] = st_ref[m].T
                    for later in range(1, N_EVEN):
                        hout_refs[d_idx][later, m] = jnp.zeros((LANE, D_STATE), F32)
                else:
                    hout_refs[d_idx][m] = st_ref[m].T


def _head_spread_tables():
    ch_head = np.arange(SSD_INNER) // SSD_HD
    out = []
    for d in range(2):
        k = np.arange(2 * DT_W) % DT_W
        out.append(jnp.asarray(k[:, None] == (d * SSD_HEADS + ch_head)[None, :], BF16))
    return out


def _ssd_parts(xbc, z, dtarr, states_prev, weights, e, *, L, nb, blk0, h0=None):
    emit_state = states_prev is not None
    all_slots = isinstance(states_prev, str)
    nblk = N_TOK // L
    nc = L // CHUNK
    vec = lambda w: pl.BlockSpec((1, w), lambda b: (0, 0))
    in_specs = [
        pl.BlockSpec((N_SLAB, L, LANE), lambda b: (0, blk0 + b, 0)),
        pl.BlockSpec((None, L, SSD_INNER), lambda b: (blk0 + b, 0, 0)),
        pl.BlockSpec((None, L, DT_W), lambda b: (blk0 + b, 0, 0)),
        pl.BlockSpec((8, CONV_CH), lambda b: (0, 0)),
        vec(CONV_CH), vec(DT_W), vec(DT_W), vec(SSD_INNER), vec(SSD_INNER),
        pl.BlockSpec((2 * DT_W, SSD_INNER), lambda b: (0, 0)),
        pl.BlockSpec((2 * DT_W, SSD_INNER), lambda b: (0, 0)),
    ]
    args = [xbc, z.reshape(nblk, L, SSD_INNER), dtarr.reshape(nblk, L, DT_W), *weights,
            *_head_spread_tables()]
    state_spec = pl.BlockSpec((None, None, NH2, LANE, D_STATE), lambda b: (b, e, 0, 0, 0))
    if h0 is not None:
        for h in h0:
            args.append(h.reshape(DEC_BATCH, N_EVEN, NH2, LANE, D_STATE))
            in_specs.append(state_spec)
    kern = functools.partial(_ssd_kernel, L=L, has_h0=h0 is not None, emit_state=emit_state,
                             all_slots=all_slots)
    out_specs = [pl.BlockSpec((None, L, SSD_INNER), lambda b: (b, 0, 0))]
    out_shape = [jax.ShapeDtypeStruct((nb, L, SSD_INNER), BF16)]
    io_alias = {}
    if emit_state:
        if all_slots:
            out_spec = pl.BlockSpec((None, N_EVEN, NH2, LANE, D_STATE), lambda b: (b, 0, 0, 0, 0))
        else:
            out_spec = state_spec
            io_alias = {len(args): 1, len(args) + 1: 2}
            kern = _drop_args(kern, len(args), 2)
            args += list(states_prev)
            in_specs += [pl.BlockSpec(memory_space=pl.ANY)] * 2
        out_specs += [out_spec, out_spec]
        out_shape += [jax.ShapeDtypeStruct((nb, N_EVEN, NH2, LANE, D_STATE), F32)] * 2
    scratch = [
        pltpu.VMEM((NH2, L, LANE), F32),
        pltpu.VMEM((NH2, L, LANE), BF16),
        pltpu.VMEM((NH2, L, LANE), BF16),
        pltpu.VMEM((nc, SSD_GROUPS * D_STATE, CHUNK), BF16),
        pltpu.VMEM((SSD_GROUPS, L, D_STATE), BF16),
        pltpu.VMEM((SSD_GROUPS, L, D_STATE), F32),
        pltpu.VMEM((L, SSD_INNER), F32),
        pltpu.VMEM((L, DT_W), F32),
        pltpu.VMEM((NH2, D_STATE, LANE), F32),
        pltpu.VMEM((NH2, D_STATE, LANE), F32),
        pltpu.VMEM((nc, CHUNK, LANE), F32),
        pltpu.VMEM((nc, LANE, CHUNK), F32),
        pltpu.VMEM((nc, SSD_GROUPS, CHUNK, CHUNK), F32),
    ]
    return dict(kernel=kern, args=args, in_specs=in_specs, out_specs=out_specs,
                out_shape=out_shape, scratch=scratch, alias=io_alias)


def _rope(x, cos, sin_a, sin_b):
    return (x * cos + pltpu.roll(x, LANE - ROPE_PAIRS, 1) * sin_a
            + pltpu.roll(x, ROPE_PAIRS, 1) * sin_b)


def _attn_kernel(*refs, L, n_ctx, lam_init, stack_maps, q_axis):
    it = iter(refs)
    q_ref, k_ref, v_ref = next(it), next(it), next(it)
    if n_ctx:
        ck_ref, cv_ref = next(it), next(it)
        cq, saq, sbq, ck, sak, sbk = (next(it) for _ in range(6))
    lam_ref, sub_ref = next(it), next(it)
    o_ref = next(it)
    kall, vext = next(it), next(it)
    lk = n_ctx + L
    tq = q_ref.shape[0]

    def prepare_keys_values():
        for h in range(ATT_HEADS):
            hs = slice(h * ATT_VD, (h + 1) * ATT_VD)
            if n_ctx:
                kall[h, 0:n_ctx, :] = ck_ref[:, hs].astype(BF16)
                vext[h, 0:n_ctx, 0:ATT_VD] = cv_ref[:, hs].astype(BF16)
                kh = _rope(k_ref[:, hs].astype(F32), ck[...], sak[...], sbk[...])
                kall[h, n_ctx:lk, :] = kh.astype(BF16)
            else:
                kall[h] = k_ref[:, hs]
            vext[h, n_ctx:lk, 0:ATT_VD] = v_ref[:, hs]
            vext[h, :, ATT_VD:MXU_N] = jnp.ones((lk, MXU_N - ATT_VD), BF16)

    if q_axis is None:
        prepare_keys_values()
    else:
        pl.when(pl.program_id(q_axis) == 0)(prepare_keys_values)

    lane = lax.broadcasted_iota(jnp.int32, (1, ATT_VD), 1)
    scale = ATT_HD ** -0.5 * math.log2(math.e)
    m1 = jnp.where(lane < ATT_HD, scale, 0.0)
    m2 = jnp.where(lane < ATT_HD, 0.0, scale)
    lq = lam_ref[...]
    lam = (jnp.exp(jnp.sum(lq[0:1, :] * lq[1:2, :], axis=-1, keepdims=True))
           - jnp.exp(jnp.sum(lq[2:3, :] * lq[3:4, :], axis=-1, keepdims=True)) + lam_init)

    for h in range(ATT_HEADS):
        hs = slice(h * ATT_VD, (h + 1) * ATT_VD)
        q = q_ref[:, hs].astype(F32)
        if n_ctx:
            q = _rope(q, cq[...], saq[...], sbq[...])
        def attend(qm):
            s = _dot_nt(qm, kall[h])
            p = jnp.exp2(s - jnp.max(s, axis=-1, keepdims=True)).astype(BF16)
            pv = _dot(p, vext[h])
            return pv[:, 0:ATT_VD] / pv[:, ATT_VD:MXU_N]

        q1, q2 = (q * m1).astype(BF16), (q * m2).astype(BF16)
        if stack_maps:
            on = attend(jnp.concatenate([q1, q2], axis=0))
            o = on[0:tq] - lam * on[tq:2 * tq]
        else:
            o = attend(q1) - lam * attend(q2)
        o_ref[:, hs] = (_rms(o) * sub_ref[...] * (1.0 - lam_init)).astype(o_ref.dtype)


def _attn_parts(pb, lam_qk, subln, lam_init, *, L, nb, blk0, tq, ctx=None, rope=None):
    nblk = N_TOK // L
    one_d = tq == L
    spec = lambda shape, f: pl.BlockSpec(shape, (lambda b: f(b, 0)) if one_d else f)
    pb3 = pb.reshape(nblk, L, PB_W)
    in_specs = [
        spec((None, tq, ATT_INNER), lambda b, i: (blk0 + b, i, 0)),
        spec((None, L, ATT_INNER), lambda b, i: (blk0 + b, 0, 1)),
        spec((None, L, ATT_INNER), lambda b, i: (blk0 + b, 0, 2)),
    ]
    args = [pb3, pb3, pb3]
    n_ctx = 0
    if ctx is not None:
        cache_k, cache_v, e = ctx
        n_ctx = cache_k.shape[2]
        for cch in (cache_k, cache_v):
            args.append(cch.reshape(DEC_BATCH, N_EVEN, n_ctx, ATT_INNER))
            in_specs.append(spec((None, None, n_ctx, ATT_INNER), lambda b, i: (b, e, 0, 0)))
        for t in rope:
            args.append(t)
            in_specs.append(spec((tq, ATT_VD), lambda b, i: (i, 0)))
        for t in rope:
            args.append(t)
            in_specs.append(spec((L, ATT_VD), lambda b, i: (0, 0)))
    args += [lam_qk, subln]
    in_specs += [spec((4, ATT_HD), lambda b, i: (0, 0)), spec((1, ATT_VD), lambda b, i: (0, 0))]
    kern = functools.partial(_attn_kernel, L=L, n_ctx=n_ctx, lam_init=lam_init,
                             stack_maps=n_ctx + L <= 2 * MXU_N,
                             q_axis=None if one_d else 1)
    return dict(kernel=kern, args=args, in_specs=in_specs,
                out_specs=[spec((None, tq, ATT_INNER), lambda b, i: (b, i, 0))],
                out_shape=[jax.ShapeDtypeStruct((nb, L, ATT_INNER), BF16)],
                scratch=[pltpu.VMEM((ATT_HEADS, n_ctx + L, ATT_VD), BF16),
                         pltpu.VMEM((ATT_HEADS, n_ctx + L, MXU_N), BF16)],
                alias={})


FFN_SUB = FFN_HIDDEN // MXU_N


def _ffn_kernel(*refs, n_x, tm, final, mixer):
    x_refs, k = refs[:n_x], n_x
    if mixer:
        y_refs, o_refs, wmix_ref, k = refs[k:k + 2], refs[k + 2:k + 4], refs[k + 4], k + 5
    mod_ref, g_ref, wg_ref, wu_ref, wo_ref = refs[k:k + 5]
    rest = list(refs[k + 5:])
    if final:
        gf_ref, op_ref, os_ref = rest[:3]
        rest = rest[3:]
    else:
        out_ref = rest.pop(0)
    h_ref, a_ref, win_s, wout_s, xin_ref = rest
    s = pl.program_id(0)
    tile = jnp.maximum(s - (FFN_SUB - 1), 0)
    npt = P_TOK // tm

    def load(parts):
        if len(parts) == 1:
            return parts[0][...]
        return jnp.where(tile < npt, parts[0][...], parts[1][...])

    def modulated():
        x = load(x_refs)
        if mixer:
            acc = (_dot(load(y_refs), wmix_ref[0:SSD_INNER, :])
                   + _dot(load(o_refs), wmix_ref[SSD_INNER:, :]))
            x = x + mod_ref[2:3, :] * acc
        xin_ref[...] = x
        h = _rms(x) * g_ref[...] * (1.0 + mod_ref[4:5, :]) + mod_ref[3:4, :]
        return h.astype(BF16)

    def hidden(j, hb, wg, wu):
        a_ref[:, j * MXU_N:(j + 1) * MXU_N] = (_silu(_dot(hb, wg)) * _dot(hb, wu)).astype(BF16)

    for j in range(FFN_SUB):
        @pl.when(s == j)
        def _(j=j):
            if j == 0:
                h_ref[...] = modulated()
            cs = slice(j * MXU_N, (j + 1) * MXU_N)
            us = slice(FFN_HIDDEN + j * MXU_N, FFN_HIDDEN + (j + 1) * MXU_N)
            wg, wu = wg_ref[...].astype(BF16), wu_ref[...].astype(BF16)
            win_s[:, cs] = wg
            win_s[:, us] = wu
            wout_s[cs, :] = wo_ref[...].astype(BF16)
            hidden(j, h_ref[...], wg, wu)

    @pl.when(s >= FFN_SUB)
    def _():
        h_ref[...] = modulated()
        for j in range(FFN_SUB):
            cs = slice(j * MXU_N, (j + 1) * MXU_N)
            us = slice(FFN_HIDDEN + j * MXU_N, FFN_HIDDEN + (j + 1) * MXU_N)
            hidden(j, h_ref[...], win_s[:, cs], win_s[:, us])

    @pl.when(s >= FFN_SUB - 1)
    def _():
        x_new = xin_ref[...] + mod_ref[5:6, :] * _dot(a_ref[...], wout_s[...])
        if final:
            y = _rms(x_new) * gf_ref[...]

            @pl.when(tile < npt)
            def _():
                op_ref[...] = y

            @pl.when(tile >= npt)
            def _():
                os_ref[...] = y
        else:
            out_ref[...] = x_new


def _ffn_call(x, mod, l, gain, w_in, w_out, final_gain=None, mixer=None):
    tm = 512
    row = _mod_row(tm)
    npt = P_TOK // tm
    tile = lambda s: jnp.maximum(s - (FFN_SUB - 1), 0)
    chunk = lambda s: jnp.minimum(s, FFN_SUB - 1)
    final = final_gain is not None

    def tok_specs(parts):
        if len(parts) == 1:
            return [pl.BlockSpec((tm, D_MODEL), lambda s: (tile(s), 0))]
        return [pl.BlockSpec((tm, D_MODEL), lambda s: (jnp.minimum(tile(s), npt - 1), 0)),
                pl.BlockSpec((tm, D_MODEL), lambda s: (jnp.maximum(tile(s) - npt, 0), 0))]

    args, in_specs = [*x], tok_specs(x)
    if mixer is not None:
        y, o, w_mix, e = mixer
        args += [*y, *o, w_mix]
        in_specs += tok_specs(y) + tok_specs(o) + [
            pl.BlockSpec((None, SSD_INNER + ATT_INNER, D_MODEL), lambda s: (e, 0, 0),
                         pipeline_mode=pl.Buffered(1))]
    args += [mod, gain, w_in, w_in, w_out]
    in_specs += [
        pl.BlockSpec((None, None, 6, D_MODEL), lambda s: (l, row(tile(s)), 0, 0)),
        pl.BlockSpec((1, D_MODEL), lambda s: (0, 0)),
        pl.BlockSpec((None, D_MODEL, MXU_N), lambda s: (l, 0, chunk(s))),
        pl.BlockSpec((None, D_MODEL, MXU_N), lambda s: (l, 0, FFN_SUB + chunk(s))),
        pl.BlockSpec((None, MXU_N, D_MODEL), lambda s: (l, chunk(s), 0)),
    ]
    if final:
        args.append(final_gain)
        in_specs.append(pl.BlockSpec((1, D_MODEL), lambda s: (0, 0)))
        out_specs = [pl.BlockSpec((tm, D_MODEL), lambda s: (jnp.minimum(tile(s), npt - 1), 0)),
                     pl.BlockSpec((tm, D_MODEL), lambda s: (jnp.maximum(tile(s) - npt, 0), 0))]
        out_shape = [jax.ShapeDtypeStruct((P_TOK, D_MODEL), F32),
                     jax.ShapeDtypeStruct((S_TOK, D_MODEL), F32)]
    else:
        out_specs = pl.BlockSpec((tm, D_MODEL), lambda s: (tile(s), 0))
        out_shape = jax.ShapeDtypeStruct((N_TOK, D_MODEL), F32)
    return pl.pallas_call(
        functools.partial(_ffn_kernel, n_x=len(x), tm=tm, final=final, mixer=mixer is not None),
        grid=(FFN_SUB - 1 + N_TOK // tm,),
        in_specs=in_specs,
        out_specs=out_specs,
        out_shape=out_shape,
        scratch_shapes=[pltpu.VMEM((tm, D_MODEL), BF16), pltpu.VMEM((tm, FFN_HIDDEN), BF16),
                        pltpu.VMEM((D_MODEL, 2 * FFN_HIDDEN), BF16),
                        pltpu.VMEM((FFN_HIDDEN, D_MODEL), BF16),
                        pltpu.VMEM((tm, D_MODEL), F32)],
        compiler_params=_params("arbitrary"),
        name="ffn",
    )(*args)


def _four_kernel(x_ref, mod_ref, g_ref, cc_ref, sc_ref, cs_ref, w_ref, b_ref, out_ref, stk_ref, *, L):
    x = x_ref[...]
    h = (_rms(x) * g_ref[...] * (1.0 + mod_ref[1:2, :]) + mod_ref[0:1, :]).astype(BF16)
    for g in range(FOUR_GROUPS):
        ls = slice(g * FOUR_GC, (g + 1) * FOUR_GC)
        hg = h[:, ls]
        stk_ref[0:L, ls] = _dot(hg, cc_ref[...]).astype(BF16)
        stk_ref[L:2 * L, ls] = _dot(hg, sc_ref[...]).astype(BF16)
    f = _dot(cs_ref[...], stk_ref[...]).astype(BF16)
    out_ref[...] = x + mod_ref[2:3, :] * (_dot(f, w_ref[...]) + b_ref[...])


def _dft_tables(L):
    n = FOUR_GC
    ang_c = 2.0 * np.pi * ((np.arange(n)[:, None] * np.arange(n)[None, :]) % n) / n
    ang_l = 2.0 * np.pi * ((np.arange(L)[:, None] * np.arange(L)[None, :]) % L) / L
    cc = np.cos(ang_c) / math.sqrt(n)
    sc = np.sin(ang_c) / math.sqrt(n)
    cs = np.concatenate([np.cos(ang_l), -np.sin(ang_l)], axis=1) / math.sqrt(L)
    return (jnp.asarray(cc, F32).astype(BF16), jnp.asarray(sc, F32).astype(BF16),
            jnp.asarray(cs, F32).astype(BF16))


def _four_call(x, mod, l, i_odd, gain, w, b, *, L, nb, blk0):
    nblk = N_TOK // L
    cc, sc, cs = _dft_tables(L)
    row = _mod_row(L)
    out = pl.pallas_call(
        functools.partial(_four_kernel, L=L),
        grid=(nb,),
        in_specs=[
            pl.BlockSpec((None, L, D_MODEL), lambda i: (blk0 + i, 0, 0)),
            pl.BlockSpec((None, None, 6, D_MODEL), lambda i: (l, row(blk0 + i), 0, 0)),
            pl.BlockSpec((1, D_MODEL), lambda i: (0, 0)),
            pl.BlockSpec((FOUR_GC, FOUR_GC), lambda i: (0, 0)),
            pl.BlockSpec((FOUR_GC, FOUR_GC), lambda i: (0, 0)),
            pl.BlockSpec((L, 2 * L), lambda i: (0, 0)),
            pl.BlockSpec((None, D_MODEL, D_MODEL), lambda i: (i_odd, 0, 0)),
            pl.BlockSpec((1, D_MODEL), lambda i: (0, 0)),
        ],
        out_specs=pl.BlockSpec((None, L, D_MODEL), lambda i: (i, 0, 0)),
        out_shape=jax.ShapeDtypeStruct((nb, L, D_MODEL), F32),
        scratch_shapes=[pltpu.VMEM((2 * L, D_MODEL), BF16)],
        compiler_params=_params("parallel"),
        name=f"fourier_L{L}",
    )(x.reshape(nblk, L, D_MODEL), mod, gain, cc, sc, cs, w, b)
    return out.reshape(nb * L, D_MODEL)


def _rope_lane_tables(L):
    t = jnp.arange(L)
    row = (t // GRID_W).astype(F32)
    col = (t % GRID_W).astype(F32)
    freq = ROPE_THETA ** (-jnp.arange(ROPE_PAIRS, dtype=F32) / ROPE_PAIRS)
    ar = row[:, None] * freq[None]
    ac = col[:, None] * freq[None]
    z = jnp.zeros_like(ar)
    cos32 = lambda a: jnp.concatenate([jnp.cos(a), jnp.cos(a)], axis=-1)
    sa32 = lambda a: jnp.concatenate([-jnp.sin(a), z], axis=-1)
    sb32 = lambda a: jnp.concatenate([z, jnp.sin(a)], axis=-1)
    tile = lambda f: jnp.concatenate([f(ar), f(ac), f(ar), f(ac)], axis=-1)
    return tile(cos32), tile(sa32), tile(sb32)


def _pad_lanes(v, width):
    v = v.reshape(1, -1).astype(F32)
    return jnp.pad(v, ((0, 0), (0, width - v.shape[1])))


def kernel(x_prompt, x_sample, cache_k, cache_v, state_ssd_fwd, state_ssd_bwd, c, c_ctx, w_ada, b_ada, norm_mix, norm_ffn, w_in_ab, conv_w, conv_b, dt_bias, a_log, d_skip, ssd_norm, lambda_qk, subln, w_out_ab, w_four, b_four, w_ffn_in, w_ffn_out, norm_final):
    x = (x_prompt.reshape(P_TOK, D_MODEL), x_sample.reshape(S_TOK, D_MODEL))
    cvec = jnp.concatenate([c_ctx[None, :], c, jnp.zeros((N_MOD - 1 - DEC_BATCH, D_MODEL), F32)], axis=0)
    mod = _ada_call(cvec, w_ada, b_ada).reshape(DEPTH, N_MOD, 6, D_MODEL)
    rope = _rope_lane_tables(DEC_SEQ)

    o2, o3 = SSD_INNER + CONV_CH, SSD_INNER + CONV_CH + 2 * SSD_HEADS
    w_in_bf = (w_in_ab[:, :, :o2].astype(BF16), w_in_ab[:, :, o3:].astype(BF16),
               jnp.pad(w_in_ab[:, :, o2:o3], ((0, 0), (0, 0), (0, DT_W - (o3 - o2)))).astype(BF16))
    w_out_bf = w_out_ab.astype(BF16)
    w_four_bf = w_four.astype(BF16)

    kv_new = None
    states = "new"
    blk_s = P_TOK // DEC_SEQ
    for l in range(DEPTH):
        gain_mix = norm_mix[l].reshape(1, D_MODEL)
        if l % 2 == 0:
            e = l // 2
            lam_init = 0.8 - 0.6 * math.exp(-0.3 * l)
            xbc, z, pb, dtarr, k_new, v_new = _inproj_call(x, mod, l, e, gain_mix, w_in_bf, kv_new)
            kv_new = (k_new, v_new)

            ssd_w = (jnp.pad(conv_w[e], ((0, 8 - CONV_W), (0, 0))), conv_b[e].reshape(1, CONV_CH),
                     _pad_lanes(dt_bias[e], DT_W), _pad_lanes(a_log[e], DT_W),
                     jnp.repeat(d_skip[e], SSD_HD).reshape(1, SSD_INNER),
                     ssd_norm[e].reshape(1, SSD_INNER))
            sub = subln[e].reshape(1, ATT_VD)
            ((y_p, hf, hb),) = _launch(
                [_ssd_parts(xbc, z, dtarr, states, ssd_w, e, L=SEQ, nb=BATCH, blk0=0)],
                (BATCH,), ("parallel",), "ssd_L256")
            states = (hf, hb)
            ((o_p,),) = _launch(
                [_attn_parts(pb, lambda_qk[e], sub, lam_init, L=SEQ, nb=BATCH, blk0=0, tq=SEQ)],
                (BATCH,), ("parallel",), "attn_L256")
            ((y_s,),) = _launch(
                [_ssd_parts(xbc, z, dtarr, None, ssd_w, e, L=DEC_SEQ, nb=DEC_BATCH, blk0=blk_s,
                            h0=(state_ssd_fwd, state_ssd_bwd))],
                (DEC_BATCH,), ("parallel",), "ssd_L1024")
            ((o_s,),) = _launch(
                [_attn_parts(pb, lambda_qk[e], sub, lam_init, L=DEC_SEQ, nb=DEC_BATCH, blk0=blk_s,
                             tq=ATT_TQ, ctx=(cache_k, cache_v, e), rope=rope)],
                (DEC_BATCH, DEC_SEQ // ATT_TQ), ("parallel", "arbitrary"), "attn_L1024")
            halves = lambda a, b: (a.reshape(P_TOK, -1), b.reshape(S_TOK, -1))
            mixer = (halves(y_p, y_s), halves(o_p, o_s), w_out_bf, e)
        else:
            mixer = None
            i_odd = l // 2
            b4 = b_four[i_odd].reshape(1, D_MODEL)
            x = (_four_call(x[0], mod, l, i_odd, gain_mix, w_four_bf, b4, L=SEQ, nb=BATCH, blk0=0),
                 _four_call(x[0], mod, l, i_odd, gain_mix, w_four_bf, b4, L=DEC_SEQ, nb=DEC_BATCH,
                            blk0=blk_s))
        last = l == DEPTH - 1
        x = _ffn_call(x, mod, l, norm_ffn[l].reshape(1, D_MODEL), w_ffn_in, w_ffn_out,
                      final_gain=norm_final.reshape(1, D_MODEL) if last else None, mixer=mixer)
        x = tuple(x) if last else (x,)

    y_p, y_s = x
    return (y_p.reshape(BATCH, SEQ, D_MODEL), y_s.reshape(DEC_BATCH, DEC_SEQ, D_MODEL),
            kv_new[0].reshape(BATCH, N_EVEN, SEQ, ATT_HEADS, 2, ATT_HD),
            kv_new[1].reshape(BATCH, N_EVEN, SEQ, ATT_HEADS, ATT_VD),
            states[0].reshape(BATCH, N_EVEN, SSD_HEADS, SSD_HD, D_STATE),
            states[1].reshape(BATCH, N_EVEN, SSD_HEADS, SSD_HD, D_STATE))
```
